```python
import jax
import jax.numpy as jnp
from jax import lax
import numpy as np

D_MODEL = 1024
BATCH = 2
SEQ = 16384
DEPTH = 2

EPS = 1e-6
CONV_WIDTH = 4
LRU_WIDTH = D_MODEL // 2
LRU_BLOCKS = 8
LRU_BLOCK = LRU_WIDTH // LRU_BLOCKS
LRU_C = 8.0
ATT_HEAD_DIM = 64
ATT_Q_HEADS = (D_MODEL // 2) // ATT_HEAD_DIM
ATT_KV_HEADS = 2
ATT_GROUPS = ATT_Q_HEADS // ATT_KV_HEADS
ATT_Q_WIDTH = ATT_Q_HEADS * ATT_HEAD_DIM
ATT_KV_WIDTH = ATT_KV_HEADS * ATT_HEAD_DIM
WINDOW = 128
ROPE_THETA = 10000.0
EVEN_IN_COLS = 2 * LRU_WIDTH + ATT_Q_WIDTH + 2 * ATT_KV_WIDTH
EVEN_MIX_WIDTH = LRU_WIDTH + ATT_Q_WIDTH
D_FF_DENSE = 2816
GDN_HEADS = 8
GDN_HEAD_DIM = 128
GDN_WIDTH = GDN_HEADS * GDN_HEAD_DIM
GDN_CHUNK = 64
ODD_IN_COLS = 4 * GDN_WIDTH + 2 * GDN_HEADS
N_EXPERTS = 8
TOP_K = 2
D_FF_EXPERT = 3584
N_EVEN = (DEPTH + 1) // 2
N_ODD = DEPTH // 2

kernel_name = "hybrid_rglru_swa_gdn_moe"


def rms_norm(x, g):
    xf = x.astype(jnp.float32)
    y = xf * lax.rsqrt(jnp.mean(xf * xf, axis=-1, keepdims=True) + EPS)
    return (y * g.astype(jnp.float32)).astype(x.dtype)


def l2_norm(x):
    return x * lax.rsqrt(jnp.sum(x * x, axis=-1, keepdims=True) + EPS)


def causal_depthwise_conv(x, w):
    seq = x.shape[1]
    xp = jnp.pad(x, ((0, 0), (CONV_WIDTH - 1, 0), (0, 0)))
    return sum(w[j] * xp[:, j:j + seq] for j in range(CONV_WIDTH))


def rope(x, pos):
    half = x.shape[-1] // 2
    inv_freq = ROPE_THETA ** (-jnp.arange(half, dtype=jnp.float32) / half)
    ang = pos.astype(jnp.float32)[:, None] * inv_freq[None, :]
    cos = jnp.cos(ang)[None, :, None, :]
    sin = jnp.sin(ang)[None, :, None, :]
    xf = x.astype(jnp.float32)
    x1, x2 = xf[..., :half], xf[..., half:]
    return jnp.concatenate([x1 * cos - x2 * sin, x2 * cos + x1 * sin], axis=-1)


def swiglu(h, w_gate, w_up, w_down):
    return (jax.nn.silu(h @ w_gate) * (h @ w_up)) @ w_down


def rg_lru(xb, conv_w, conv_b, w_a, b_a, w_i, b_i, lam):
    bsz, seq, _ = xb.shape
    xc = causal_depthwise_conv(xb, conv_w) + conv_b
    xh = xc.reshape(bsz, seq, LRU_BLOCKS, LRU_BLOCK)
    r = jax.nn.sigmoid(jnp.einsum('bshi,hij->bshj', xh, w_a).reshape(bsz, seq, LRU_WIDTH) + b_a)
    gi = jax.nn.sigmoid(jnp.einsum('bshi,hij->bshj', xh, w_i).reshape(bsz, seq, LRU_WIDTH) + b_i)
    log_a = -LRU_C * r * jax.nn.softplus(-lam)
    a = jnp.exp(log_a)
    u = jnp.sqrt(-jnp.expm1(2.0 * log_a)) * (gi * xc)

    def combine(c1, c2):
        a1, b1 = c1
        a2, b2 = c2
        return a1 * a2, a2 * b1 + b2

    _, h = lax.associative_scan(combine, (a, u), axis=1)
    return h


def sliding_window_attention(q, k, v, sinks):
    bsz, seq = q.shape[:2]
    nb = seq // WINDOW
    qb = q.astype(jnp.float32).reshape(bsz, nb, WINDOW, ATT_KV_HEADS, ATT_GROUPS, ATT_HEAD_DIM)

    def with_prev(t):
        tb = t.astype(jnp.float32).reshape(bsz, nb, WINDOW, ATT_KV_HEADS, ATT_HEAD_DIM)
        prev = jnp.pad(tb, ((0, 0), (1, 0), (0, 0), (0, 0), (0, 0)))[:, :-1]
        return jnp.concatenate([prev, tb], axis=2)

    kb, vb = with_prev(k), with_prev(v)
    s = jnp.einsum('bnqhgd,bnkhd->bnhgqk', qb, kb) * (ATT_HEAD_DIM ** -0.5)
    qi = jnp.arange(WINDOW)[:, None] + WINDOW
    kj = jnp.arange(2 * WINDOW)[None, :]
    rel = qi - kj
    band = (rel >= 0) & (rel < WINDOW)
    valid = band[None] & ((jnp.arange(nb)[:, None, None] > 0) | (kj >= WINDOW)[None])
    s = jnp.where(valid[None, :, None, None], s, -1e30)
    sink = sinks.astype(jnp.float32).reshape(ATT_KV_HEADS, ATT_GROUPS)[None, None, :, :, None, None]
    m = jnp.maximum(jnp.max(s, axis=-1, keepdims=True), sink)
    p = jnp.exp(s - m)
    denom = jnp.sum(p, axis=-1, keepdims=True) + jnp.exp(sink - m)
    o = jnp.einsum('bnhgqk,bnkhd->bnqhgd', p / denom, vb)
    return o.reshape(bsz, seq, ATT_Q_WIDTH)


def even_mixer(h, pos, w_in, conv_w, conv_b, w_a, b_a, w_i, b_i, lam, q_norm, k_norm, sinks, w_out):
    bsz, seq, _ = h.shape
    proj = h @ w_in
    cuts = np.cumsum([LRU_WIDTH, LRU_WIDTH, ATT_Q_WIDTH, ATT_KV_WIDTH]).tolist()
    xb, gate, q, k, v = jnp.split(proj, cuts, axis=-1)
    y_lru = rg_lru(xb.astype(jnp.float32), conv_w, conv_b, w_a, b_a, w_i, b_i, lam) \
        * jax.nn.gelu(gate.astype(jnp.float32))
    q = rope(rms_norm(q.reshape(bsz, seq, ATT_Q_HEADS, ATT_HEAD_DIM), q_norm), pos)
    k = rope(rms_norm(k.reshape(bsz, seq, ATT_KV_HEADS, ATT_HEAD_DIM), k_norm), pos)
    v = v.reshape(bsz, seq, ATT_KV_HEADS, ATT_HEAD_DIM)
    y_att = sliding_window_attention(q, k, v, sinks)
    y = jnp.concatenate([y_lru.astype(h.dtype), y_att.astype(h.dtype)], axis=-1)
    return y @ w_out


def gated_deltanet(h, w_in, conv_w, a_log, dt_bias, out_norm, w_out):
    bsz, seq, _ = h.shape
    nc = seq // GDN_CHUNK
    C = GDN_CHUNK
    proj = (h @ w_in).astype(jnp.float32)
    cuts = [3 * GDN_WIDTH, 4 * GDN_WIDTH, 4 * GDN_WIDTH + GDN_HEADS]
    qkv, gate, a_in, b_in = jnp.split(proj, cuts, axis=-1)
    qkv = jax.nn.silu(causal_depthwise_conv(qkv, conv_w.astype(jnp.float32)))
    q, k, v = jnp.split(qkv, 3, axis=-1)
    q = l2_norm(q.reshape(bsz, seq, GDN_HEADS, GDN_HEAD_DIM)) * (GDN_HEAD_DIM ** -0.5)
    k = l2_norm(k.reshape(bsz, seq, GDN_HEADS, GDN_HEAD_DIM))
    v = v.reshape(bsz, seq, GDN_HEADS, GDN_HEAD_DIM)
    beta = jax.nn.sigmoid(b_in)
    g = -jnp.exp(a_log.astype(jnp.float32)) * jax.nn.softplus(a_in + dt_bias.astype(jnp.float32))

    def to_chunks(t):
        return t.reshape(bsz, nc, C, GDN_HEADS, -1).transpose(0, 3, 1, 2, 4)

    qc, kc, vc = to_chunks(q), to_chunks(k), to_chunks(v)
    gc = g.reshape(bsz, nc, C, GDN_HEADS).transpose(0, 3, 1, 2)
    bc = beta.reshape(bsz, nc, C, GDN_HEADS).transpose(0, 3, 1, 2)
    G = jnp.cumsum(gc, axis=-1)
    tri = jnp.tril(jnp.ones((C, C), dtype=bool))
    strict = jnp.tril(jnp.ones((C, C), dtype=bool), -1)
    diff = G[..., :, None] - G[..., None, :]
    decay = jnp.where(tri, jnp.exp(jnp.where(tri, diff, 0.0)), 0.0)
    kk = jnp.einsum('bhnid,bhnjd->bhnij', kc, kc)
    A = jnp.where(strict, bc[..., :, None] * kk * decay, 0.0) + jnp.eye(C, dtype=jnp.float32)
    u = lax.linalg.triangular_solve(A, bc[..., None] * vc, left_side=True, lower=True, unit_diagonal=True)
    w = lax.linalg.triangular_solve(A, bc[..., None] * jnp.exp(G)[..., None] * kc,
                                    left_side=True, lower=True, unit_diagonal=True)
    a_qk = jnp.where(tri, jnp.einsum('bhnid,bhnjd->bhnij', qc, kc) * decay, 0.0)
    q_dec = qc * jnp.exp(G)[..., None]
    k_dec = kc * jnp.exp(G[..., -1:] - G)[..., None]
    g_last = jnp.exp(G[..., -1])
    xs = (jnp.moveaxis(u, 2, 0), jnp.moveaxis(w, 2, 0), jnp.moveaxis(a_qk, 2, 0),
          jnp.moveaxis(q_dec, 2, 0), jnp.moveaxis(k_dec, 2, 0), jnp.moveaxis(g_last, 2, 0))

    def step(state, inp):
        u_c, w_c, a_c, qd_c, kd_c, gl_c = inp
        v_new = u_c - jnp.einsum('bhcd,bhde->bhce', w_c, state)
        o_c = jnp.einsum('bhcd,bhde->bhce', qd_c, state) + jnp.einsum('bhcj,bhje->bhce', a_c, v_new)
        state = state * gl_c[..., None, None] + jnp.einsum('bhcd,bhce->bhde', kd_c, v_new)
        return state, o_c

    s0 = jnp.zeros((bsz, GDN_HEADS, GDN_HEAD_DIM, GDN_HEAD_DIM), jnp.float32)
    _, o = lax.scan(step, s0, xs)
    o = o.transpose(1, 0, 3, 2, 4).reshape(bsz, seq, GDN_HEADS, GDN_HEAD_DIM)
    o = rms_norm(o, out_norm) * jax.nn.silu(gate.reshape(bsz, seq, GDN_HEADS, GDN_HEAD_DIM))
    return o.reshape(bsz, seq, GDN_WIDTH).astype(h.dtype) @ w_out


def moe_swiglu(h, router, w_gate, w_up, w_down):
    bsz, seq, d = h.shape
    t = h.reshape(-1, d)
    logits = (t @ router).astype(jnp.float32)
    top_v, top_i = lax.top_k(logits, TOP_K)
    top_w = jax.nn.softmax(top_v, axis=-1)
    gates = jnp.sum(jax.nn.one_hot(top_i, N_EXPERTS, dtype=jnp.float32) * top_w[..., None], axis=1)
    out = jnp.zeros(t.shape, jnp.float32)
    for e in range(N_EXPERTS):
        y_e = swiglu(t, w_gate[e], w_up[e], w_down[e]).astype(jnp.float32)
        out = out + gates[:, e:e + 1] * y_e
    return out.astype(h.dtype).reshape(bsz, seq, d)


def setup_inputs(seed: int = 0) -> dict:
    key = jax.random.key(seed)
    ks = jax.random.split(key, 32)
    f32 = jnp.float32
    D = D_MODEL

    def nrm(k, shape, scale):
        return scale * jax.random.normal(k, shape, f32)

    lru_u = jax.random.uniform(ks[10], (N_EVEN, LRU_WIDTH), f32, minval=0.9, maxval=0.999)
    lru_s = lru_u ** (1.0 / LRU_C)
    dt = jnp.exp(jax.random.uniform(ks[21], (N_ODD, GDN_HEADS), f32,
                                    minval=float(np.log(0.001)), maxval=float(np.log(0.1))))
    return {
        "x": nrm(ks[0], (BATCH, SEQ, D), 1.0),
        "ln_mix": 1.0 + nrm(ks[1], (DEPTH, D), 0.02),
        "ln_ffn": 1.0 + nrm(ks[2], (DEPTH, D), 0.02),
        "e_w_in": nrm(ks[3], (N_EVEN, D, EVEN_IN_COLS), D ** -0.5),
        "e_lru_conv_w": nrm(ks[4], (N_EVEN, CONV_WIDTH, LRU_WIDTH), CONV_WIDTH ** -0.5),
        "e_lru_conv_b": nrm(ks[5], (N_EVEN, LRU_WIDTH), 0.02),
        "e_lru_w_a": nrm(ks[6], (N_EVEN, LRU_BLOCKS, LRU_BLOCK, LRU_BLOCK), LRU_BLOCK ** -0.5),
        "e_lru_b_a": nrm(ks[7], (N_EVEN, LRU_WIDTH), 0.1),
        "e_lru_w_i": nrm(ks[8], (N_EVEN, LRU_BLOCKS, LRU_BLOCK, LRU_BLOCK), LRU_BLOCK ** -0.5),
        "e_lru_b_i": nrm(ks[9], (N_EVEN, LRU_WIDTH), 0.1),
        "e_lru_lambda": jnp.log(lru_s) - jnp.log1p(-lru_s),
        "e_q_norm": 1.0 + nrm(ks[11], (N_EVEN, ATT_HEAD_DIM), 0.02),
        "e_k_norm": 1.0 + nrm(ks[12], (N_EVEN, ATT_HEAD_DIM), 0.02),
        "e_sinks": nrm(ks[13], (N_EVEN, ATT_Q_HEADS), 1.0),
        "e_w_out": nrm(ks[14], (N_EVEN, EVEN_MIX_WIDTH, D), EVEN_MIX_WIDTH ** -0.5),
        "e_ffn_w_gate": nrm(ks[15], (N_EVEN, D, D_FF_DENSE), D ** -0.5),
        "e_ffn_w_up": nrm(ks[16], (N_EVEN, D, D_FF_DENSE), D ** -0.5),
        "e_ffn_w_down": nrm(ks[17], (N_EVEN, D_FF_DENSE, D), D_FF_DENSE ** -0.5),
        "o_w_in": nrm(ks[18], (N_ODD, D, ODD_IN_COLS), D ** -0.5),
        "o_conv_w": nrm(ks[19], (N_ODD, CONV_WIDTH, 3 * GDN_WIDTH), CONV_WIDTH ** -0.5),
        "o_a_log": jnp.log(jax.random.uniform(ks[20], (N_ODD, GDN_HEADS), f32, minval=1.0, maxval=16.0)),
        "o_dt_bias": dt + jnp.log(-jnp.expm1(-dt)),
        "o_out_norm": 1.0 + nrm(ks[22], (N_ODD, GDN_HEAD_DIM), 0.02),
        "o_w_out": nrm(ks[23], (N_ODD, GDN_WIDTH, D), GDN_WIDTH ** -0.5),
        "o_router": nrm(ks[24], (N_ODD, D, N_EXPERTS), D ** -0.5),
        "o_moe_w_gate": nrm(ks[25], (N_ODD, N_EXPERTS, D, D_FF_EXPERT), D ** -0.5),
        "o_moe_w_up": nrm(ks[26], (N_ODD, N_EXPERTS, D, D_FF_EXPERT), D ** -0.5),
        "o_moe_w_down": nrm(ks[27], (N_ODD, N_EXPERTS, D_FF_EXPERT, D), D_FF_EXPERT ** -0.5),
    }


def reference(x, ln_mix, ln_ffn,
              e_w_in, e_lru_conv_w, e_lru_conv_b, e_lru_w_a, e_lru_b_a, e_lru_w_i, e_lru_b_i,
              e_lru_lambda, e_q_norm, e_k_norm, e_sinks, e_w_out, e_ffn_w_gate, e_ffn_w_up, e_ffn_w_down,
              o_w_in, o_conv_w, o_a_log, o_dt_bias, o_out_norm, o_w_out, o_router,
              o_moe_w_gate, o_moe_w_up, o_moe_w_down):
    pos = jnp.arange(x.shape[1], dtype=jnp.int32)
    h = x
    for layer in range(DEPTH):
        j = layer // 2
        hn = rms_norm(h, ln_mix[layer])
        if layer % 2 == 0:
            h = h + even_mixer(hn, pos, e_w_in[j], e_lru_conv_w[j], e_lru_conv_b[j], e_lru_w_a[j],
                               e_lru_b_a[j], e_lru_w_i[j], e_lru_b_i[j], e_lru_lambda[j],
                               e_q_norm[j], e_k_norm[j], e_sinks[j], e_w_out[j]).astype(h.dtype)
            h = h + swiglu(rms_norm(h, ln_ffn[layer]), e_ffn_w_gate[j], e_ffn_w_up[j],
                           e_ffn_w_down[j]).astype(h.dtype)
        else:
            h = h + gated_deltanet(hn, o_w_in[j], o_conv_w[j], o_a_log[j], o_dt_bias[j],
                                   o_out_norm[j], o_w_out[j]).astype(h.dtype)
            h = h + moe_swiglu(rms_norm(h, ln_ffn[layer]), o_router[j], o_moe_w_gate[j],
                               o_moe_w_up[j], o_moe_w_down[j]).astype(h.dtype)
    return h
```

```python
import functools

import jax
import jax.numpy as jnp
from jax import lax
from jax.scipy.linalg import block_diag
from jax.experimental import pallas as pl
from jax.experimental.pallas import tpu as pltpu

F32 = jnp.float32
BF16 = jnp.bfloat16
I32 = jnp.int32

EPS = 1e-6
LANES = 128
SUBLANES = 8
VMEM_LIMIT = 52 * 1024 * 1024

CONV_WIDTH = 4
LRU_C = 8.0
ATT_HEAD_DIM = 64
ATT_WINDOW = 128
ROPE_THETA = 10000.0
GDN_HEAD_DIM = 128
GDN_CHUNK = 64
N_EXPERTS = 8
MOE_TILE = 512
MOE_FF_CHUNK = 512


def _params(n_axes, vmem=VMEM_LIMIT):
    return pltpu.CompilerParams(dimension_semantics=("arbitrary",) * n_axes, vmem_limit_bytes=vmem)


def _rms(x, g):
    return x * lax.rsqrt(jnp.mean(x * x, axis=-1, keepdims=True) + EPS) * g


def _dot(a, b):
    return jnp.dot(a, b, preferred_element_type=F32)


def _dot_nt(a, b):
    return lax.dot_general(a, b, (((1,), (1,)), ((), ())), preferred_element_type=F32)


def _split_bf16(x):
    hi = x.astype(BF16)
    lo = (x - hi.astype(F32)).astype(BF16)
    return hi, lo


def _rms_matmul_kernel(x_ref, g_ref, w_ref, o_ref, xn_ref):
    @pl.when(pl.program_id(1) == 0)
    def _():
        xn_ref[...] = _rms(x_ref[...], g_ref[...]).astype(BF16)

    o_ref[...] = _dot(xn_ref[...], w_ref[...])


def rms_matmul(x, g, w, tm, tn):
    t, d = x.shape
    n = w.shape[1]
    return pl.pallas_call(
        _rms_matmul_kernel,
        grid=(t // tm, n // tn),
        in_specs=[pl.BlockSpec((tm, d), lambda i, j: (i, 0)),
                  pl.BlockSpec((1, d), lambda i, j: (0, 0)),
                  pl.BlockSpec((d, tn), lambda i, j: (0, j))],
        out_specs=pl.BlockSpec((tm, tn), lambda i, j: (i, j)),
        out_shape=jax.ShapeDtypeStruct((t, n), F32),
        scratch_shapes=[pltpu.VMEM((tm, d), BF16)],
        compiler_params=_params(2),
    )(x, g.reshape(1, d), w)


def _causal_conv(ext_ref, x, cw, tm):
    ext_ref[SUBLANES:SUBLANES + tm, :] = x
    base = SUBLANES - (CONV_WIDTH - 1)
    acc = cw[0:1, :] * ext_ref[base:base + tm, :]
    for j in range(1, CONV_WIDTH):
        acc = acc + cw[j:j + 1, :] * ext_ref[base + j:base + j + tm, :]
    ext_ref[0:SUBLANES, :] = x[tm - SUBLANES:tm, :]
    return acc


def _rglru_kernel(xb_ref, gate_ref, cw_ref, cb_ref, wg_ref, bg_ref, lam_ref, o_ref, ext_ref, h_ref,
                  *, tm, width):
    @pl.when(pl.program_id(1) == 0)
    def _():
        ext_ref[0:SUBLANES, :] = jnp.zeros((SUBLANES, width), F32)
        h_ref[...] = jnp.zeros_like(h_ref)

    xc = _causal_conv(ext_ref, xb_ref[...], cw_ref[...], tm) + cb_ref[...]
    z = _dot(xc.astype(BF16), wg_ref[...]) + bg_ref[...]
    r = jax.nn.sigmoid(z[:, :width])
    gi = jax.nn.sigmoid(z[:, width:])
    log_a = -LRU_C * r * jax.nn.softplus(-lam_ref[...])
    a = jnp.exp(log_a)
    u = jnp.sqrt(1.0 - a * a) * (gi * xc)

    row = lax.broadcasted_iota(I32, (tm, width), 0)
    d = 1
    while d < tm:
        keep = row >= d
        u = u + jnp.where(keep, a * pltpu.roll(u, d, 0), 0.0)
        a = jnp.where(keep, a * pltpu.roll(a, d, 0), a)
        d *= 2
    h = u + a * h_ref[0:1, :]
    h_ref[...] = jnp.broadcast_to(h[tm - 1:tm, :], h_ref.shape)
    o_ref[...] = (h * jax.nn.gelu(gate_ref[...])).astype(o_ref.dtype)


def rglru(proj, cw, cb, wg, bg, lam, bsz, seq, width, tm):
    nt = seq // tm
    kern = functools.partial(_rglru_kernel, tm=tm, width=width)
    row = lambda c: pl.BlockSpec((1, c), lambda b, i: (0, 0))
    return pl.pallas_call(
        kern,
        grid=(bsz, nt),
        in_specs=[pl.BlockSpec((tm, width), lambda b, i: (b * nt + i, 0)),
                  pl.BlockSpec((tm, width), lambda b, i: (b * nt + i, 1)),
                  pl.BlockSpec((CONV_WIDTH, width), lambda b, i: (0, 0)),
                  row(width),
                  pl.BlockSpec((width, 2 * width), lambda b, i: (0, 0)),
                  row(2 * width),
                  row(width)],
        out_specs=pl.BlockSpec((tm, width), lambda b, i: (b * nt + i, 0)),
        out_shape=jax.ShapeDtypeStruct((bsz * seq, width), BF16),
        scratch_shapes=[pltpu.VMEM((tm + SUBLANES, width), F32), pltpu.VMEM((SUBLANES, width), F32)],
        compiler_params=_params(2),
    )(proj, proj, cw, cb.reshape(1, -1), wg, bg.reshape(1, -1), lam.reshape(1, -1))


def _swa_kernel(sink_ref, q_ref, k_ref, v_ref, cos_ref, sin_ref, qg_ref, kg_ref, o_ref, kall_ref, vall_ref,
                *, q_heads, kv_heads):
    w = ATT_WINDOW
    n = pl.program_id(1)
    group = q_heads // kv_heads
    assert kv_heads * ATT_HEAD_DIM == LANES and group % 2 == 0

    @pl.when(n == 0)
    def _():
        kall_ref[0:w, :] = jnp.zeros((w, LANES), BF16)
        vall_ref[0:w, :] = jnp.zeros((w, LANES), BF16)

    @pl.when(n > 0)
    def _():
        kall_ref[0:w, :] = kall_ref[w:2 * w, :]
        vall_ref[0:w, :] = vall_ref[w:2 * w, :]

    lane = lax.broadcasted_iota(I32, (w, LANES), 1)
    first_half = (lane % ATT_HEAD_DIM) < (ATT_HEAD_DIM // 2)
    lane_head = lane // ATT_HEAD_DIM
    bd_r = lax.broadcasted_iota(I32, (LANES, LANES), 0) // ATT_HEAD_DIM
    bd_c = lax.broadcasted_iota(I32, (LANES, LANES), 1) // ATT_HEAD_DIM
    head_ones = (bd_r == bd_c).astype(BF16)
    cos = cos_ref[...]
    sin = sin_ref[...]

    def norm_rope(x, g):
        hi, lo = _split_bf16(x * x)
        ss = _dot(hi, head_ones) + _dot(lo, head_ones)
        xn = x * lax.rsqrt(ss * (1.0 / ATT_HEAD_DIM) + EPS) * g
        rot = jnp.where(first_half, pltpu.roll(xn, LANES - ATT_HEAD_DIM // 2, 1),
                        pltpu.roll(xn, ATT_HEAD_DIM // 2, 1))
        return xn * cos + rot * sin

    kall_ref[w:2 * w, :] = norm_rope(k_ref[...], kg_ref[...]).astype(BF16)
    vall_ref[w:2 * w, :] = v_ref[...].astype(BF16)
    qc = [norm_rope(q_ref[:, LANES * c:LANES * (c + 1)], qg_ref[...]) for c in range(q_heads // 2)]

    rows = group * w
    qi = lax.broadcasted_iota(I32, (rows, 2 * w), 0) % w
    kj = lax.broadcasted_iota(I32, (rows, 2 * w), 1)
    valid = (kj > qi) & (kj <= qi + w) & ((n > 0) | (kj >= w))
    row_head = lax.broadcasted_iota(I32, (rows, 1), 0) // w

    outs = []
    for g in range(kv_heads):
        parts = []
        for hh in range(group):
            h = g * group + hh
            x = qc[h // 2]
            if h % 2 != g:
                x = pltpu.roll(x, ATT_HEAD_DIM, 1)
            parts.append(jnp.where(lane_head == g, x, 0.0))
        qs = jnp.concatenate(parts, axis=0).astype(BF16)
        s = _dot_nt(qs, kall_ref[...]) * (ATT_HEAD_DIM ** -0.5)
        s = jnp.where(valid, s, -1e30)
        sink = jnp.zeros((rows, 1), F32)
        for hh in range(group):
            sink = jnp.where(row_head == hh, sink_ref[g * group + hh], sink)
        m = jnp.maximum(jnp.max(s, axis=-1, keepdims=True), sink)
        p = jnp.exp(s - m)
        denom = jnp.sum(p, axis=-1, keepdims=True) + jnp.exp(sink - m)
        pv = _dot(p.astype(BF16), vall_ref[...]) / denom
        outs.extend(pv[w * hh:w * (hh + 1), :] for hh in range(group))

    for c in range(q_heads // 2):
        g = (2 * c) // group
        a, b = outs[2 * c], outs[2 * c + 1]
        if g == 1:
            a = pltpu.roll(a, ATT_HEAD_DIM, 1)
        else:
            b = pltpu.roll(b, ATT_HEAD_DIM, 1)
        o_ref[:, LANES * c:LANES * (c + 1)] = jnp.where(lane_head == 0, a, b).astype(o_ref.dtype)


def swa(proj, sinks, cos, sin, qg, kg, bsz, seq, q_col, k_col, v_col, q_heads, kv_heads):
    w = ATT_WINDOW
    nb = seq // w
    qw = q_heads * ATT_HEAD_DIM
    kern = functools.partial(_swa_kernel, q_heads=q_heads, kv_heads=kv_heads)
    return pl.pallas_call(
        kern,
        grid=(bsz, nb),
        in_specs=[pl.BlockSpec(memory_space=pltpu.SMEM),
                  pl.BlockSpec((w, qw), lambda b, i: (b * nb + i, q_col // qw)),
                  pl.BlockSpec((w, LANES), lambda b, i: (b * nb + i, k_col // LANES)),
                  pl.BlockSpec((w, LANES), lambda b, i: (b * nb + i, v_col // LANES)),
                  pl.BlockSpec((w, LANES), lambda b, i: (i, 0)),
                  pl.BlockSpec((w, LANES), lambda b, i: (i, 0)),
                  pl.BlockSpec((1, LANES), lambda b, i: (0, 0)),
                  pl.BlockSpec((1, LANES), lambda b, i: (0, 0))],
        out_specs=pl.BlockSpec((w, qw), lambda b, i: (b * nb + i, 0)),
        out_shape=jax.ShapeDtypeStruct((bsz * seq, qw), BF16),
        scratch_shapes=[pltpu.VMEM((2 * w, LANES), BF16), pltpu.VMEM((2 * w, LANES), BF16)],
        compiler_params=_params(2),
    )(sinks, proj, proj, proj, cos, sin, qg, kg)


def _proj_res_kernel(*refs, n_in):
    h_ref, o_ref = refs[0], refs[-1]
    acc = h_ref[...]
    for y_ref, w_ref in zip(refs[1:1 + n_in], refs[1 + n_in:1 + 2 * n_in]):
        acc = acc + _dot(y_ref[...], w_ref[...])
    o_ref[...] = acc


def proj_res(h, ys, ws, tm):
    t, d = h.shape
    kern = functools.partial(_proj_res_kernel, n_in=len(ys))
    return pl.pallas_call(
        kern,
        grid=(t // tm,),
        in_specs=([pl.BlockSpec((tm, d), lambda i: (i, 0))]
                  + [pl.BlockSpec((tm, y.shape[1]), lambda i: (i, 0)) for y in ys]
                  + [pl.BlockSpec(w.shape, lambda i: (0, 0)) for w in ws]),
        out_specs=pl.BlockSpec((tm, d), lambda i: (i, 0)),
        out_shape=jax.ShapeDtypeStruct((t, d), F32),
        compiler_params=_params(1),
    )(h, *ys, *ws)


def _ffn_kernel(h_ref, g_ref, wg_ref, wu_ref, wd_ref, o_ref, xn_ref):
    j = pl.program_id(1)

    @pl.when(j == 0)
    def _():
        x = h_ref[...]
        xn_ref[...] = _rms(x, g_ref[...]).astype(BF16)
        o_ref[...] = x

    xn = xn_ref[...]
    act = (jax.nn.silu(_dot(xn, wg_ref[...])) * _dot(xn, wu_ref[...])).astype(BF16)
    o_ref[...] += _dot(act, wd_ref[...])


def ffn(h, g, wg, wu, wd, tm, tf):
    t, d = h.shape
    f = wg.shape[1]
    return pl.pallas_call(
        _ffn_kernel,
        grid=(t // tm, f // tf),
        in_specs=[pl.BlockSpec((tm, d), lambda i, j: (i, 0)),
                  pl.BlockSpec((1, d), lambda i, j: (0, 0)),
                  pl.BlockSpec((d, tf), lambda i, j: (0, j)),
                  pl.BlockSpec((d, tf), lambda i, j: (0, j)),
                  pl.BlockSpec((tf, d), lambda i, j: (j, 0))],
        out_specs=pl.BlockSpec((tm, d), lambda i, j: (i, 0)),
        out_shape=jax.ShapeDtypeStruct((t, d), F32),
        scratch_shapes=[pltpu.VMEM((tm, d), BF16)],
        compiler_params=_params(2),
    )(h, g.reshape(1, d), wg, wu, wd)


def _gdn_kernel(qkv_ref, gate_ref, ab_ref, cw_ref, alog_ref, dtb_ref, onorm_ref, o_ref,
                ext_ref, q_ref, k_ref, v_ref, gc_ref, beta_ref, state_ref, *, tc, heads):
    hd = GDN_HEAD_DIM
    c = GDN_CHUNK
    width = heads * hd

    @pl.when(pl.program_id(1) == 0)
    def _():
        ext_ref[0:SUBLANES, :] = jnp.zeros((SUBLANES, 3 * width), F32)
        state_ref[...] = jnp.zeros_like(state_ref)

    qkv = jax.nn.silu(_causal_conv(ext_ref, qkv_ref[...], cw_ref[...], tc))
    for h in range(heads):
        q = qkv[:, hd * h:hd * (h + 1)]
        k = qkv[:, width + hd * h:width + hd * (h + 1)]
        q_ref[:, hd * h:hd * (h + 1)] = (q * lax.rsqrt(jnp.sum(q * q, axis=-1, keepdims=True) + EPS)
                                          * (hd ** -0.5))
        k_ref[:, hd * h:hd * (h + 1)] = k * lax.rsqrt(jnp.sum(k * k, axis=-1, keepdims=True) + EPS)
    v_ref[...] = qkv[:, 2 * width:]

    ab = ab_ref[...]
    g = -jnp.exp(alog_ref[...]) * jax.nn.softplus(ab + dtb_ref[...])
    beta_ref[...] = pltpu.roll(jax.nn.sigmoid(ab), LANES - heads, 1)
    row = lax.broadcasted_iota(I32, (tc, LANES), 0) % c
    d = 1
    while d < c:
        g = g + jnp.where(row >= d, pltpu.roll(g, d, 0), 0.0)
        d *= 2
    gc_ref[...] = g

    ri = lax.broadcasted_iota(I32, (c, c), 0)
    ci = lax.broadcasted_iota(I32, (c, c), 1)
    tri = ri >= ci
    strict = ri > ci
    eye = (ri == ci).astype(F32)

    def chunk_body(n, carry):
        r0 = pl.multiple_of(n * c, c)
        rows = pl.ds(r0, c)
        gc_all = gc_ref[rows, :]
        gr_all = gc_all.T
        beta_all = beta_ref[rows, :]
        for h in range(heads):
            cols = slice(hd * h, hd * (h + 1))
            q = q_ref[rows, cols]
            k = k_ref[rows, cols]
            v = v_ref[rows, cols]
            gcol = gc_all[:, h:h + 1]
            grow = gr_all[h:h + 1, :]
            beta = beta_all[:, h:h + 1]
            kb = k.astype(BF16)
            qk_kk = _dot_nt(jnp.concatenate([q, k], axis=0).astype(BF16), kb)
            decay = jnp.where(tri, jnp.exp(jnp.where(tri, gcol - grow, 0.0)), 0.0)
            m = -jnp.where(strict, beta * qk_kk[c:, :] * decay, 0.0)
            p = eye + m
            e = _dot(m.astype(BF16), m.astype(BF16))
            lvl = 2
            while 2 * lvl < c:
                eb = e.astype(BF16)
                r = _dot(jnp.concatenate([p, e], axis=0).astype(BF16), eb)
                p = p + r[:c, :]
                e = r[c:, :]
                lvl *= 2
            p = p + _dot(p.astype(BF16), e.astype(BF16))
            eg = jnp.exp(gcol)
            rhs = jnp.concatenate([beta * v, (beta * eg) * k], axis=1).astype(BF16)
            uw = _dot(p.astype(BF16), rhs)
            a_qk = jnp.where(tri, qk_kk[:c, :] * decay, 0.0)
            g_last = gcol[c - 1:c, :]
            k_dec = k * jnp.exp(g_last - gcol)
            s = state_ref[h]
            sb = s.astype(BF16)
            ws_qs = _dot(jnp.concatenate([uw[:, hd:], q * eg], axis=0).astype(BF16), sb)
            v_new = uw[:, :hd] - ws_qs[:c, :]
            vb = v_new.astype(BF16)
            o = ws_qs[c:, :] + _dot(a_qk.astype(BF16), vb)
            state_ref[h] = s * jnp.exp(g_last) + _dot(k_dec.T.astype(BF16), vb)
            on = o * lax.rsqrt(jnp.mean(o * o, axis=-1, keepdims=True) + EPS) * onorm_ref[...]
            o_ref[rows, cols] = (on * jax.nn.silu(gate_ref[rows, cols])).astype(o_ref.dtype)
        return carry

    lax.fori_loop(0, tc // c, chunk_body, 0)


def gdn(proj, cw, alog, dtb, onorm, bsz, seq, heads, tc):
    nt = seq // tc
    width = heads * GDN_HEAD_DIM
    kern = functools.partial(_gdn_kernel, tc=tc, heads=heads)
    row = lambda n: pl.BlockSpec((1, n), lambda b, i: (0, 0))
    return pl.pallas_call(
        kern,
        grid=(bsz, nt),
        in_specs=[pl.BlockSpec((tc, 3 * width), lambda b, i: (b * nt + i, 0)),
                  pl.BlockSpec((tc, width), lambda b, i: (b * nt + i, 3)),
                  pl.BlockSpec((tc, LANES), lambda b, i: (b * nt + i, 4 * width // LANES)),
                  pl.BlockSpec((CONV_WIDTH, 3 * width), lambda b, i: (0, 0)),
                  row(LANES), row(LANES), row(GDN_HEAD_DIM)],
        out_specs=pl.BlockSpec((tc, width), lambda b, i: (b * nt + i, 0)),
        out_shape=jax.ShapeDtypeStruct((bsz * seq, width), BF16),
        scratch_shapes=[pltpu.VMEM((tc + SUBLANES, 3 * width), F32),
                        pltpu.VMEM((tc, width), F32), pltpu.VMEM((tc, width), F32),
                        pltpu.VMEM((tc, width), F32),
                        pltpu.VMEM((tc, LANES), F32), pltpu.VMEM((tc, LANES), F32),
                        pltpu.VMEM((heads, GDN_HEAD_DIM, GDN_HEAD_DIM), F32)],
        compiler_params=_params(2),
    )(proj, proj, proj, cw, alog, dtb, onorm.reshape(1, -1))


def _router_kernel(h_ref, g_ref, r_ref, route_ref, cnt_ref, *, tm):
    @pl.when(pl.program_id(0) == 0)
    def _():
        cnt_ref[...] = jnp.zeros_like(cnt_ref)

    t_hi, t_lo = _split_bf16(_rms(h_ref[...], g_ref[...]))
    r_hi, r_lo = _split_bf16(r_ref[...])
    logits = _dot(t_hi, r_hi) + (_dot(t_lo, r_hi) + _dot(t_hi, r_lo))
    lane = lax.broadcasted_iota(I32, (tm, LANES), 1)
    lane_f = lane.astype(F32)
    neg = -jnp.inf
    lg = jnp.where(lane < N_EXPERTS, logits, neg)
    m1 = jnp.max(lg, axis=-1, keepdims=True)
    i1 = jnp.min(jnp.where(lg == m1, lane_f, float(LANES)), axis=-1, keepdims=True)
    oh1 = lane_f == i1
    lg2 = jnp.where(oh1, neg, lg)
    m2 = jnp.max(lg2, axis=-1, keepdims=True)
    i2 = jnp.min(jnp.where(lg2 == m2, lane_f, float(LANES)), axis=-1, keepdims=True)
    oh2 = lane_f == i2
    e2 = jnp.exp(m2 - m1)
    w1 = 1.0 / (1.0 + e2)
    w2 = e2 / (1.0 + e2)

    both = (oh1 | oh2).astype(BF16)
    ri = lax.broadcasted_iota(I32, (tm, tm), 0)
    ci = lax.broadcasted_iota(I32, (tm, tm), 1)
    pos = _dot((ri > ci).astype(BF16), both) + cnt_ref[0:1, :]
    rank1 = jnp.sum(jnp.where(oh1, pos, 0.0), axis=-1, keepdims=True)
    rank2 = jnp.sum(jnp.where(oh2, pos, 0.0), axis=-1, keepdims=True)
    cnt_ref[...] = cnt_ref[...] + jnp.sum(both.astype(F32), axis=0, keepdims=True)

    out = jnp.where(lane == 0, i1, 0.0)
    out = jnp.where(lane == 1, i2, out)
    out = jnp.where(lane == 2, rank1, out)
    out = jnp.where(lane == 3, rank2, out)
    out = jnp.where(lane == 4, w1, out)
    route_ref[...] = jnp.where(lane == 5, w2, out)


def router(h, g, r, tm):
    t, d = h.shape
    kern = functools.partial(_router_kernel, tm=tm)
    return pl.pallas_call(
        kern,
        grid=(t // tm,),
        in_specs=[pl.BlockSpec((tm, d), lambda i: (i, 0)),
                  pl.BlockSpec((1, d), lambda i: (0, 0)),
                  pl.BlockSpec((d, LANES), lambda i: (0, 0))],
        out_specs=[pl.BlockSpec((tm, LANES), lambda i: (i, 0)),
                   pl.BlockSpec((SUBLANES, LANES), lambda i: (0, 0))],
        out_shape=[jax.ShapeDtypeStruct((t, LANES), F32),
                   jax.ShapeDtypeStruct((SUBLANES, LANES), F32)],
        compiler_params=_params(1),
    )(h, g.reshape(1, d), r)


def _moe_kernel(te_ref, nu_ref, src_ref, h_hbm, g_ref, wg_ref, wu_ref, wd_ref, y_ref,
                gbuf, xn_ref, sem, *, tm):
    i = pl.program_id(0)
    j = pl.program_id(1)
    n_used = nu_ref[0]

    def row_copy(tile, slot, r):
        tok = src_ref[tile * tm + r]
        return pltpu.make_async_copy(h_hbm.at[pl.ds(tok, 1)], gbuf.at[slot, pl.ds(r, 1)], sem.at[slot])

    def issue(tile, slot):
        def body(r8, carry):
            for k in range(SUBLANES):
                row_copy(tile, slot, r8 * SUBLANES + k).start()
            return carry
        lax.fori_loop(0, tm // SUBLANES, body, 0)

    def wait_all(slot):
        pltpu.make_async_copy(h_hbm.at[pl.ds(0, tm)], gbuf.at[slot], sem.at[slot]).wait()

    @pl.when((j == 0) & (i == 0))
    def _():
        issue(0, 0)

    @pl.when((j == 0) & (i < n_used))
    def _():
        slot = i % 2
        wait_all(slot)

        @pl.when(i + 1 < n_used)
        def _():
            issue(i + 1, 1 - slot)

        xn_ref[...] = _rms(gbuf[slot], g_ref[...]).astype(BF16)

    @pl.when(i < n_used)
    def _():
        xn = xn_ref[...]
        act = (jax.nn.silu(_dot(xn, wg_ref[0])) * _dot(xn, wu_ref[0])).astype(BF16)
        part = _dot(act, wd_ref[0])

        @pl.when(j == 0)
        def _():
            y_ref[...] = part

        @pl.when(j > 0)
        def _():
            y_ref[...] += part

    @pl.when((i >= n_used) & (j == 0))
    def _():
        y_ref[...] = jnp.zeros_like(y_ref)


def moe_experts(tile_expert, n_used, src_token, h, g, wg, wu, wd, n_tiles, tm, tf):
    t, d = h.shape
    f = wg.shape[2]
    n_ff = f // tf

    def ff_idx(i, j, nu):
        return jnp.where(i < nu[0], j, n_ff - 1)

    kern = functools.partial(_moe_kernel, tm=tm)
    grid_spec = pltpu.PrefetchScalarGridSpec(
        num_scalar_prefetch=3,
        grid=(n_tiles, n_ff),
        in_specs=[pl.BlockSpec(memory_space=pl.ANY),
                  pl.BlockSpec((1, d), lambda i, j, te, nu, src: (0, 0)),
                  pl.BlockSpec((1, d, tf), lambda i, j, te, nu, src: (te[i], 0, ff_idx(i, j, nu))),
                  pl.BlockSpec((1, d, tf), lambda i, j, te, nu, src: (te[i], 0, ff_idx(i, j, nu))),
                  pl.BlockSpec((1, tf, d), lambda i, j, te, nu, src: (te[i], ff_idx(i, j, nu), 0))],
        out_specs=pl.BlockSpec((tm, d), lambda i, j, te, nu, src: (i, 0)),
        scratch_shapes=[pltpu.VMEM((2, tm, d), F32), pltpu.VMEM((tm, d), BF16),
                        pltpu.SemaphoreType.DMA((2,))],
    )
    return pl.pallas_call(
        kern,
        grid_spec=grid_spec,
        out_shape=jax.ShapeDtypeStruct((n_tiles * tm, d), F32),
        compiler_params=_params(2),
    )(tile_expert, n_used, src_token, h, g.reshape(1, d), wg, wu, wd)


def _combine_kernel(s1_ref, s2_ref, h_ref, route_ref, y_hbm, o_ref, gbuf, sem, *, tm):
    i = pl.program_id(0)
    n = pl.num_programs(0)

    def issue(tile, slot):
        def body(r8, carry):
            for k in range(SUBLANES):
                r = r8 * SUBLANES + k
                pltpu.make_async_copy(y_hbm.at[pl.ds(s1_ref[tile * tm + r], 1)],
                                      gbuf.at[slot, pl.ds(r, 1)], sem.at[slot]).start()
                pltpu.make_async_copy(y_hbm.at[pl.ds(s2_ref[tile * tm + r], 1)],
                                      gbuf.at[slot, pl.ds(tm + r, 1)], sem.at[slot]).start()
            return carry
        lax.fori_loop(0, tm // SUBLANES, body, 0)

    @pl.when(i == 0)
    def _():
        issue(0, 0)

    slot = i % 2
    pltpu.make_async_copy(y_hbm.at[pl.ds(0, 2 * tm)], gbuf.at[slot], sem.at[slot]).wait()

    @pl.when(i + 1 < n)
    def _():
        issue(i + 1, 1 - slot)

    route = route_ref[...]
    o_ref[...] = (h_ref[...] + route[:, 4:5] * gbuf[slot, 0:tm, :]
                  + route[:, 5:6] * gbuf[slot, tm:2 * tm, :])


def moe_combine(slot1, slot2, h, route, y, tm):
    t, d = h.shape
    kern = functools.partial(_combine_kernel, tm=tm)
    grid_spec = pltpu.PrefetchScalarGridSpec(
        num_scalar_prefetch=2,
        grid=(t // tm,),
        in_specs=[pl.BlockSpec((tm, d), lambda i, s1, s2: (i, 0)),
                  pl.BlockSpec((tm, LANES), lambda i, s1, s2: (i, 0)),
                  pl.BlockSpec(memory_space=pl.ANY)],
        out_specs=pl.BlockSpec((tm, d), lambda i, s1, s2: (i, 0)),
        scratch_shapes=[pltpu.VMEM((2, 2 * tm, d), F32), pltpu.SemaphoreType.DMA((2,))],
    )
    return pl.pallas_call(
        kern,
        grid_spec=grid_spec,
        out_shape=jax.ShapeDtypeStruct((t, d), F32),
        compiler_params=_params(1),
    )(slot1, slot2, h, route, y)


def _rope_tables(seq):
    half = ATT_HEAD_DIM // 2
    inv_freq = ROPE_THETA ** (-jnp.arange(half, dtype=F32) / half)
    ang = jnp.arange(seq, dtype=jnp.int32).astype(F32)[:, None] * inv_freq[None, :]
    reps = LANES // half
    cos = jnp.tile(jnp.cos(ang), (1, reps))
    sign = jnp.tile(jnp.concatenate([-jnp.ones((half,), F32), jnp.ones((half,), F32)]), LANES // ATT_HEAD_DIM)
    sin = jnp.tile(jnp.sin(ang), (1, reps)) * sign[None, :]
    return cos, sin


def _even_layer(h, bsz, seq, ln_mix, ln_ffn, w_in, conv_w, conv_b, w_a, b_a, w_i, b_i, lam,
                q_norm, k_norm, sinks, w_out, f_gate, f_up, f_down):
    lru_w = conv_w.shape[1]
    q_heads = sinks.shape[0]
    q_w = q_heads * ATT_HEAD_DIM
    kv_w = (w_in.shape[1] - 2 * lru_w - q_w) // 2
    kv_heads = kv_w // ATT_HEAD_DIM

    proj = rms_matmul(h, ln_mix, w_in.astype(BF16), tm=512, tn=w_in.shape[1])
    gates_w = jnp.concatenate([block_diag(*w_a), block_diag(*w_i)], axis=1).astype(BF16)
    y_lru = rglru(proj, conv_w, conv_b, gates_w, jnp.concatenate([b_a, b_i]), lam, bsz, seq, lru_w, tm=256)
    cos, sin = _rope_tables(seq)
    tile2 = lambda g: jnp.tile(g, LANES // ATT_HEAD_DIM).reshape(1, LANES)
    y_att = swa(proj, sinks, cos, sin, tile2(q_norm), tile2(k_norm), bsz, seq,
                q_col=2 * lru_w, k_col=2 * lru_w + q_w, v_col=2 * lru_w + q_w + kv_w,
                q_heads=q_heads, kv_heads=kv_heads)
    w_out = w_out.astype(BF16)
    h = proj_res(h, [y_lru, y_att], [w_out[:lru_w], w_out[lru_w:]], tm=512)
    return ffn(h, ln_ffn, f_gate.astype(BF16), f_up.astype(BF16), f_down.astype(BF16), tm=512, tf=1408)


def _odd_layer(h, bsz, seq, ln_mix, ln_ffn, w_in, conv_w, a_log, dt_bias, out_norm, w_out, router_w,
               m_gate, m_up, m_down):
    t, d = h.shape
    heads = a_log.shape[0]
    width = heads * GDN_HEAD_DIM
    cols = w_in.shape[1]
    pad = (-cols) % (11 * LANES)
    w_in_p = jnp.pad(w_in, ((0, 0), (0, pad))).astype(BF16)
    proj = rms_matmul(h, ln_mix, w_in_p, tm=512, tn=w_in_p.shape[1] // 3)
    lane_pad = lambda v: jnp.pad(v, (0, LANES - heads)).reshape(1, LANES)
    y = gdn(proj, conv_w, lane_pad(a_log), lane_pad(dt_bias), out_norm, bsz, seq, heads, tc=256)
    h = proj_res(h, [y], [w_out.astype(BF16)], tm=512)

    route, counts = router(h, ln_ffn, jnp.pad(router_w, ((0, 0), (0, LANES - N_EXPERTS))), tm=512)
    tm = MOE_TILE
    n_tiles = (2 * t) // tm + N_EXPERTS
    counts = counts[0, :N_EXPERTS].astype(I32)
    tiles_per = (counts + tm - 1) // tm
    tile_end = jnp.cumsum(tiles_per)
    starts = (tile_end - tiles_per) * tm
    n_used = tile_end[-1:]
    tile_ids = jnp.minimum(jnp.arange(n_tiles, dtype=I32), n_used - 1)
    tile_expert = jnp.sum((tile_ids[:, None] >= tile_end[None, :]).astype(I32), axis=1)
    e1 = route[:, 0].astype(I32)
    e2 = route[:, 1].astype(I32)
    slot1 = starts[e1] + route[:, 2].astype(I32)
    slot2 = starts[e2] + route[:, 3].astype(I32)
    tok = jnp.arange(t, dtype=I32)
    src_token = jnp.zeros((n_tiles * tm,), I32).at[slot1].set(tok).at[slot2].set(tok)

    y = moe_experts(tile_expert, n_used, src_token, h, ln_ffn, m_gate.astype(BF16), m_up.astype(BF16),
                    m_down.astype(BF16), n_tiles, tm, MOE_FF_CHUNK)
    return moe_combine(slot1, slot2, h, route, y, tm=256)


def kernel(x, ln_mix, ln_ffn, e_w_in, e_lru_conv_w, e_lru_conv_b, e_lru_w_a, e_lru_b_a, e_lru_w_i, e_lru_b_i, e_lru_lambda, e_q_norm, e_k_norm, e_sinks, e_w_out, e_ffn_w_gate, e_ffn_w_up, e_ffn_w_down, o_w_in, o_conv_w, o_a_log, o_dt_bias, o_out_norm, o_w_out, o_router, o_moe_w_gate, o_moe_w_up, o_moe_w_down):
    bsz, seq, d = x.shape
    h = x.reshape(bsz * seq, d)
    for layer in range(ln_mix.shape[0]):
        j = layer // 2
        if layer % 2 == 0:
            h = _even_layer(h, bsz, seq, ln_mix[layer], ln_ffn[layer], e_w_in[j], e_lru_conv_w[j],
                            e_lru_conv_b[j], e_lru_w_a[j], e_lru_b_a[j], e_lru_w_i[j], e_lru_b_i[j],
                            e_lru_lambda[j], e_q_norm[j], e_k_norm[j], e_sinks[j], e_w_out[j],
                            e_ffn_w_gate[j], e_ffn_w_up[j], e_ffn_w_down[j])
        else:
            h = _odd_layer(h, bsz, seq, ln_mix[layer], ln_ffn[layer], o_w_in[j], o_conv_w[j], o_a_log[j],
                           o_dt_bias[j], o_out_norm[j], o_w_out[j], o_router[j], o_moe_w_gate[j],
                           o_moe_w_up[j], o_moe_w_down[j])
    return h.reshape(bsz, seq, d)
```

```python
import functools

import jax
import jax.numpy as jnp
from jax import lax
from jax.scipy.linalg import block_diag
from jax.experimental import pallas as pl
from jax.experimental.pallas import tpu as pltpu

F32 = jnp.float32
BF16 = jnp.bfloat16
I32 = jnp.int32

EPS = 1e-6
LANES = 128
SUBLANES = 8
VMEM_LIMIT = 52 * 1024 * 1024

CONV_WIDTH = 4
LRU_C = 8.0
ATT_HEAD_DIM = 64
ATT_WINDOW = 128
ROPE_THETA = 10000.0
GDN_HEAD_DIM = 128
GDN_CHUNK = 64
N_EXPERTS = 8
MOE_TILE = 512
MOE_FF_CHUNK = 512


def _params(n_axes, vmem=VMEM_LIMIT):
    return pltpu.CompilerParams(dimension_semantics=("arbitrary",) * n_axes, vmem_limit_bytes=vmem)


def _rms(x, g):
    return x * lax.rsqrt(jnp.mean(x * x, axis=-1, keepdims=True) + EPS) * g


def _dot(a, b):
    return jnp.dot(a, b, preferred_element_type=F32)


def _dot_nt(a, b):
    return lax.dot_general(a, b, (((1,), (1,)), ((), ())), preferred_element_type=F32)


def _split_bf16(x):
    hi = x.astype(BF16)
    lo = (x - hi.astype(F32)).astype(BF16)
    return hi, lo


def _rms_matmul_kernel(x_ref, g_ref, w_ref, o_ref, xn_ref):
    @pl.when(pl.program_id(1) == 0)
    def _():
        xn_ref[...] = _rms(x_ref[...], g_ref[...]).astype(BF16)

    o_ref[...] = _dot(xn_ref[...], w_ref[...])


def rms_matmul(x, g, w, tm, tn):
    t, d = x.shape
    n = w.shape[1]
    return pl.pallas_call(
        _rms_matmul_kernel,
        grid=(t // tm, n // tn),
        in_specs=[pl.BlockSpec((tm, d), lambda i, j: (i, 0)),
                  pl.BlockSpec((1, d), lambda i, j: (0, 0)),
                  pl.BlockSpec((d, tn), lambda i, j: (0, j))],
        out_specs=pl.BlockSpec((tm, tn), lambda i, j: (i, j)),
        out_shape=jax.ShapeDtypeStruct((t, n), F32),
        scratch_shapes=[pltpu.VMEM((tm, d), BF16)],
        compiler_params=_params(2),
    )(x, g.reshape(1, d), w)


def _causal_conv(ext_ref, x, cw, tm):
    ext_ref[SUBLANES:SUBLANES + tm, :] = x
    base = SUBLANES - (CONV_WIDTH - 1)
    acc = cw[0:1, :] * ext_ref[base:base + tm, :]
    for j in range(1, CONV_WIDTH):
        acc = acc + cw[j:j + 1, :] * ext_ref[base + j:base + j + tm, :]
    ext_ref[0:SUBLANES, :] = x[tm - SUBLANES:tm, :]
    return acc


def _rglru_kernel(xb_ref, gate_ref, cw_ref, cb_ref, wg_ref, bg_ref, lam_ref, o_ref, ext_ref, h_ref,
                  *, tm, width):
    @pl.when(pl.program_id(1) == 0)
    def _():
        ext_ref[0:SUBLANES, :] = jnp.zeros((SUBLANES, width), F32)
        h_ref[...] = jnp.zeros_like(h_ref)

    xc = _causal_conv(ext_ref, xb_ref[...], cw_ref[...], tm) + cb_ref[...]
    z = _dot(xc.astype(BF16), wg_ref[...]) + bg_ref[...]
    r = jax.nn.sigmoid(z[:, :width])
    gi = jax.nn.sigmoid(z[:, width:])
    log_a = -LRU_C * r * jax.nn.softplus(-lam_ref[...])
    a = jnp.exp(log_a)
    u = jnp.sqrt(1.0 - a * a) * (gi * xc)

    row = lax.broadcasted_iota(I32, (tm, width), 0)
    d = 1
    while d < tm:
        keep = row >= d
        u = u + jnp.where(keep, a * pltpu.roll(u, d, 0), 0.0)
        a = jnp.where(keep, a * pltpu.roll(a, d, 0), a)
        d *= 2
    h = u + a * h_ref[0:1, :]
    h_ref[...] = jnp.broadcast_to(h[tm - 1:tm, :], h_ref.shape)
    o_ref[...] = (h * jax.nn.gelu(gate_ref[...])).astype(o_ref.dtype)


def rglru(proj, cw, cb, wg, bg, lam, bsz, seq, width, tm):
    nt = seq // tm
    kern = functools.partial(_rglru_kernel, tm=tm, width=width)
    row = lambda c: pl.BlockSpec((1, c), lambda b, i: (0, 0))
    return pl.pallas_call(
        kern,
        grid=(bsz, nt),
        in_specs=[pl.BlockSpec((tm, width), lambda b, i: (b * nt + i, 0)),
                  pl.BlockSpec((tm, width), lambda b, i: (b * nt + i, 1)),
                  pl.BlockSpec((CONV_WIDTH, width), lambda b, i: (0, 0)),
                  row(width),
                  pl.BlockSpec((width, 2 * width), lambda b, i: (0, 0)),
                  row(2 * width),
                  row(width)],
        out_specs=pl.BlockSpec((tm, width), lambda b, i: (b * nt + i, 0)),
        out_shape=jax.ShapeDtypeStruct((bsz * seq, width), BF16),
        scratch_shapes=[pltpu.VMEM((tm + SUBLANES, width), F32), pltpu.VMEM((SUBLANES, width), F32)],
        compiler_params=_params(2),
    )(proj, proj, cw, cb.reshape(1, -1), wg, bg.reshape(1, -1), lam.reshape(1, -1))


def _swa_kernel(sink_ref, q_ref, k_ref, v_ref, cos_ref, sin_ref, qg_ref, kg_ref, o_ref, kall_ref, vall_ref,
                *, q_heads, kv_heads):
    w = ATT_WINDOW
    n = pl.program_id(1)
    group = q_heads // kv_heads
    assert kv_heads * ATT_HEAD_DIM == LANES and group % 2 == 0

    @pl.when(n == 0)
    def _():
        kall_ref[0:w, :] = jnp.zeros((w, LANES), BF16)
        vall_ref[0:w, :] = jnp.zeros((w, LANES), BF16)

    @pl.when(n > 0)
    def _():
        kall_ref[0:w, :] = kall_ref[w:2 * w, :]
        vall_ref[0:w, :] = vall_ref[w:2 * w, :]

    lane = lax.broadcasted_iota(I32, (w, LANES), 1)
    first_half = (lane % ATT_HEAD_DIM) < (ATT_HEAD_DIM // 2)
    lane_head = lane // ATT_HEAD_DIM
    bd_r = lax.broadcasted_iota(I32, (LANES, LANES), 0) // ATT_HEAD_DIM
    bd_c = lax.broadcasted_iota(I32, (LANES, LANES), 1) // ATT_HEAD_DIM
    head_ones = (bd_r == bd_c).astype(BF16)
    cos = cos_ref[...]
    sin = sin_ref[...]

    def norm_rope(x, g):
        hi, lo = _split_bf16(x * x)
        ss = _dot(hi, head_ones) + _dot(lo, head_ones)
        xn = x * lax.rsqrt(ss * (1.0 / ATT_HEAD_DIM) + EPS) * g
        rot = jnp.where(first_half, pltpu.roll(xn, LANES - ATT_HEAD_DIM // 2, 1),
                        pltpu.roll(xn, ATT_HEAD_DIM // 2, 1))
        return xn * cos + rot * sin

    kall_ref[w:2 * w, :] = norm_rope(k_ref[...], kg_ref[...]).astype(BF16)
    vall_ref[w:2 * w, :] = v_ref[...].astype(BF16)
    qc = [norm_rope(q_ref[:, LANES * c:LANES * (c + 1)], qg_ref[...]) for c in range(q_heads // 2)]

    rows = group * w
    qi = lax.broadcasted_iota(I32, (rows, 2 * w), 0) % w
    kj = lax.broadcasted_iota(I32, (rows, 2 * w), 1)
    valid = (kj > qi) & (kj <= qi + w) & ((n > 0) | (kj >= w))
    row_head = lax.broadcasted_iota(I32, (rows, 1), 0) // w

    outs = []
    for g in range(kv_heads):
        parts = []
        for hh in range(group):
            h = g * group + hh
            x = qc[h // 2]
            if h % 2 != g:
                x = pltpu.roll(x, ATT_HEAD_DIM, 1)
            parts.append(jnp.where(lane_head == g, x, 0.0))
        qs = jnp.concatenate(parts, axis=0).astype(BF16)
        s = _dot_nt(qs, kall_ref[...]) * (ATT_HEAD_DIM ** -0.5)
        s = jnp.where(valid, s, -1e30)
        sink = jnp.zeros((rows, 1), F32)
        for hh in range(group):
            sink = jnp.where(row_head == hh, sink_ref[g * group + hh], sink)
        m = jnp.maximum(jnp.max(s, axis=-1, keepdims=True), sink)
        p = jnp.exp(s - m)
        denom = jnp.sum(p, axis=-1, keepdims=True) + jnp.exp(sink - m)
        pv = _dot(p.astype(BF16), vall_ref[...]) / denom
        outs.extend(pv[w * hh:w * (hh + 1), :] for hh in range(group))

    for c in range(q_heads // 2):
        g = (2 * c) // group
        a, b = outs[2 * c], outs[2 * c + 1]
        if g == 1:
            a = pltpu.roll(a, ATT_HEAD_DIM, 1)
        else:
            b = pltpu.roll(b, ATT_HEAD_DIM, 1)
        o_ref[:, LANES * c:LANES * (c + 1)] = jnp.where(lane_head == 0, a, b).astype(o_ref.dtype)


def swa(proj, sinks, cos, sin, qg, kg, bsz, seq, q_col, k_col, v_col, q_heads, kv_heads):
    w = ATT_WINDOW
    nb = seq // w
    qw = q_heads * ATT_HEAD_DIM
    kern = functools.partial(_swa_kernel, q_heads=q_heads, kv_heads=kv_heads)
    return pl.pallas_call(
        kern,
        grid=(bsz, nb),
        in_specs=[pl.BlockSpec(memory_space=pltpu.SMEM),
                  pl.BlockSpec((w, qw), lambda b, i: (b * nb + i, q_col // qw)),
                  pl.BlockSpec((w, LANES), lambda b, i: (b * nb + i, k_col // LANES)),
                  pl.BlockSpec((w, LANES), lambda b, i: (b * nb + i, v_col // LANES)),
                  pl.BlockSpec((w, LANES), lambda b, i: (i, 0)),
                  pl.BlockSpec((w, LANES), lambda b, i: (i, 0)),
                  pl.BlockSpec((1, LANES), lambda b, i: (0, 0)),
                  pl.BlockSpec((1, LANES), lambda b, i: (0, 0))],
        out_specs=pl.BlockSpec((w, qw), lambda b, i: (b * nb + i, 0)),
        out_shape=jax.ShapeDtypeStruct((bsz * seq, qw), BF16),
        scratch_shapes=[pltpu.VMEM((2 * w, LANES), BF16), pltpu.VMEM((2 * w, LANES), BF16)],
        compiler_params=_params(2),
    )(sinks, proj, proj, proj, cos, sin, qg, kg)


def _proj_res_kernel(*refs, n_in):
    h_ref, o_ref = refs[0], refs[-1]
    acc = h_ref[...]
    for y_ref, w_ref in zip(refs[1:1 + n_in], refs[1 + n_in:1 + 2 * n_in]):
        acc = acc + _dot(y_ref[...], w_ref[...])
    o_ref[...] = acc


def proj_res(h, ys, ws, tm):
    t, d = h.shape
    kern = functools.partial(_proj_res_kernel, n_in=len(ys))
    return pl.pallas_call(
        kern,
        grid=(t // tm,),
        in_specs=([pl.BlockSpec((tm, d), lambda i: (i, 0))]
                  + [pl.BlockSpec((tm, y.shape[1]), lambda i: (i, 0)) for y in ys]
                  + [pl.BlockSpec(w.shape, lambda i: (0, 0)) for w in ws]),
        out_specs=pl.BlockSpec((tm, d), lambda i: (i, 0)),
        out_shape=jax.ShapeDtypeStruct((t, d), F32),
        compiler_params=_params(1),
    )(h, *ys, *ws)


def _ffn_kernel(h_ref, g_ref, wg_ref, wu_ref, wd_ref, o_ref, xn_ref):
    j = pl.program_id(1)

    @pl.when(j == 0)
    def _():
        x = h_ref[...]
        xn_ref[...] = _rms(x, g_ref[...]).astype(BF16)
        o_ref[...] = x

    xn = xn_ref[...]
    act = (jax.nn.silu(_dot(xn, wg_ref[...])) * _dot(xn, wu_ref[...])).astype(BF16)
    o_ref[...] += _dot(act, wd_ref[...])


def ffn(h, g, wg, wu, wd, tm, tf):
    t, d = h.shape
    f = wg.shape[1]
    return pl.pallas_call(
        _ffn_kernel,
        grid=(t // tm, f // tf),
        in_specs=[pl.BlockSpec((tm, d), lambda i, j: (i, 0)),
                  pl.BlockSpec((1, d), lambda i, j: (0, 0)),
                  pl.BlockSpec((d, tf), lambda i, j: (0, j)),
                  pl.BlockSpec((d, tf), lambda i, j: (0, j)),
                  pl.BlockSpec((tf, d), lambda i, j: (j, 0))],
        out_specs=pl.BlockSpec((tm, d), lambda i, j: (i, 0)),
        out_shape=jax.ShapeDtypeStruct((t, d), F32),
        scratch_shapes=[pltpu.VMEM((tm, d), BF16)],
        compiler_params=_params(2),
    )(h, g.reshape(1, d), wg, wu, wd)


def _gdn_kernel(qkv_ref, gate_ref, ab_ref, cw_ref, alog_ref, dtb_ref, onorm_ref, o_ref,
                ext_ref, q_ref, k_ref, v_ref, gc_ref, beta_ref, *state_refs, tc, heads):
    hd = GDN_HEAD_DIM
    c = GDN_CHUNK
    width = heads * hd

    @pl.when(pl.program_id(1) == 0)
    def _():
        ext_ref[0:SUBLANES, :] = jnp.zeros((SUBLANES, 3 * width), F32)
        for s_ref in state_refs:
            s_ref[...] = jnp.zeros_like(s_ref)

    qkv = jax.nn.silu(_causal_conv(ext_ref, qkv_ref[...], cw_ref[...], tc))
    for h in range(heads):
        q = qkv[:, hd * h:hd * (h + 1)]
        k = qkv[:, width + hd * h:width + hd * (h + 1)]
        q_ref[:, hd * h:hd * (h + 1)] = (q * lax.rsqrt(jnp.sum(q * q, axis=-1, keepdims=True) + EPS)
                                          * (hd ** -0.5))
        k_ref[:, hd * h:hd * (h + 1)] = k * lax.rsqrt(jnp.sum(k * k, axis=-1, keepdims=True) + EPS)
    v_ref[...] = qkv[:, 2 * width:]

    ab = ab_ref[...]
    g = -jnp.exp(alog_ref[...]) * jax.nn.softplus(ab + dtb_ref[...])
    beta_ref[...] = pltpu.roll(jax.nn.sigmoid(ab), LANES - heads, 1)
    row = lax.broadcasted_iota(I32, (tc, LANES), 0) % c
    d = 1
    while d < c:
        g = g + jnp.where(row >= d, pltpu.roll(g, d, 0), 0.0)
        d *= 2
    gc_ref[...] = g

    ri = lax.broadcasted_iota(I32, (c, c), 0)
    ci = lax.broadcasted_iota(I32, (c, c), 1)
    tri = ri >= ci
    strict = ri > ci
    eye = (ri == ci).astype(F32)

    def chunk_body(n, carry):
        r0 = pl.multiple_of(n * c, c)
        rows = pl.ds(r0, c)
        gc_all = gc_ref[rows, :]
        gr_all = gc_all.T
        beta_all = beta_ref[rows, :]
        hs = range(heads)
        cols = [slice(hd * h, hd * (h + 1)) for h in hs]
        q = [q_ref[rows, cols[h]] for h in hs]
        k = [k_ref[rows, cols[h]] for h in hs]
        v = [v_ref[rows, cols[h]] for h in hs]
        gcol = [gc_all[:, h:h + 1] for h in hs]
        grow = [gr_all[h:h + 1, :] for h in hs]
        beta = [beta_all[:, h:h + 1] for h in hs]
        qk_kk = [_dot_nt(jnp.concatenate([q[h], k[h]], axis=0).astype(BF16), k[h].astype(BF16)) for h in hs]
        decay = [jnp.where(tri, jnp.exp(jnp.where(tri, gcol[h] - grow[h], 0.0)), 0.0) for h in hs]
        m = [-jnp.where(strict, beta[h] * qk_kk[h][c:, :] * decay[h], 0.0) for h in hs]
        p = [eye + m[h] for h in hs]
        e = [_dot(m[h].astype(BF16), m[h].astype(BF16)) for h in hs]
        lvl = 2
        while 2 * lvl < c:
            r = [_dot(jnp.concatenate([p[h], e[h]], axis=0).astype(BF16), e[h].astype(BF16)) for h in hs]
            p = [p[h] + r[h][:c, :] for h in hs]
            e = [r[h][c:, :] for h in hs]
            lvl *= 2
        p = [p[h] + _dot(p[h].astype(BF16), e[h].astype(BF16)) for h in hs]
        eg = [jnp.exp(gcol[h]) for h in hs]
        uw = [_dot(p[h].astype(BF16),
                   jnp.concatenate([beta[h] * v[h], (beta[h] * eg[h]) * k[h]], axis=1).astype(BF16)) for h in hs]
        a_qk = [jnp.where(tri, qk_kk[h][:c, :] * decay[h], 0.0) for h in hs]
        g_last = [gcol[h][c - 1:c, :] for h in hs]
        s = [state_refs[h][...] for h in hs]
        ws_qs = [_dot(jnp.concatenate([uw[h][:, hd:], q[h] * eg[h]], axis=0).astype(BF16), s[h].astype(BF16))
                 for h in hs]
        vb = [(uw[h][:, :hd] - ws_qs[h][:c, :]).astype(BF16) for h in hs]
        o = [ws_qs[h][c:, :] + _dot(a_qk[h].astype(BF16), vb[h]) for h in hs]
        s_new = [s[h] * jnp.exp(g_last[h]) + _dot((k[h] * jnp.exp(g_last[h] - gcol[h])).T.astype(BF16), vb[h])
                 for h in hs]
        for h in hs:
            state_refs[h][...] = s_new[h]
            on = o[h] * lax.rsqrt(jnp.mean(o[h] * o[h], axis=-1, keepdims=True) + EPS) * onorm_ref[...]
            o_ref[rows, cols[h]] = (on * jax.nn.silu(gate_ref[rows, cols[h]])).astype(o_ref.dtype)
        return carry

    lax.fori_loop(0, tc // c, chunk_body, 0)


def gdn(proj, cw, alog, dtb, onorm, bsz, seq, heads, tc):
    nt = seq // tc
    width = heads * GDN_HEAD_DIM
    kern = functools.partial(_gdn_kernel, tc=tc, heads=heads)
    row = lambda n: pl.BlockSpec((1, n), lambda b, i: (0, 0))
    return pl.pallas_call(
        kern,
        grid=(bsz, nt),
        in_specs=[pl.BlockSpec((tc, 3 * width), lambda b, i: (b * nt + i, 0)),
                  pl.BlockSpec((tc, width), lambda b, i: (b * nt + i, 3)),
                  pl.BlockSpec((tc, LANES), lambda b, i: (b * nt + i, 4 * width // LANES)),
                  pl.BlockSpec((CONV_WIDTH, 3 * width), lambda b, i: (0, 0)),
                  row(LANES), row(LANES), row(GDN_HEAD_DIM)],
        out_specs=pl.BlockSpec((tc, width), lambda b, i: (b * nt + i, 0)),
        out_shape=jax.ShapeDtypeStruct((bsz * seq, width), BF16),
        scratch_shapes=[pltpu.VMEM((tc + SUBLANES, 3 * width), F32),
                        pltpu.VMEM((tc, width), F32), pltpu.VMEM((tc, width), F32),
                        pltpu.VMEM((tc, width), F32),
                        pltpu.VMEM((tc, LANES), F32), pltpu.VMEM((tc, LANES), F32),
                        *[pltpu.VMEM((GDN_HEAD_DIM, GDN_HEAD_DIM), F32) for _ in range(heads)]],
        compiler_params=_params(2),
    )(proj, proj, proj, cw, alog, dtb, onorm.reshape(1, -1))


def _router_kernel(h_ref, g_ref, r_ref, route_ref, cnt_ref, *, tm):
    @pl.when(pl.program_id(0) == 0)
    def _():
        cnt_ref[...] = jnp.zeros_like(cnt_ref)

    t_hi, t_lo = _split_bf16(_rms(h_ref[...], g_ref[...]))
    r_hi, r_lo = _split_bf16(r_ref[...])
    logits = _dot(t_hi, r_hi) + (_dot(t_lo, r_hi) + _dot(t_hi, r_lo))
    lane = lax.broadcasted_iota(I32, (tm, LANES), 1)
    lane_f = lane.astype(F32)
    neg = -jnp.inf
    lg = jnp.where(lane < N_EXPERTS, logits, neg)
    m1 = jnp.max(lg, axis=-1, keepdims=True)
    i1 = jnp.min(jnp.where(lg == m1, lane_f, float(LANES)), axis=-1, keepdims=True)
    oh1 = lane_f == i1
    lg2 = jnp.where(oh1, neg, lg)
    m2 = jnp.max(lg2, axis=-1, keepdims=True)
    i2 = jnp.min(jnp.where(lg2 == m2, lane_f, float(LANES)), axis=-1, keepdims=True)
    oh2 = lane_f == i2
    e2 = jnp.exp(m2 - m1)
    w1 = 1.0 / (1.0 + e2)
    w2 = e2 / (1.0 + e2)

    both = (oh1 | oh2).astype(BF16)
    ri = lax.broadcasted_iota(I32, (tm, tm), 0)
    ci = lax.broadcasted_iota(I32, (tm, tm), 1)
    pos = _dot((ri > ci).astype(BF16), both) + cnt_ref[0:1, :]
    rank1 = jnp.sum(jnp.where(oh1, pos, 0.0), axis=-1, keepdims=True)
    rank2 = jnp.sum(jnp.where(oh2, pos, 0.0), axis=-1, keepdims=True)
    cnt_ref[...] = cnt_ref[...] + jnp.sum(both.astype(F32), axis=0, keepdims=True)

    out = jnp.where(lane == 0, i1, 0.0)
    out = jnp.where(lane == 1, i2, out)
    out = jnp.where(lane == 2, rank1, out)
    out = jnp.where(lane == 3, rank2, out)
    out = jnp.where(lane == 4, w1, out)
    route_ref[...] = jnp.where(lane == 5, w2, out)


def router(h, g, r, tm):
    t, d = h.shape
    kern = functools.partial(_router_kernel, tm=tm)
    return pl.pallas_call(
        kern,
        grid=(t // tm,),
        in_specs=[pl.BlockSpec((tm, d), lambda i: (i, 0)),
                  pl.BlockSpec((1, d), lambda i: (0, 0)),
                  pl.BlockSpec((d, LANES), lambda i: (0, 0))],
        out_specs=[pl.BlockSpec((tm, LANES), lambda i: (i, 0)),
                   pl.BlockSpec((SUBLANES, LANES), lambda i: (0, 0))],
        out_shape=[jax.ShapeDtypeStruct((t, LANES), F32),
                   jax.ShapeDtypeStruct((SUBLANES, LANES), F32)],
        compiler_params=_params(1),
    )(h, g.reshape(1, d), r)


def _moe_kernel(te_ref, nu_ref, src_ref, h_hbm, g_ref, wg_ref, wu_ref, wd_ref, y_ref,
                gbuf, xn_ref, sem, *, tm):
    i = pl.program_id(0)
    j = pl.program_id(1)
    n_used = nu_ref[0]

    def row_copy(tile, slot, r):
        tok = src_ref[tile * tm + r]
        return pltpu.make_async_copy(h_hbm.at[pl.ds(tok, 1)], gbuf.at[slot, pl.ds(r, 1)], sem.at[slot])

    def issue(tile, slot):
        def body(r8, carry):
            for k in range(SUBLANES):
                row_copy(tile, slot, r8 * SUBLANES + k).start()
            return carry
        lax.fori_loop(0, tm // SUBLANES, body, 0)

    def wait_all(slot):
        pltpu.make_async_copy(h_hbm.at[pl.ds(0, tm)], gbuf.at[slot], sem.at[slot]).wait()

    @pl.when((j == 0) & (i == 0))
    def _():
        issue(0, 0)

    @pl.when((j == 0) & (i < n_used))
    def _():
        slot = i % 2
        wait_all(slot)

        @pl.when(i + 1 < n_used)
        def _():
            issue(i + 1, 1 - slot)

        xn_ref[...] = _rms(gbuf[slot], g_ref[...]).astype(BF16)

    @pl.when(i < n_used)
    def _():
        xn = xn_ref[...]
        act = (jax.nn.silu(_dot(xn, wg_ref[0])) * _dot(xn, wu_ref[0])).astype(BF16)
        part = _dot(act, wd_ref[0])

        @pl.when(j == 0)
        def _():
            y_ref[...] = part

        @pl.when(j > 0)
        def _():
            y_ref[...] += part

    @pl.when((i >= n_used) & (j == 0))
    def _():
        y_ref[...] = jnp.zeros_like(y_ref)


def moe_experts(tile_expert, n_used, src_token, h, g, wg, wu, wd, n_tiles, tm, tf):
    t, d = h.shape
    f = wg.shape[2]
    n_ff = f // tf

    def ff_idx(i, j, nu):
        return jnp.where(i < nu[0], j, n_ff - 1)

    kern = functools.partial(_moe_kernel, tm=tm)
    grid_spec = pltpu.PrefetchScalarGridSpec(
        num_scalar_prefetch=3,
        grid=(n_tiles, n_ff),
        in_specs=[pl.BlockSpec(memory_space=pl.ANY),
                  pl.BlockSpec((1, d), lambda i, j, te, nu, src: (0, 0)),
                  pl.BlockSpec((1, d, tf), lambda i, j, te, nu, src: (te[i], 0, ff_idx(i, j, nu))),
                  pl.BlockSpec((1, d, tf), lambda i, j, te, nu, src: (te[i], 0, ff_idx(i, j, nu))),
                  pl.BlockSpec((1, tf, d), lambda i, j, te, nu, src: (te[i], ff_idx(i, j, nu), 0))],
        out_specs=pl.BlockSpec((tm, d), lambda i, j, te, nu, src: (i, 0)),
        scratch_shapes=[pltpu.VMEM((2, tm, d), F32), pltpu.VMEM((tm, d), BF16),
                        pltpu.SemaphoreType.DMA((2,))],
    )
    return pl.pallas_call(
        kern,
        grid_spec=grid_spec,
        out_shape=jax.ShapeDtypeStruct((n_tiles * tm, d), F32),
        compiler_params=_params(2),
    )(tile_expert, n_used, src_token, h, g.reshape(1, d), wg, wu, wd)


def _combine_kernel(s1_ref, s2_ref, h_ref, route_ref, y_hbm, o_ref, gbuf, sem, *, tm):
    i = pl.program_id(0)
    n = pl.num_programs(0)

    def issue(tile, slot):
        def body(r8, carry):
            for k in range(SUBLANES):
                r = r8 * SUBLANES + k
                pltpu.make_async_copy(y_hbm.at[pl.ds(s1_ref[tile * tm + r], 1)],
                                      gbuf.at[slot, pl.ds(r, 1)], sem.at[slot]).start()
                pltpu.make_async_copy(y_hbm.at[pl.ds(s2_ref[tile * tm + r], 1)],
                                      gbuf.at[slot, pl.ds(tm + r, 1)], sem.at[slot]).start()
            return carry
        lax.fori_loop(0, tm // SUBLANES, body, 0)

    @pl.when(i == 0)
    def _():
        issue(0, 0)

    slot = i % 2
    pltpu.make_async_copy(y_hbm.at[pl.ds(0, 2 * tm)], gbuf.at[slot], sem.at[slot]).wait()

    @pl.when(i + 1 < n)
    def _():
        issue(i + 1, 1 - slot)

    route = route_ref[...]
    o_ref[...] = (h_ref[...] + route[:, 4:5] * gbuf[slot, 0:tm, :]
                  + route[:, 5:6] * gbuf[slot, tm:2 * tm, :])


def moe_combine(slot1, slot2, h, route, y, tm):
    t, d = h.shape
    kern = functools.partial(_combine_kernel, tm=tm)
    grid_spec = pltpu.PrefetchScalarGridSpec(
        num_scalar_prefetch=2,
        grid=(t // tm,),
        in_specs=[pl.BlockSpec((tm, d), lambda i, s1, s2: (i, 0)),
                  pl.BlockSpec((tm, LANES), lambda i, s1, s2: (i, 0)),
                  pl.BlockSpec(memory_space=pl.ANY)],
        out_specs=pl.BlockSpec((tm, d), lambda i, s1, s2: (i, 0)),
        scratch_shapes=[pltpu.VMEM((2, 2 * tm, d), F32), pltpu.SemaphoreType.DMA((2,))],
    )
    return pl.pallas_call(
        kern,
        grid_spec=grid_spec,
        out_shape=jax.ShapeDtypeStruct((t, d), F32),
        compiler_params=_params(1),
    )(slot1, slot2, h, route, y)


def _rope_tables(seq):
    half = ATT_HEAD_DIM // 2
    inv_freq = ROPE_THETA ** (-jnp.arange(half, dtype=F32) / half)
    ang = jnp.arange(seq, dtype=jnp.int32).astype(F32)[:, None] * inv_freq[None, :]
    reps = LANES // half
    cos = jnp.tile(jnp.cos(ang), (1, reps))
    sign = jnp.tile(jnp.concatenate([-jnp.ones((half,), F32), jnp.ones((half,), F32)]), LANES // ATT_HEAD_DIM)
    sin = jnp.tile(jnp.sin(ang), (1, reps)) * sign[None, :]
    return cos, sin


def _even_layer(h, bsz, seq, ln_mix, ln_ffn, w_in, conv_w, conv_b, w_a, b_a, w_i, b_i, lam,
                q_norm, k_norm, sinks, w_out, f_gate, f_up, f_down):
    lru_w = conv_w.shape[1]
    q_heads = sinks.shape[0]
    q_w = q_heads * ATT_HEAD_DIM
    kv_w = (w_in.shape[1] - 2 * lru_w - q_w) // 2
    kv_heads = kv_w // ATT_HEAD_DIM

    proj = rms_matmul(h, ln_mix, w_in.astype(BF16), tm=512, tn=w_in.shape[1])
    gates_w = jnp.concatenate([block_diag(*w_a), block_diag(*w_i)], axis=1).astype(BF16)
    y_lru = rglru(proj, conv_w, conv_b, gates_w, jnp.concatenate([b_a, b_i]), lam, bsz, seq, lru_w, tm=256)
    cos, sin = _rope_tables(seq)
    tile2 = lambda g: jnp.tile(g, LANES // ATT_HEAD_DIM).reshape(1, LANES)
    y_att = swa(proj, sinks, cos, sin, tile2(q_norm), tile2(k_norm), bsz, seq,
                q_col=2 * lru_w, k_col=2 * lru_w + q_w, v_col=2 * lru_w + q_w + kv_w,
                q_heads=q_heads, kv_heads=kv_heads)
    w_out = w_out.astype(BF16)
    h = proj_res(h, [y_lru, y_att], [w_out[:lru_w], w_out[lru_w:]], tm=512)
    return ffn(h, ln_ffn, f_gate.astype(BF16), f_up.astype(BF16), f_down.astype(BF16), tm=512, tf=1408)


def _odd_layer(h, bsz, seq, ln_mix, ln_ffn, w_in, conv_w, a_log, dt_bias, out_norm, w_out, router_w,
               m_gate, m_up, m_down):
    t, d = h.shape
    heads = a_log.shape[0]
    width = heads * GDN_HEAD_DIM
    cols = w_in.shape[1]
    pad = (-cols) % (11 * LANES)
    w_in_p = jnp.pad(w_in, ((0, 0), (0, pad))).astype(BF16)
    proj = rms_matmul(h, ln_mix, w_in_p, tm=512, tn=w_in_p.shape[1] // 3)
    lane_pad = lambda v: jnp.pad(v, (0, LANES - heads)).reshape(1, LANES)
    y = gdn(proj, conv_w, lane_pad(a_log), lane_pad(dt_bias), out_norm, bsz, seq, heads, tc=256)
    h = proj_res(h, [y], [w_out.astype(BF16)], tm=512)

    route, counts = router(h, ln_ffn, jnp.pad(router_w, ((0, 0), (0, LANES - N_EXPERTS))), tm=512)
    tm = MOE_TILE
    n_tiles = (2 * t) // tm + N_EXPERTS
    counts = counts[0, :N_EXPERTS].astype(I32)
    tiles_per = (counts + tm - 1) // tm
    tile_end = jnp.cumsum(tiles_per)
    starts = (tile_end - tiles_per) * tm
    n_used = tile_end[-1:]
    tile_ids = jnp.minimum(jnp.arange(n_tiles, dtype=I32), n_used - 1)
    tile_expert = jnp.sum((tile_ids[:, None] >= tile_end[None, :]).astype(I32), axis=1)
    e1 = route[:, 0].astype(I32)
    e2 = route[:, 1].astype(I32)
    slot1 = starts[e1] + route[:, 2].astype(I32)
    slot2 = starts[e2] + route[:, 3].astype(I32)
    tok = jnp.arange(t, dtype=I32)
    src_token = jnp.zeros((n_tiles * tm,), I32).at[slot1].set(tok).at[slot2].set(tok)

    y = moe_experts(tile_expert, n_used, src_token, h, ln_ffn, m_gate.astype(BF16), m_up.astype(BF16),
                    m_down.astype(BF16), n_tiles, tm, MOE_FF_CHUNK)
    return moe_combine(slot1, slot2, h, route, y, tm=256)


def kernel(x, ln_mix, ln_ffn, e_w_in, e_lru_conv_w, e_lru_conv_b, e_lru_w_a, e_lru_b_a, e_lru_w_i, e_lru_b_i, e_lru_lambda, e_q_norm, e_k_norm, e_sinks, e_w_out, e_ffn_w_gate, e_ffn_w_up, e_ffn_w_down, o_w_in, o_conv_w, o_a_log, o_dt_bias, o_out_norm, o_w_out, o_router, o_moe_w_gate, o_moe_w_up, o_moe_w_down):
    bsz, seq, d = x.shape
    h = x.reshape(bsz * seq, d)
    for layer in range(ln_mix.shape[0]):
        j = layer // 2
        if layer % 2 == 0:
            h = _even_layer(h, bsz, seq, ln_mix[layer], ln_ffn[layer], e_w_in[j], e_lru_conv_w[j],
                            e_lru_conv_b[j], e_lru_w_a[j], e_lru_b_a[j], e_lru_w_i[j], e_lru_b_i[j],
                            e_lru_lambda[j], e_q_norm[j], e_k_norm[j], e_sinks[j], e_w_out[j],
                            e_ffn_w_gate[j], e_ffn_w_up[j], e_ffn_w_down[j])
        else:
            h = _odd_layer(h, bsz, seq, ln_mix[layer], ln_ffn[layer], o_w_in[j], o_conv_w[j], o_a_log[j],
                           o_dt_bias[j], o_out_norm[j], o_w_out[j], o_router[j], o_moe_w_gate[j],
                           o_moe_w_up[j], o_moe_w_down[j])
    return h.reshape(bsz, seq, d)
```

```python
import functools

import jax
import jax.numpy as jnp
from jax import lax
from jax.scipy.linalg import block_diag
from jax.experimental import pallas as pl
from jax.experimental.pallas import tpu as pltpu

F32 = jnp.float32
BF16 = jnp.bfloat16
I32 = jnp.int32

EPS = 1e-6
LANES = 128
SUBLANES = 8
MXU_WIDTH = 256
VMEM_LIMIT = 52 * 1024 * 1024

CONV_WIDTH = 4
LRU_C = 8.0
ATT_HEAD_DIM = 64
ATT_WINDOW = 128
ROPE_THETA = 10000.0
GDN_HEAD_DIM = 128
GDN_CHUNK = 64
N_EXPERTS = 8
MOE_TILE = 512
MOE_FF_CHUNK = 1792


def _params(n_axes, vmem=VMEM_LIMIT):
    return pltpu.CompilerParams(dimension_semantics=("arbitrary",) * n_axes, vmem_limit_bytes=vmem)


def _rms(x, g):
    return x * lax.rsqrt(jnp.mean(x * x, axis=-1, keepdims=True) + EPS) * g


def _dot(a, b):
    return jnp.dot(a, b, preferred_element_type=F32)


def _dot_nt(a, b):
    return lax.dot_general(a, b, (((1,), (1,)), ((), ())), preferred_element_type=F32)


def _split_bf16(x):
    hi = x.astype(BF16)
    lo = (x - hi.astype(F32)).astype(BF16)
    return hi, lo


def _rms_matmul_kernel(x_ref, g_ref, w_ref, o_ref, xn_ref):
    @pl.when(pl.program_id(1) == 0)
    def _():
        xn_ref[...] = _rms(x_ref[...], g_ref[...]).astype(BF16)

    o_ref[...] = _dot(xn_ref[...], w_ref[...])


def rms_matmul(x, g, w, tm, tn):
    t, d = x.shape
    n = w.shape[1]
    return pl.pallas_call(
        _rms_matmul_kernel,
        grid=(t // tm, n // tn),
        in_specs=[pl.BlockSpec((tm, d), lambda i, j: (i, 0)),
                  pl.BlockSpec((1, d), lambda i, j: (0, 0)),
                  pl.BlockSpec((d, tn), lambda i, j: (0, j))],
        out_specs=pl.BlockSpec((tm, tn), lambda i, j: (i, j)),
        out_shape=jax.ShapeDtypeStruct((t, n), F32),
        scratch_shapes=[pltpu.VMEM((tm, d), BF16)],
        compiler_params=_params(2),
    )(x, g.reshape(1, d), w)


def _causal_conv(ext_ref, x, cw, tm):
    ext_ref[SUBLANES:SUBLANES + tm, :] = x
    base = SUBLANES - (CONV_WIDTH - 1)
    acc = cw[0:1, :] * ext_ref[base:base + tm, :]
    for j in range(1, CONV_WIDTH):
        acc = acc + cw[j:j + 1, :] * ext_ref[base + j:base + j + tm, :]
    ext_ref[0:SUBLANES, :] = x[tm - SUBLANES:tm, :]
    return acc


def _rglru_kernel(xb_ref, gate_ref, cw_ref, cb_ref, wg_ref, bg_ref, lam_ref, o_ref, ext_ref, h_ref,
                  *, tm, width):
    @pl.when(pl.program_id(1) == 0)
    def _():
        ext_ref[0:SUBLANES, :] = jnp.zeros((SUBLANES, width), F32)
        h_ref[...] = jnp.zeros_like(h_ref)

    xc = _causal_conv(ext_ref, xb_ref[...], cw_ref[...], tm) + cb_ref[...]
    z = _dot(xc.astype(BF16), wg_ref[...]) + bg_ref[...]
    r = jax.nn.sigmoid(z[:, :width])
    gi = jax.nn.sigmoid(z[:, width:])
    log_a = -LRU_C * r * jax.nn.softplus(-lam_ref[...])
    a = jnp.exp(log_a)
    u = jnp.sqrt(1.0 - a * a) * (gi * xc)

    row = lax.broadcasted_iota(I32, (tm, width), 0)
    d = 1
    while d < tm:
        keep = row >= d
        u = u + jnp.where(keep, a * pltpu.roll(u, d, 0), 0.0)
        a = jnp.where(keep, a * pltpu.roll(a, d, 0), a)
        d *= 2
    h = u + a * h_ref[0:1, :]
    h_ref[...] = jnp.broadcast_to(h[tm - 1:tm, :], h_ref.shape)
    o_ref[...] = (h * jax.nn.gelu(gate_ref[...])).astype(o_ref.dtype)


def rglru(proj, cw, cb, wg, bg, lam, bsz, seq, width, tm):
    nt = seq // tm
    kern = functools.partial(_rglru_kernel, tm=tm, width=width)
    row = lambda c: pl.BlockSpec((1, c), lambda b, i: (0, 0))
    return pl.pallas_call(
        kern,
        grid=(bsz, nt),
        in_specs=[pl.BlockSpec((tm, width), lambda b, i: (b * nt + i, 0)),
                  pl.BlockSpec((tm, width), lambda b, i: (b * nt + i, 1)),
                  pl.BlockSpec((CONV_WIDTH, width), lambda b, i: (0, 0)),
                  row(width),
                  pl.BlockSpec((width, 2 * width), lambda b, i: (0, 0)),
                  row(2 * width),
                  row(width)],
        out_specs=pl.BlockSpec((tm, width), lambda b, i: (b * nt + i, 0)),
        out_shape=jax.ShapeDtypeStruct((bsz * seq, width), BF16),
        scratch_shapes=[pltpu.VMEM((tm + SUBLANES, width), F32), pltpu.VMEM((SUBLANES, width), F32)],
        compiler_params=_params(2),
    )(proj, proj, cw, cb.reshape(1, -1), wg, bg.reshape(1, -1), lam.reshape(1, -1))


def _swa_kernel(sink_ref, q_ref, k_ref, v_ref, cos_ref, sin_ref, qg_ref, kg_ref, o_ref, kall_ref, vall_ref,
                *, q_heads, kv_heads):
    w = ATT_WINDOW
    n = pl.program_id(1)
    group = q_heads // kv_heads
    assert kv_heads * ATT_HEAD_DIM == LANES and group % 2 == 0

    @pl.when(n == 0)
    def _():
        kall_ref[0:w, :] = jnp.zeros((w, LANES), BF16)
        vall_ref[0:w, :] = jnp.zeros((w, LANES), BF16)

    @pl.when(n > 0)
    def _():
        kall_ref[0:w, :] = kall_ref[w:2 * w, :]
        vall_ref[0:w, :] = vall_ref[w:2 * w, :]

    lane = lax.broadcasted_iota(I32, (w, LANES), 1)
    first_half = (lane % ATT_HEAD_DIM) < (ATT_HEAD_DIM // 2)
    lane_head = lane // ATT_HEAD_DIM
    bd_r = lax.broadcasted_iota(I32, (LANES, LANES), 0) // ATT_HEAD_DIM
    bd_c = lax.broadcasted_iota(I32, (LANES, LANES), 1) // ATT_HEAD_DIM
    head_ones = (bd_r == bd_c).astype(BF16)
    cos = cos_ref[...]
    sin = sin_ref[...]

    def norm_rope(x, g):
        hi, lo = _split_bf16(x * x)
        ss = _dot(hi, head_ones) + _dot(lo, head_ones)
        xn = x * lax.rsqrt(ss * (1.0 / ATT_HEAD_DIM) + EPS) * g
        rot = jnp.where(first_half, pltpu.roll(xn, LANES - ATT_HEAD_DIM // 2, 1),
                        pltpu.roll(xn, ATT_HEAD_DIM // 2, 1))
        return xn * cos + rot * sin

    kall_ref[w:2 * w, :] = norm_rope(k_ref[...], kg_ref[...]).astype(BF16)
    vall_ref[w:2 * w, :] = v_ref[...].astype(BF16)
    qc = [norm_rope(q_ref[:, LANES * c:LANES * (c + 1)], qg_ref[...]) for c in range(q_heads // 2)]

    rows = group * w
    qi = lax.broadcasted_iota(I32, (rows, 2 * w), 0) % w
    kj = lax.broadcasted_iota(I32, (rows, 2 * w), 1)
    valid = (kj > qi) & (kj <= qi + w) & ((n > 0) | (kj >= w))
    row_head = lax.broadcasted_iota(I32, (rows, 1), 0) // w

    outs = []
    for g in range(kv_heads):
        parts = []
        for hh in range(group):
            h = g * group + hh
            x = qc[h // 2]
            if h % 2 != g:
                x = pltpu.roll(x, ATT_HEAD_DIM, 1)
            parts.append(jnp.where(lane_head == g, x, 0.0))
        qs = jnp.concatenate(parts, axis=0).astype(BF16)
        s = _dot_nt(qs, kall_ref[...]) * (ATT_HEAD_DIM ** -0.5)
        s = jnp.where(valid, s, -1e30)
        sink = jnp.zeros((rows, 1), F32)
        for hh in range(group):
            sink = jnp.where(row_head == hh, sink_ref[g * group + hh], sink)
        m = jnp.maximum(jnp.max(s, axis=-1, keepdims=True), sink)
        p = jnp.exp(s - m)
        denom = jnp.sum(p, axis=-1, keepdims=True) + jnp.exp(sink - m)
        pv = _dot(p.astype(BF16), vall_ref[...]) / denom
        outs.extend(pv[w * hh:w * (hh + 1), :] for hh in range(group))

    for c in range(q_heads // 2):
        g = (2 * c) // group
        a, b = outs[2 * c], outs[2 * c + 1]
        if g == 1:
            a = pltpu.roll(a, ATT_HEAD_DIM, 1)
        else:
            b = pltpu.roll(b, ATT_HEAD_DIM, 1)
        o_ref[:, LANES * c:LANES * (c + 1)] = jnp.where(lane_head == 0, a, b).astype(o_ref.dtype)


def swa(proj, sinks, cos, sin, qg, kg, bsz, seq, q_col, k_col, v_col, q_heads, kv_heads):
    w = ATT_WINDOW
    nb = seq // w
    qw = q_heads * ATT_HEAD_DIM
    kern = functools.partial(_swa_kernel, q_heads=q_heads, kv_heads=kv_heads)
    return pl.pallas_call(
        kern,
        grid=(bsz, nb),
        in_specs=[pl.BlockSpec(memory_space=pltpu.SMEM),
                  pl.BlockSpec((w, qw), lambda b, i: (b * nb + i, q_col // qw)),
                  pl.BlockSpec((w, LANES), lambda b, i: (b * nb + i, k_col // LANES)),
                  pl.BlockSpec((w, LANES), lambda b, i: (b * nb + i, v_col // LANES)),
                  pl.BlockSpec((w, LANES), lambda b, i: (i, 0)),
                  pl.BlockSpec((w, LANES), lambda b, i: (i, 0)),
                  pl.BlockSpec((1, LANES), lambda b, i: (0, 0)),
                  pl.BlockSpec((1, LANES), lambda b, i: (0, 0))],
        out_specs=pl.BlockSpec((w, qw), lambda b, i: (b * nb + i, 0)),
        out_shape=jax.ShapeDtypeStruct((bsz * seq, qw), BF16),
        scratch_shapes=[pltpu.VMEM((2 * w, LANES), BF16), pltpu.VMEM((2 * w, LANES), BF16)],
        compiler_params=_params(2),
    )(sinks, proj, proj, proj, cos, sin, qg, kg)


def _proj_res_kernel(*refs, n_in):
    h_ref, o_ref = refs[0], refs[-1]
    acc = h_ref[...]
    for y_ref, w_ref in zip(refs[1:1 + n_in], refs[1 + n_in:1 + 2 * n_in]):
        acc = acc + _dot(y_ref[...], w_ref[...])
    o_ref[...] = acc


def proj_res(h, ys, ws, tm):
    t, d = h.shape
    kern = functools.partial(_proj_res_kernel, n_in=len(ys))
    return pl.pallas_call(
        kern,
        grid=(t // tm,),
        in_specs=([pl.BlockSpec((tm, d), lambda i: (i, 0))]
                  + [pl.BlockSpec((tm, y.shape[1]), lambda i: (i, 0)) for y in ys]
                  + [pl.BlockSpec(w.shape, lambda i: (0, 0)) for w in ws]),
        out_specs=pl.BlockSpec((tm, d), lambda i: (i, 0)),
        out_shape=jax.ShapeDtypeStruct((t, d), F32),
        compiler_params=_params(1),
    )(h, *ys, *ws)


def _ffn_kernel(h_ref, g_ref, wg_ref, wu_ref, wd_ref, o_ref, xn_ref):
    j = pl.program_id(1)

    @pl.when(j == 0)
    def _():
        x = h_ref[...]
        xn_ref[...] = _rms(x, g_ref[...]).astype(BF16)
        o_ref[...] = x

    xn = xn_ref[...]
    act = (jax.nn.silu(_dot(xn, wg_ref[...])) * _dot(xn, wu_ref[...])).astype(BF16)
    o_ref[...] += _dot(act, wd_ref[...])


def ffn(h, g, wg, wu, wd, tm, tf):
    t, d = h.shape
    f = wg.shape[1]
    return pl.pallas_call(
        _ffn_kernel,
        grid=(t // tm, f // tf),
        in_specs=[pl.BlockSpec((tm, d), lambda i, j: (i, 0)),
                  pl.BlockSpec((1, d), lambda i, j: (0, 0)),
                  pl.BlockSpec((d, tf), lambda i, j: (0, j)),
                  pl.BlockSpec((d, tf), lambda i, j: (0, j)),
                  pl.BlockSpec((tf, d), lambda i, j: (j, 0))],
        out_specs=pl.BlockSpec((tm, d), lambda i, j: (i, 0)),
        out_shape=jax.ShapeDtypeStruct((t, d), F32),
        scratch_shapes=[pltpu.VMEM((tm, d), BF16)],
        compiler_params=_params(2),
    )(h, g.reshape(1, d), wg, wu, wd)


def _gdn_kernel(qkv_ref, gate_ref, ab_ref, cw_ref, alog_ref, dtb_ref, onorm_ref, o_ref,
                ext_ref, q_ref, k_ref, v_ref, gc_ref, beta_ref, *state_refs, tc, heads):
    hd = GDN_HEAD_DIM
    c = GDN_CHUNK
    width = heads * hd

    @pl.when(pl.program_id(1) == 0)
    def _():
        ext_ref[0:SUBLANES, :] = jnp.zeros((SUBLANES, 3 * width), F32)
        for s_ref in state_refs:
            s_ref[...] = jnp.zeros_like(s_ref)

    qkv = jax.nn.silu(_causal_conv(ext_ref, qkv_ref[...], cw_ref[...], tc))
    for h in range(heads):
        q = qkv[:, hd * h:hd * (h + 1)]
        k = qkv[:, width + hd * h:width + hd * (h + 1)]
        q_ref[:, hd * h:hd * (h + 1)] = (q * lax.rsqrt(jnp.sum(q * q, axis=-1, keepdims=True) + EPS)
                                          * (hd ** -0.5))
        k_ref[:, hd * h:hd * (h + 1)] = k * lax.rsqrt(jnp.sum(k * k, axis=-1, keepdims=True) + EPS)
    v_ref[...] = qkv[:, 2 * width:]

    ab = ab_ref[...]
    g = -jnp.exp(alog_ref[...]) * jax.nn.softplus(ab + dtb_ref[...])
    beta_ref[...] = pltpu.roll(jax.nn.sigmoid(ab), LANES - heads, 1)
    row = lax.broadcasted_iota(I32, (tc, LANES), 0) % c
    d = 1
    while d < c:
        g = g + jnp.where(row >= d, pltpu.roll(g, d, 0), 0.0)
        d *= 2
    gc_ref[...] = g

    ri = lax.broadcasted_iota(I32, (c, c), 0)
    ci = lax.broadcasted_iota(I32, (c, c), 1)
    tri = ri >= ci
    strict = ri > ci
    eye = (ri == ci).astype(F32)

    def chunk_body(n, carry):
        r0 = pl.multiple_of(n * c, c)
        rows = pl.ds(r0, c)
        gc_all = gc_ref[rows, :]
        gr_all = gc_all.T
        beta_all = beta_ref[rows, :]
        hs = range(heads)
        cols = [slice(hd * h, hd * (h + 1)) for h in hs]
        q = [q_ref[rows, cols[h]] for h in hs]
        k = [k_ref[rows, cols[h]] for h in hs]
        v = [v_ref[rows, cols[h]] for h in hs]
        gcol = [gc_all[:, h:h + 1] for h in hs]
        grow = [gr_all[h:h + 1, :] for h in hs]
        beta = [beta_all[:, h:h + 1] for h in hs]
        qk_kk = [_dot_nt(jnp.concatenate([q[h], k[h]], axis=0).astype(BF16), k[h].astype(BF16)) for h in hs]
        decay = [jnp.where(tri, jnp.exp(jnp.where(tri, gcol[h] - grow[h], 0.0)), 0.0) for h in hs]
        m = [-jnp.where(strict, beta[h] * qk_kk[h][c:, :] * decay[h], 0.0) for h in hs]
        p = [eye + m[h] for h in hs]
        e = [_dot(m[h].astype(BF16), m[h].astype(BF16)) for h in hs]
        lvl = 2
        while 2 * lvl < c:
            r = [_dot(jnp.concatenate([p[h], e[h]], axis=0).astype(BF16), e[h].astype(BF16)) for h in hs]
            p = [p[h] + r[h][:c, :] for h in hs]
            e = [r[h][c:, :] for h in hs]
            lvl *= 2
        p = [p[h] + _dot(p[h].astype(BF16), e[h].astype(BF16)) for h in hs]
        eg = [jnp.exp(gcol[h]) for h in hs]
        uw = [_dot(p[h].astype(BF16),
                   jnp.concatenate([beta[h] * v[h], (beta[h] * eg[h]) * k[h]], axis=1).astype(BF16)) for h in hs]
        a_qk = [jnp.where(tri, qk_kk[h][:c, :] * decay[h], 0.0) for h in hs]
        g_last = [gcol[h][c - 1:c, :] for h in hs]
        s = [state_refs[h][...] for h in hs]
        ws_qs = [_dot(jnp.concatenate([uw[h][:, hd:], q[h] * eg[h]], axis=0).astype(BF16), s[h].astype(BF16))
                 for h in hs]
        vb = [(uw[h][:, :hd] - ws_qs[h][:c, :]).astype(BF16) for h in hs]
        o = [ws_qs[h][c:, :] + _dot(a_qk[h].astype(BF16), vb[h]) for h in hs]
        s_new = [s[h] * jnp.exp(g_last[h]) + _dot((k[h] * jnp.exp(g_last[h] - gcol[h])).T.astype(BF16), vb[h])
                 for h in hs]
        for h in hs:
            state_refs[h][...] = s_new[h]
            on = o[h] * lax.rsqrt(jnp.mean(o[h] * o[h], axis=-1, keepdims=True) + EPS) * onorm_ref[...]
            o_ref[rows, cols[h]] = (on * jax.nn.silu(gate_ref[rows, cols[h]])).astype(o_ref.dtype)
        return carry

    lax.fori_loop(0, tc // c, chunk_body, 0)


def gdn(proj, cw, alog, dtb, onorm, bsz, seq, heads, tc):
    nt = seq // tc
    width = heads * GDN_HEAD_DIM
    kern = functools.partial(_gdn_kernel, tc=tc, heads=heads)
    row = lambda n: pl.BlockSpec((1, n), lambda b, i: (0, 0))
    return pl.pallas_call(
        kern,
        grid=(bsz, nt),
        in_specs=[pl.BlockSpec((tc, 3 * width), lambda b, i: (b * nt + i, 0)),
                  pl.BlockSpec((tc, width), lambda b, i: (b * nt + i, 3)),
                  pl.BlockSpec((tc, LANES), lambda b, i: (b * nt + i, 4 * width // LANES)),
                  pl.BlockSpec((CONV_WIDTH, 3 * width), lambda b, i: (0, 0)),
                  row(LANES), row(LANES), row(GDN_HEAD_DIM)],
        out_specs=pl.BlockSpec((tc, width), lambda b, i: (b * nt + i, 0)),
        out_shape=jax.ShapeDtypeStruct((bsz * seq, width), BF16),
        scratch_shapes=[pltpu.VMEM((tc + SUBLANES, 3 * width), F32),
                        pltpu.VMEM((tc, width), F32), pltpu.VMEM((tc, width), F32),
                        pltpu.VMEM((tc, width), F32),
                        pltpu.VMEM((tc, LANES), F32), pltpu.VMEM((tc, LANES), F32),
                        *[pltpu.VMEM((GDN_HEAD_DIM, GDN_HEAD_DIM), F32) for _ in range(heads)]],
        compiler_params=_params(2),
    )(proj, proj, proj, cw, alog, dtb, onorm.reshape(1, -1))


def _router_kernel(h_ref, g_ref, r_ref, route_ref, cnt_ref, *, tm):
    @pl.when(pl.program_id(0) == 0)
    def _():
        cnt_ref[...] = jnp.zeros_like(cnt_ref)

    t_hi, t_lo = _split_bf16(_rms(h_ref[...], g_ref[...]))
    r_hi, r_lo = _split_bf16(r_ref[...])
    logits = _dot(t_hi, r_hi) + (_dot(t_lo, r_hi) + _dot(t_hi, r_lo))
    lane = lax.broadcasted_iota(I32, (tm, LANES), 1)
    lane_f = lane.astype(F32)
    neg = -jnp.inf
    lg = jnp.where(lane < N_EXPERTS, logits, neg)
    m1 = jnp.max(lg, axis=-1, keepdims=True)
    i1 = jnp.min(jnp.where(lg == m1, lane_f, float(LANES)), axis=-1, keepdims=True)
    oh1 = lane_f == i1
    lg2 = jnp.where(oh1, neg, lg)
    m2 = jnp.max(lg2, axis=-1, keepdims=True)
    i2 = jnp.min(jnp.where(lg2 == m2, lane_f, float(LANES)), axis=-1, keepdims=True)
    oh2 = lane_f == i2
    e2 = jnp.exp(m2 - m1)
    w1 = 1.0 / (1.0 + e2)
    w2 = e2 / (1.0 + e2)

    both = (oh1 | oh2).astype(BF16)
    ri = lax.broadcasted_iota(I32, (tm, tm), 0)
    ci = lax.broadcasted_iota(I32, (tm, tm), 1)
    pos = _dot((ri > ci).astype(BF16), both) + cnt_ref[0:1, :]
    rank1 = jnp.sum(jnp.where(oh1, pos, 0.0), axis=-1, keepdims=True)
    rank2 = jnp.sum(jnp.where(oh2, pos, 0.0), axis=-1, keepdims=True)
    cnt_ref[...] = cnt_ref[...] + jnp.sum(both.astype(F32), axis=0, keepdims=True)

    out = jnp.where(lane == 0, i1, 0.0)
    out = jnp.where(lane == 1, i2, out)
    out = jnp.where(lane == 2, rank1, out)
    out = jnp.where(lane == 3, rank2, out)
    out = jnp.where(lane == 4, w1, out)
    route_ref[...] = jnp.where(lane == 5, w2, out)


def router(h, g, r, tm):
    t, d = h.shape
    kern = functools.partial(_router_kernel, tm=tm)
    return pl.pallas_call(
        kern,
        grid=(t // tm,),
        in_specs=[pl.BlockSpec((tm, d), lambda i: (i, 0)),
                  pl.BlockSpec((1, d), lambda i: (0, 0)),
                  pl.BlockSpec((d, LANES), lambda i: (0, 0))],
        out_specs=[pl.BlockSpec((tm, LANES), lambda i: (i, 0)),
                   pl.BlockSpec((SUBLANES, LANES), lambda i: (0, 0))],
        out_shape=[jax.ShapeDtypeStruct((t, LANES), F32),
                   jax.ShapeDtypeStruct((SUBLANES, LANES), F32)],
        compiler_params=_params(1),
    )(h, g.reshape(1, d), r)


def _moe_kernel(te_ref, nu_ref, code_ref, h_hbm, g_ref, wg_ref, wu_ref, wd_ref, out_hbm,
                buf, xn_ref, gsem, ssem, *, tm, n_tok, n_ff):
    i = pl.program_id(0)
    j = pl.program_id(1)
    last_step = (i == pl.num_programs(0) - 1) & (j == pl.num_programs(1) - 1)
    n_used = nu_ref[0]
    slot = i % 2
    other = 1 - slot
    rows_per_step = tm // n_ff
    pow2 = n_tok & (n_tok - 1) == 0

    def gather_copy(idx, buf_slot, r):
        code = code_ref[idx]
        tok = (code & (n_tok - 1)) if pow2 else lax.rem(code, jnp.int32(n_tok))
        return pltpu.make_async_copy(h_hbm.at[pl.ds(tok, 1)], buf.at[buf_slot, pl.ds(r, 1)], gsem.at[buf_slot])

    def scatter_copy(idx, buf_slot, r):
        return pltpu.make_async_copy(buf.at[2 + buf_slot, pl.ds(r, 1)], out_hbm.at[pl.ds(code_ref[idx], 1)],
                                     ssem.at[buf_slot])

    def wait_gather(buf_slot):
        pltpu.make_async_copy(h_hbm.at[pl.ds(0, tm)], buf.at[buf_slot], gsem.at[buf_slot]).wait()

    def wait_scatter(buf_slot):
        pltpu.make_async_copy(buf.at[2 + buf_slot], out_hbm.at[pl.ds(0, tm)], ssem.at[buf_slot]).wait()

    def issue_rows(lo, hi):
        for r in range(lo, hi):
            row = j * rows_per_step + r
            gather_copy((i + 2) * tm + row, other, row).start()
            scatter_copy(i * tm + row, other, row).start()

    @pl.when(j == 0)
    def _():
        @pl.when(i == 0)
        def _():
            buf[2] = jnp.zeros((tm, buf.shape[2]), F32)
            buf[3] = jnp.zeros((tm, buf.shape[2]), F32)

            def body(r, carry):
                gather_copy(tm + r, 0, r).start()
                return carry
            lax.fori_loop(0, tm, body, 0)

        wait_gather(slot)

        @pl.when(i >= 1)
        def _():
            wait_scatter(slot)

        @pl.when(i < n_used)
        def _():
            xn_ref[...] = _rms(buf[slot], g_ref[...]).astype(BF16)

    @pl.when(i < n_used)
    def _():
        xn = xn_ref[...]
        d = buf.shape[2]
        tf = wg_ref.shape[2]
        n_chunks = tf // MXU_WIDTH
        per_slot = d // MXU_WIDTH
        for c in range(n_chunks):
            issue_rows(rows_per_step * c // n_chunks, rows_per_step * (c + 1) // n_chunks)
            cols = slice(MXU_WIDTH * c, MXU_WIDTH * (c + 1))
            a = jax.nn.silu(_dot(xn, wg_ref[0, :, cols])) * _dot(xn, wu_ref[0, :, cols])
            buf[4 + c // per_slot, :, MXU_WIDTH * (c % per_slot):MXU_WIDTH * (c % per_slot + 1)] = a
        act = jnp.concatenate([buf[4 + s, :, 0:min(d, tf - s * d)] for s in range(pl.cdiv(tf, d))], axis=1)
        part = _dot(act.astype(BF16), wd_ref[0])
        buf[2 + slot] = jnp.where(j == 0, 0.0, buf[2 + slot]) + part

    @pl.when(i >= n_used)
    def _():
        issue_rows(0, rows_per_step)

    @pl.when(last_step)
    def _():
        wait_gather(other)
        wait_scatter(other)


def moe_experts(tile_expert, n_used, code, h, g, wg, wu, wd, n_tiles, tm, tf):
    t, d = h.shape
    f = wg.shape[2]
    n_ff = f // tf
    assert tm % n_ff == 0 and code.shape[0] == (n_tiles + 2) * tm

    def ff_idx(i, j, nu):
        return jnp.where(i < nu[0], j, n_ff - 1)

    kern = functools.partial(_moe_kernel, tm=tm, n_tok=t, n_ff=n_ff)
    grid_spec = pltpu.PrefetchScalarGridSpec(
        num_scalar_prefetch=3,
        grid=(n_tiles, n_ff),
        in_specs=[pl.BlockSpec(memory_space=pl.ANY),
                  pl.BlockSpec((1, d), lambda i, j, te, nu, code: (0, 0)),
                  pl.BlockSpec((1, d, tf), lambda i, j, te, nu, code: (te[i], 0, ff_idx(i, j, nu))),
                  pl.BlockSpec((1, d, tf), lambda i, j, te, nu, code: (te[i], 0, ff_idx(i, j, nu))),
                  pl.BlockSpec((1, tf, d), lambda i, j, te, nu, code: (te[i], ff_idx(i, j, nu), 0))],
        out_specs=pl.BlockSpec(memory_space=pl.ANY),
        scratch_shapes=[pltpu.VMEM((4 + pl.cdiv(tf, d), tm, d), F32), pltpu.VMEM((tm, d), BF16),
                        pltpu.SemaphoreType.DMA((2,)), pltpu.SemaphoreType.DMA((2,))],
    )
    return pl.pallas_call(
        kern,
        grid_spec=grid_spec,
        out_shape=jax.ShapeDtypeStruct((n_tiles * tm, d), F32),
        compiler_params=_params(2),
    )(tile_expert, n_used, code, h, g.reshape(1, d), wg, wu, wd)


def _combine_kernel(h_ref, route_ref, y1_ref, y2_ref, o_ref):
    route = route_ref[...]
    o_ref[...] = h_ref[...] + route[:, 4:5] * y1_ref[...] + route[:, 5:6] * y2_ref[...]


def moe_combine(h, route, y, tm):
    t, d = h.shape
    nb = t // tm
    return pl.pallas_call(
        _combine_kernel,
        grid=(nb,),
        in_specs=[pl.BlockSpec((tm, d), lambda i: (i, 0)),
                  pl.BlockSpec((tm, LANES), lambda i: (i, 0)),
                  pl.BlockSpec((tm, d), lambda i: (i, 0)),
                  pl.BlockSpec((tm, d), lambda i: (nb + i, 0))],
        out_specs=pl.BlockSpec((tm, d), lambda i: (i, 0)),
        out_shape=jax.ShapeDtypeStruct((t, d), F32),
        compiler_params=_params(1),
    )(h, route, y, y)


def _rope_tables(seq):
    half = ATT_HEAD_DIM // 2
    inv_freq = ROPE_THETA ** (-jnp.arange(half, dtype=F32) / half)
    ang = jnp.arange(seq, dtype=jnp.int32).astype(F32)[:, None] * inv_freq[None, :]
    reps = LANES // half
    cos = jnp.tile(jnp.cos(ang), (1, reps))
    sign = jnp.tile(jnp.concatenate([-jnp.ones((half,), F32), jnp.ones((half,), F32)]), LANES // ATT_HEAD_DIM)
    sin = jnp.tile(jnp.sin(ang), (1, reps)) * sign[None, :]
    return cos, sin


def _even_layer(h, bsz, seq, ln_mix, ln_ffn, w_in, conv_w, conv_b, w_a, b_a, w_i, b_i, lam,
                q_norm, k_norm, sinks, w_out, f_gate, f_up, f_down):
    lru_w = conv_w.shape[1]
    q_heads = sinks.shape[0]
    q_w = q_heads * ATT_HEAD_DIM
    kv_w = (w_in.shape[1] - 2 * lru_w - q_w) // 2
    kv_heads = kv_w // ATT_HEAD_DIM

    proj = rms_matmul(h, ln_mix, w_in.astype(BF16), tm=512, tn=w_in.shape[1])
    gates_w = jnp.concatenate([block_diag(*w_a), block_diag(*w_i)], axis=1).astype(BF16)
    y_lru = rglru(proj, conv_w, conv_b, gates_w, jnp.concatenate([b_a, b_i]), lam, bsz, seq, lru_w, tm=256)
    cos, sin = _rope_tables(seq)
    tile2 = lambda g: jnp.tile(g, LANES // ATT_HEAD_DIM).reshape(1, LANES)
    y_att = swa(proj, sinks, cos, sin, tile2(q_norm), tile2(k_norm), bsz, seq,
                q_col=2 * lru_w, k_col=2 * lru_w + q_w, v_col=2 * lru_w + q_w + kv_w,
                q_heads=q_heads, kv_heads=kv_heads)
    w_out = w_out.astype(BF16)
    h = proj_res(h, [y_lru, y_att], [w_out[:lru_w], w_out[lru_w:]], tm=512)
    return ffn(h, ln_ffn, f_gate.astype(BF16), f_up.astype(BF16), f_down.astype(BF16), tm=512, tf=1408)


def _odd_layer(h, bsz, seq, ln_mix, ln_ffn, w_in, conv_w, a_log, dt_bias, out_norm, w_out, router_w,
               m_gate, m_up, m_down):
    t, d = h.shape
    heads = a_log.shape[0]
    width = heads * GDN_HEAD_DIM
    cols = w_in.shape[1]
    pad = (-cols) % (11 * LANES)
    w_in_p = jnp.pad(w_in, ((0, 0), (0, pad))).astype(BF16)
    proj = rms_matmul(h, ln_mix, w_in_p, tm=512, tn=w_in_p.shape[1] // 3)
    lane_pad = lambda v: jnp.pad(v, (0, LANES - heads)).reshape(1, LANES)
    y = gdn(proj, conv_w, lane_pad(a_log), lane_pad(dt_bias), out_norm, bsz, seq, heads, tc=256)
    h = proj_res(h, [y], [w_out.astype(BF16)], tm=512)

    route, counts = router(h, ln_ffn, jnp.pad(router_w, ((0, 0), (0, LANES - N_EXPERTS))), tm=512)
    tm = MOE_TILE
    n_tiles = (2 * t) // tm + N_EXPERTS + 1
    counts = counts[0, :N_EXPERTS].astype(I32)
    tiles_per = (counts + tm - 1) // tm
    tile_end = jnp.cumsum(tiles_per)
    starts = (tile_end - tiles_per) * tm
    n_used = tile_end[-1:]
    tile_ids = jnp.minimum(jnp.arange(n_tiles, dtype=I32), n_used - 1)
    tile_expert = jnp.sum((tile_ids[:, None] >= tile_end[None, :]).astype(I32), axis=1)
    e1 = route[:, 0].astype(I32)
    e2 = route[:, 1].astype(I32)
    slot1 = starts[e1] + route[:, 2].astype(I32)
    slot2 = starts[e2] + route[:, 3].astype(I32)
    tok = jnp.arange(t, dtype=I32)
    idx = jnp.arange((n_tiles + 2) * tm, dtype=I32)
    slot = idx - tm
    seg = jnp.sum((slot[:, None] >= (tile_end * tm)[None, :]).astype(I32), axis=1)
    seg_c = jnp.minimum(seg, N_EXPERTS - 1)
    routed = (slot >= 0) & (seg < N_EXPERTS) & (slot - starts[seg_c] < counts[seg_c])
    scattered = idx < n_tiles * tm
    dump_id = jnp.cumsum((scattered & ~routed).astype(I32)) - 1
    code = jnp.where(scattered, 2 * t + dump_id, 0).at[jnp.concatenate([slot1, slot2]) + tm].set(
        jnp.concatenate([tok, tok + t]))

    y = moe_experts(tile_expert, n_used, code, h, ln_ffn, m_gate.astype(BF16), m_up.astype(BF16),
                    m_down.astype(BF16), n_tiles, tm, MOE_FF_CHUNK)
    return moe_combine(h, route, y, tm=512)


def kernel(x, ln_mix, ln_ffn, e_w_in, e_lru_conv_w, e_lru_conv_b, e_lru_w_a, e_lru_b_a, e_lru_w_i, e_lru_b_i, e_lru_lambda, e_q_norm, e_k_norm, e_sinks, e_w_out, e_ffn_w_gate, e_ffn_w_up, e_ffn_w_down, o_w_in, o_conv_w, o_a_log, o_dt_bias, o_out_norm, o_w_out, o_router, o_moe_w_gate, o_moe_w_up, o_moe_w_down):
    bsz, seq, d = x.shape
    h = x.reshape(bsz * seq, d)
    for layer in range(ln_mix.shape[0]):
        j = layer // 2
        if layer % 2 == 0:
            h = _even_layer(h, bsz, seq, ln_mix[layer], ln_ffn[layer], e_w_in[j], e_lru_conv_w[j],
                            e_lru_conv_b[j], e_lru_w_a[j], e_lru_b_a[j], e_lru_w_i[j], e_lru_b_i[j],
                            e_lru_lambda[j], e_q_norm[j], e_k_norm[j], e_sinks[j], e_w_out[j],
                            e_ffn_w_gate[j], e_ffn_w_up[j], e_ffn_w_down[j])
        else:
            h = _odd_layer(h, bsz, seq, ln_mix[layer], ln_ffn[layer], o_w_in[j], o_conv_w[j], o_a_log[j],
                           o_dt_bias[j], o_out_norm[j], o_w_out[j], o_router[j], o_moe_w_gate[j],
                           o_moe_w_up[j], o_moe_w_down[j])
    return h.reshape(bsz, seq, d)
```

```python
import functools

import jax
import jax.numpy as jnp
from jax import lax
from jax.scipy.linalg import block_diag
from jax.experimental import pallas as pl
from jax.experimental.pallas import tpu as pltpu

F32 = jnp.float32
BF16 = jnp.bfloat16
I32 = jnp.int32

EPS = 1e-6
LANES = 128
SUBLANES = 8
MXU_WIDTH = 256
VMEM_LIMIT = 52 * 1024 * 1024

CONV_WIDTH = 4
LRU_C = 8.0
ATT_HEAD_DIM = 64
ATT_WINDOW = 128
ROPE_THETA = 10000.0
GDN_HEAD_DIM = 128
GDN_CHUNK = 64
N_EXPERTS = 8
MOE_TILE = 512
MOE_FF_CHUNK = 1792


def _params(n_axes, vmem=VMEM_LIMIT):
    return pltpu.CompilerParams(dimension_semantics=("arbitrary",) * n_axes, vmem_limit_bytes=vmem)


def _rms(x, g):
    return x * lax.rsqrt(jnp.mean(x * x, axis=-1, keepdims=True) + EPS) * g


def _dot(a, b):
    return jnp.dot(a, b, preferred_element_type=F32)


def _dot_nt(a, b):
    return lax.dot_general(a, b, (((1,), (1,)), ((), ())), preferred_element_type=F32)


def _split_bf16(x):
    hi = x.astype(BF16)
    lo = (x - hi.astype(F32)).astype(BF16)
    return hi, lo


def _rms_matmul_kernel(x_ref, g_ref, w_ref, o_ref, xn_ref):
    @pl.when(pl.program_id(1) == 0)
    def _():
        xn_ref[...] = _rms(x_ref[...], g_ref[...]).astype(BF16)

    o_ref[...] = _dot(xn_ref[...], w_ref[...])


def rms_matmul(x, g, w, tm, tn):
    t, d = x.shape
    n = w.shape[1]
    return pl.pallas_call(
        _rms_matmul_kernel,
        grid=(t // tm, n // tn),
        in_specs=[pl.BlockSpec((tm, d), lambda i, j: (i, 0)),
                  pl.BlockSpec((1, d), lambda i, j: (0, 0)),
                  pl.BlockSpec((d, tn), lambda i, j: (0, j))],
        out_specs=pl.BlockSpec((tm, tn), lambda i, j: (i, j)),
        out_shape=jax.ShapeDtypeStruct((t, n), F32),
        scratch_shapes=[pltpu.VMEM((tm, d), BF16)],
        compiler_params=_params(2),
    )(x, g.reshape(1, d), w)


def _causal_conv(ext_ref, x, cw, tm):
    ext_ref[SUBLANES:SUBLANES + tm, :] = x
    base = SUBLANES - (CONV_WIDTH - 1)
    acc = cw[0:1, :] * ext_ref[base:base + tm, :]
    for j in range(1, CONV_WIDTH):
        acc = acc + cw[j:j + 1, :] * ext_ref[base + j:base + j + tm, :]
    ext_ref[0:SUBLANES, :] = x[tm - SUBLANES:tm, :]
    return acc


def _rglru_kernel(xb_ref, gate_ref, cw_ref, cb_ref, wg_ref, bg_ref, lam_ref, o_ref, ext_ref, h_ref,
                  *, tm, width):
    @pl.when(pl.program_id(1) == 0)
    def _():
        ext_ref[0:SUBLANES, :] = jnp.zeros((SUBLANES, width), F32)
        h_ref[...] = jnp.zeros_like(h_ref)

    xc = _causal_conv(ext_ref, xb_ref[...], cw_ref[...], tm) + cb_ref[...]
    z = _dot(xc.astype(BF16), wg_ref[...]) + bg_ref[...]
    r = jax.nn.sigmoid(z[:, :width])
    gi = jax.nn.sigmoid(z[:, width:])
    log_a = -LRU_C * r * jax.nn.softplus(-lam_ref[...])
    a = jnp.exp(log_a)
    u = jnp.sqrt(1.0 - a * a) * (gi * xc)

    row = lax.broadcasted_iota(I32, (tm, width), 0)
    d = 1
    while d < tm:
        keep = row >= d
        u = u + jnp.where(keep, a * pltpu.roll(u, d, 0), 0.0)
        a = jnp.where(keep, a * pltpu.roll(a, d, 0), a)
        d *= 2
    h = u + a * h_ref[0:1, :]
    h_ref[...] = jnp.broadcast_to(h[tm - 1:tm, :], h_ref.shape)
    o_ref[...] = (h * jax.nn.gelu(gate_ref[...])).astype(o_ref.dtype)


def rglru(proj, cw, cb, wg, bg, lam, bsz, seq, width, tm):
    nt = seq // tm
    kern = functools.partial(_rglru_kernel, tm=tm, width=width)
    row = lambda c: pl.BlockSpec((1, c), lambda b, i: (0, 0))
    return pl.pallas_call(
        kern,
        grid=(bsz, nt),
        in_specs=[pl.BlockSpec((tm, width), lambda b, i: (b * nt + i, 0)),
                  pl.BlockSpec((tm, width), lambda b, i: (b * nt + i, 1)),
                  pl.BlockSpec((CONV_WIDTH, width), lambda b, i: (0, 0)),
                  row(width),
                  pl.BlockSpec((width, 2 * width), lambda b, i: (0, 0)),
                  row(2 * width),
                  row(width)],
        out_specs=pl.BlockSpec((tm, width), lambda b, i: (b * nt + i, 0)),
        out_shape=jax.ShapeDtypeStruct((bsz * seq, width), BF16),
        scratch_shapes=[pltpu.VMEM((tm + SUBLANES, width), F32), pltpu.VMEM((SUBLANES, width), F32)],
        compiler_params=_params(2),
    )(proj, proj, cw, cb.reshape(1, -1), wg, bg.reshape(1, -1), lam.reshape(1, -1))


def _swa_kernel(sink_ref, q_ref, k_ref, v_ref, cos_ref, sin_ref, qg_ref, kg_ref, o_ref, kall_ref, vall_ref,
                *, q_heads, kv_heads):
    w = ATT_WINDOW
    n = pl.program_id(1)
    group = q_heads // kv_heads
    assert kv_heads * ATT_HEAD_DIM == LANES and group % 2 == 0

    @pl.when(n == 0)
    def _():
        kall_ref[0:w, :] = jnp.zeros((w, LANES), BF16)
        vall_ref[0:w, :] = jnp.zeros((w, LANES), BF16)

    @pl.when(n > 0)
    def _():
        kall_ref[0:w, :] = kall_ref[w:2 * w, :]
        vall_ref[0:w, :] = vall_ref[w:2 * w, :]

    lane = lax.broadcasted_iota(I32, (w, LANES), 1)
    first_half = (lane % ATT_HEAD_DIM) < (ATT_HEAD_DIM // 2)
    lane_head = lane // ATT_HEAD_DIM
    bd_r = lax.broadcasted_iota(I32, (LANES, LANES), 0) // ATT_HEAD_DIM
    bd_c = lax.broadcasted_iota(I32, (LANES, LANES), 1) // ATT_HEAD_DIM
    head_ones = (bd_r == bd_c).astype(BF16)
    cos = cos_ref[...]
    sin = sin_ref[...]

    def norm_rope(x, g):
        hi, lo = _split_bf16(x * x)
        ss = _dot(hi, head_ones) + _dot(lo, head_ones)
        xn = x * lax.rsqrt(ss * (1.0 / ATT_HEAD_DIM) + EPS) * g
        rot = jnp.where(first_half, pltpu.roll(xn, LANES - ATT_HEAD_DIM // 2, 1),
                        pltpu.roll(xn, ATT_HEAD_DIM // 2, 1))
        return xn * cos + rot * sin

    kall_ref[w:2 * w, :] = norm_rope(k_ref[...], kg_ref[...]).astype(BF16)
    vall_ref[w:2 * w, :] = v_ref[...].astype(BF16)
    qc = [norm_rope(q_ref[:, LANES * c:LANES * (c + 1)], qg_ref[...]) for c in range(q_heads // 2)]

    rows = group * w
    qi = lax.broadcasted_iota(I32, (rows, 2 * w), 0) % w
    kj = lax.broadcasted_iota(I32, (rows, 2 * w), 1)
    valid = (kj > qi) & (kj <= qi + w) & ((n > 0) | (kj >= w))
    row_head = lax.broadcasted_iota(I32, (rows, 1), 0) // w

    outs = []
    for g in range(kv_heads):
        parts = []
        for hh in range(group):
            h = g * group + hh
            x = qc[h // 2]
            if h % 2 != g:
                x = pltpu.roll(x, ATT_HEAD_DIM, 1)
            parts.append(jnp.where(lane_head == g, x, 0.0))
        qs = jnp.concatenate(parts, axis=0).astype(BF16)
        s = _dot_nt(qs, kall_ref[...]) * (ATT_HEAD_DIM ** -0.5)
        s = jnp.where(valid, s, -1e30)
        sink = jnp.zeros((rows, 1), F32)
        for hh in range(group):
            sink = jnp.where(row_head == hh, sink_ref[g * group + hh], sink)
        m = jnp.maximum(jnp.max(s, axis=-1, keepdims=True), sink)
        p = jnp.exp(s - m)
        denom = jnp.sum(p, axis=-1, keepdims=True) + jnp.exp(sink - m)
        pv = _dot(p.astype(BF16), vall_ref[...]) / denom
        outs.extend(pv[w * hh:w * (hh + 1), :] for hh in range(group))

    for c in range(q_heads // 2):
        g = (2 * c) // group
        a, b = outs[2 * c], outs[2 * c + 1]
        if g == 1:
            a = pltpu.roll(a, ATT_HEAD_DIM, 1)
        else:
            b = pltpu.roll(b, ATT_HEAD_DIM, 1)
        o_ref[:, LANES * c:LANES * (c + 1)] = jnp.where(lane_head == 0, a, b).astype(o_ref.dtype)


def swa(proj, sinks, cos, sin, qg, kg, bsz, seq, q_col, k_col, v_col, q_heads, kv_heads):
    w = ATT_WINDOW
    nb = seq // w
    qw = q_heads * ATT_HEAD_DIM
    kern = functools.partial(_swa_kernel, q_heads=q_heads, kv_heads=kv_heads)
    return pl.pallas_call(
        kern,
        grid=(bsz, nb),
        in_specs=[pl.BlockSpec(memory_space=pltpu.SMEM),
                  pl.BlockSpec((w, qw), lambda b, i: (b * nb + i, q_col // qw)),
                  pl.BlockSpec((w, LANES), lambda b, i: (b * nb + i, k_col // LANES)),
                  pl.BlockSpec((w, LANES), lambda b, i: (b * nb + i, v_col // LANES)),
                  pl.BlockSpec((w, LANES), lambda b, i: (i, 0)),
                  pl.BlockSpec((w, LANES), lambda b, i: (i, 0)),
                  pl.BlockSpec((1, LANES), lambda b, i: (0, 0)),
                  pl.BlockSpec((1, LANES), lambda b, i: (0, 0))],
        out_specs=pl.BlockSpec((w, qw), lambda b, i: (b * nb + i, 0)),
        out_shape=jax.ShapeDtypeStruct((bsz * seq, qw), BF16),
        scratch_shapes=[pltpu.VMEM((2 * w, LANES), BF16), pltpu.VMEM((2 * w, LANES), BF16)],
        compiler_params=_params(2),
    )(sinks, proj, proj, proj, cos, sin, qg, kg)


def _proj_res_kernel(*refs, n_in):
    h_ref, o_ref = refs[0], refs[-1]
    acc = h_ref[...]
    for y_ref, w_ref in zip(refs[1:1 + n_in], refs[1 + n_in:1 + 2 * n_in]):
        acc = acc + _dot(y_ref[...], w_ref[...])
    o_ref[...] = acc


def proj_res(h, ys, ws, tm):
    t, d = h.shape
    kern = functools.partial(_proj_res_kernel, n_in=len(ys))
    return pl.pallas_call(
        kern,
        grid=(t // tm,),
        in_specs=([pl.BlockSpec((tm, d), lambda i: (i, 0))]
                  + [pl.BlockSpec((tm, y.shape[1]), lambda i: (i, 0)) for y in ys]
                  + [pl.BlockSpec(w.shape, lambda i: (0, 0)) for w in ws]),
        out_specs=pl.BlockSpec((tm, d), lambda i: (i, 0)),
        out_shape=jax.ShapeDtypeStruct((t, d), F32),
        compiler_params=_params(1),
    )(h, *ys, *ws)


def _ffn_kernel(h_ref, g_ref, wg_ref, wu_ref, wd_ref, o_ref, xn_ref):
    j = pl.program_id(1)

    @pl.when(j == 0)
    def _():
        x = h_ref[...]
        xn_ref[...] = _rms(x, g_ref[...]).astype(BF16)
        o_ref[...] = x

    xn = xn_ref[...]
    act = (jax.nn.silu(_dot(xn, wg_ref[...])) * _dot(xn, wu_ref[...])).astype(BF16)
    o_ref[...] += _dot(act, wd_ref[...])


def ffn(h, g, wg, wu, wd, tm, tf):
    t, d = h.shape
    f = wg.shape[1]
    return pl.pallas_call(
        _ffn_kernel,
        grid=(t // tm, f // tf),
        in_specs=[pl.BlockSpec((tm, d), lambda i, j: (i, 0)),
                  pl.BlockSpec((1, d), lambda i, j: (0, 0)),
                  pl.BlockSpec((d, tf), lambda i, j: (0, j)),
                  pl.BlockSpec((d, tf), lambda i, j: (0, j)),
                  pl.BlockSpec((tf, d), lambda i, j: (j, 0))],
        out_specs=pl.BlockSpec((tm, d), lambda i, j: (i, 0)),
        out_shape=jax.ShapeDtypeStruct((t, d), F32),
        scratch_shapes=[pltpu.VMEM((tm, d), BF16)],
        compiler_params=_params(2),
    )(h, g.reshape(1, d), wg, wu, wd)


def _gdn_kernel(qkv_ref, gate_ref, ab_ref, cw_ref, alog_ref, dtb_ref, onorm_ref, o_ref,
                ext_ref, q_ref, k_ref, v_ref, gc_ref, beta_ref, *state_refs, tc, heads):
    hd = GDN_HEAD_DIM
    c = GDN_CHUNK
    width = heads * hd

    @pl.when(pl.program_id(1) == 0)
    def _():
        ext_ref[0:SUBLANES, :] = jnp.zeros((SUBLANES, 3 * width), F32)
        for s_ref in state_refs:
            s_ref[...] = jnp.zeros_like(s_ref)

    qkv = jax.nn.silu(_causal_conv(ext_ref, qkv_ref[...], cw_ref[...], tc))
    for h in range(heads):
        q = qkv[:, hd * h:hd * (h + 1)]
        k = qkv[:, width + hd * h:width + hd * (h + 1)]
        q_ref[:, hd * h:hd * (h + 1)] = (q * lax.rsqrt(jnp.sum(q * q, axis=-1, keepdims=True) + EPS)
                                          * (hd ** -0.5))
        k_ref[:, hd * h:hd * (h + 1)] = k * lax.rsqrt(jnp.sum(k * k, axis=-1, keepdims=True) + EPS)
    v_ref[...] = qkv[:, 2 * width:]

    ab = ab_ref[...]
    g = -jnp.exp(alog_ref[...]) * jax.nn.softplus(ab + dtb_ref[...])
    beta_ref[...] = pltpu.roll(jax.nn.sigmoid(ab), LANES - heads, 1)
    row = lax.broadcasted_iota(I32, (tc, LANES), 0) % c
    d = 1
    while d < c:
        g = g + jnp.where(row >= d, pltpu.roll(g, d, 0), 0.0)
        d *= 2
    gc_ref[...] = g

    ri = lax.broadcasted_iota(I32, (c, c), 0)
    ci = lax.broadcasted_iota(I32, (c, c), 1)
    tri = ri >= ci
    strict = ri > ci
    eye = (ri == ci).astype(F32)

    def chunk_body(n, carry):
        r0 = pl.multiple_of(n * c, c)
        rows = pl.ds(r0, c)
        gc_all = gc_ref[rows, :]
        gr_all = gc_all.T
        beta_all = beta_ref[rows, :]
        hs = range(heads)
        cols = [slice(hd * h, hd * (h + 1)) for h in hs]
        q = [q_ref[rows, cols[h]] for h in hs]
        k = [k_ref[rows, cols[h]] for h in hs]
        v = [v_ref[rows, cols[h]] for h in hs]
        gcol = [gc_all[:, h:h + 1] for h in hs]
        grow = [gr_all[h:h + 1, :] for h in hs]
        beta = [beta_all[:, h:h + 1] for h in hs]
        qk_kk = [_dot_nt(jnp.concatenate([q[h], k[h]], axis=0).astype(BF16), k[h].astype(BF16)) for h in hs]
        decay = [jnp.where(tri, jnp.exp(jnp.where(tri, gcol[h] - grow[h], 0.0)), 0.0) for h in hs]
        m = [-jnp.where(strict, beta[h] * qk_kk[h][c:, :] * decay[h], 0.0) for h in hs]
        p = [eye + m[h] for h in hs]
        e = [_dot(m[h].astype(BF16), m[h].astype(BF16)) for h in hs]
        lvl = 2
        while 2 * lvl < c:
            r = [_dot(jnp.concatenate([p[h], e[h]], axis=0).astype(BF16), e[h].astype(BF16)) for h in hs]
            p = [p[h] + r[h][:c, :] for h in hs]
            e = [r[h][c:, :] for h in hs]
            lvl *= 2
        p = [p[h] + _dot(p[h].astype(BF16), e[h].astype(BF16)) for h in hs]
        eg = [jnp.exp(gcol[h]) for h in hs]
        uw = [_dot(p[h].astype(BF16),
                   jnp.concatenate([beta[h] * v[h], (beta[h] * eg[h]) * k[h]], axis=1).astype(BF16)) for h in hs]
        a_qk = [jnp.where(tri, qk_kk[h][:c, :] * decay[h], 0.0) for h in hs]
        g_last = [gcol[h][c - 1:c, :] for h in hs]
        s = [state_refs[h][...] for h in hs]
        ws_qs = [_dot(jnp.concatenate([uw[h][:, hd:], q[h] * eg[h]], axis=0).astype(BF16), s[h].astype(BF16))
                 for h in hs]
        vb = [(uw[h][:, :hd] - ws_qs[h][:c, :]).astype(BF16) for h in hs]
        o = [ws_qs[h][c:, :] + _dot(a_qk[h].astype(BF16), vb[h]) for h in hs]
        s_new = [s[h] * jnp.exp(g_last[h]) + _dot((k[h] * jnp.exp(g_last[h] - gcol[h])).T.astype(BF16), vb[h])
                 for h in hs]
        for h in hs:
            state_refs[h][...] = s_new[h]
            on = o[h] * lax.rsqrt(jnp.mean(o[h] * o[h], axis=-1, keepdims=True) + EPS) * onorm_ref[...]
            o_ref[rows, cols[h]] = (on * jax.nn.silu(gate_ref[rows, cols[h]])).astype(o_ref.dtype)
        return carry

    lax.fori_loop(0, tc // c, chunk_body, 0)


def gdn(proj, cw, alog, dtb, onorm, bsz, seq, heads, tc):
    nt = seq // tc
    width = heads * GDN_HEAD_DIM
    kern = functools.partial(_gdn_kernel, tc=tc, heads=heads)
    row = lambda n: pl.BlockSpec((1, n), lambda b, i: (0, 0))
    return pl.pallas_call(
        kern,
        grid=(bsz, nt),
        in_specs=[pl.BlockSpec((tc, 3 * width), lambda b, i: (b * nt + i, 0)),
                  pl.BlockSpec((tc, width), lambda b, i: (b * nt + i, 3)),
                  pl.BlockSpec((tc, LANES), lambda b, i: (b * nt + i, 4 * width // LANES)),
                  pl.BlockSpec((CONV_WIDTH, 3 * width), lambda b, i: (0, 0)),
                  row(LANES), row(LANES), row(GDN_HEAD_DIM)],
        out_specs=pl.BlockSpec((tc, width), lambda b, i: (b * nt + i, 0)),
        out_shape=jax.ShapeDtypeStruct((bsz * seq, width), BF16),
        scratch_shapes=[pltpu.VMEM((tc + SUBLANES, 3 * width), F32),
                        pltpu.VMEM((tc, width), F32), pltpu.VMEM((tc, width), F32),
                        pltpu.VMEM((tc, width), F32),
                        pltpu.VMEM((tc, LANES), F32), pltpu.VMEM((tc, LANES), F32),
                        *[pltpu.VMEM((GDN_HEAD_DIM, GDN_HEAD_DIM), F32) for _ in range(heads)]],
        compiler_params=_params(2),
    )(proj, proj, proj, cw, alog, dtb, onorm.reshape(1, -1))


def _router_kernel(h_ref, g_ref, r_ref, route_ref, cnt_ref, *, tm):
    @pl.when(pl.program_id(0) == 0)
    def _():
        cnt_ref[...] = jnp.zeros_like(cnt_ref)

    t_hi, t_lo = _split_bf16(_rms(h_ref[...], g_ref[...]))
    r_hi, r_lo = _split_bf16(r_ref[...])
    logits = _dot(t_hi, r_hi) + (_dot(t_lo, r_hi) + _dot(t_hi, r_lo))
    lane = lax.broadcasted_iota(I32, (tm, LANES), 1)
    lane_f = lane.astype(F32)
    neg = -jnp.inf
    lg = jnp.where(lane < N_EXPERTS, logits, neg)
    m1 = jnp.max(lg, axis=-1, keepdims=True)
    i1 = jnp.min(jnp.where(lg == m1, lane_f, float(LANES)), axis=-1, keepdims=True)
    oh1 = lane_f == i1
    lg2 = jnp.where(oh1, neg, lg)
    m2 = jnp.max(lg2, axis=-1, keepdims=True)
    i2 = jnp.min(jnp.where(lg2 == m2, lane_f, float(LANES)), axis=-1, keepdims=True)
    oh2 = lane_f == i2
    e2 = jnp.exp(m2 - m1)
    w1 = 1.0 / (1.0 + e2)
    w2 = e2 / (1.0 + e2)

    both = (oh1 | oh2).astype(BF16)
    ri = lax.broadcasted_iota(I32, (tm, tm), 0)
    ci = lax.broadcasted_iota(I32, (tm, tm), 1)
    pos = _dot((ri > ci).astype(BF16), both) + cnt_ref[0:1, :]
    rank1 = jnp.sum(jnp.where(oh1, pos, 0.0), axis=-1, keepdims=True)
    rank2 = jnp.sum(jnp.where(oh2, pos, 0.0), axis=-1, keepdims=True)
    cnt_ref[...] = cnt_ref[...] + jnp.sum(both.astype(F32), axis=0, keepdims=True)

    out = jnp.where(lane == 0, i1, 0.0)
    out = jnp.where(lane == 1, i2, out)
    out = jnp.where(lane == 2, rank1, out)
    out = jnp.where(lane == 3, rank2, out)
    out = jnp.where(lane == 4, w1, out)
    route_ref[...] = jnp.where(lane == 5, w2, out)


def router(h, g, r, tm):
    t, d = h.shape
    kern = functools.partial(_router_kernel, tm=tm)
    return pl.pallas_call(
        kern,
        grid=(t // tm,),
        in_specs=[pl.BlockSpec((tm, d), lambda i: (i, 0)),
                  pl.BlockSpec((1, d), lambda i: (0, 0)),
                  pl.BlockSpec((d, LANES), lambda i: (0, 0))],
        out_specs=[pl.BlockSpec((tm, LANES), lambda i: (i, 0)),
                   pl.BlockSpec((SUBLANES, LANES), lambda i: (0, 0))],
        out_shape=[jax.ShapeDtypeStruct((t, LANES), F32),
                   jax.ShapeDtypeStruct((SUBLANES, LANES), F32)],
        compiler_params=_params(1),
    )(h, g.reshape(1, d), r)


def _moe_kernel(te_ref, nu_ref, code_ref, h_hbm, g_ref, wg_ref, wu_ref, wd_ref, out_hbm,
                buf, xn_ref, gsem, ssem, *, tm, n_tok, n_ff):
    i = pl.program_id(0)
    j = pl.program_id(1)
    last_step = (i == pl.num_programs(0) - 1) & (j == pl.num_programs(1) - 1)
    n_used = nu_ref[0]
    cur, nxt, prv = lax.rem(i, 3), lax.rem(i + 1, 3), lax.rem(i + 2, 3)
    rows_per_step = tm // n_ff
    pow2 = n_tok & (n_tok - 1) == 0
    res0, act0 = 3, 6

    def gather_copy(idx, s, r):
        code = code_ref[idx]
        tok = (code & (n_tok - 1)) if pow2 else lax.rem(code, jnp.int32(n_tok))
        return pltpu.make_async_copy(h_hbm.at[pl.ds(tok, 1)], buf.at[s, pl.ds(r, 1)], gsem.at[s])

    def scatter_copy(idx, s, r):
        return pltpu.make_async_copy(buf.at[res0 + s, pl.ds(r, 1)], out_hbm.at[pl.ds(code_ref[idx], 1)],
                                     ssem.at[s])

    def wait_gather(s):
        pltpu.make_async_copy(h_hbm.at[pl.ds(0, tm)], buf.at[s], gsem.at[s]).wait()

    def wait_scatter(s):
        pltpu.make_async_copy(buf.at[res0 + s], out_hbm.at[pl.ds(0, tm)], ssem.at[s]).wait()

    def issue_rows(lo, hi):
        for r in range(lo, hi):
            row = j * rows_per_step + r
            gather_copy((i + 3) * tm + row, prv, row).start()
            scatter_copy(i * tm + row, prv, row).start()

    @pl.when(j == 0)
    def _():
        @pl.when(i == 0)
        def _():
            for s in range(3):
                buf[res0 + s] = jnp.zeros((tm, buf.shape[2]), F32)

            def body(r, carry):
                gather_copy(tm + r, 0, r).start()
                gather_copy(2 * tm + r, 1, r).start()
                return carry
            lax.fori_loop(0, tm, body, 0)

        wait_gather(cur)

        @pl.when(i >= 2)
        def _():
            wait_scatter(cur)

        @pl.when(i < n_used)
        def _():
            xn_ref[...] = _rms(buf[cur], g_ref[...]).astype(BF16)

    @pl.when(i < n_used)
    def _():
        xn = xn_ref[...]
        d = buf.shape[2]
        tf = wg_ref.shape[2]
        n_chunks = tf // MXU_WIDTH
        per_slot = d // MXU_WIDTH
        for c in range(n_chunks):
            issue_rows(rows_per_step * c // n_chunks, rows_per_step * (c + 1) // n_chunks)
            cols = slice(MXU_WIDTH * c, MXU_WIDTH * (c + 1))
            a = jax.nn.silu(_dot(xn, wg_ref[0, :, cols])) * _dot(xn, wu_ref[0, :, cols])
            buf[act0 + c // per_slot, :, MXU_WIDTH * (c % per_slot):MXU_WIDTH * (c % per_slot + 1)] = a
        act = jnp.concatenate([buf[act0 + s, :, 0:min(d, tf - s * d)] for s in range(pl.cdiv(tf, d))], axis=1)
        part = _dot(act.astype(BF16), wd_ref[0])
        buf[res0 + cur] = jnp.where(j == 0, 0.0, buf[res0 + cur]) + part

    @pl.when(i >= n_used)
    def _():
        issue_rows(0, rows_per_step)

    @pl.when(last_step)
    def _():
        wait_gather(nxt)
        wait_gather(prv)
        wait_scatter(nxt)
        wait_scatter(prv)


def moe_experts(tile_expert, n_used, code, h, g, wg, wu, wd, n_tiles, tm, tf):
    t, d = h.shape
    f = wg.shape[2]
    n_ff = f // tf
    assert tm % n_ff == 0 and code.shape[0] == (n_tiles + 3) * tm

    def ff_idx(i, j, nu):
        return jnp.where(i < nu[0], j, n_ff - 1)

    kern = functools.partial(_moe_kernel, tm=tm, n_tok=t, n_ff=n_ff)
    grid_spec = pltpu.PrefetchScalarGridSpec(
        num_scalar_prefetch=3,
        grid=(n_tiles, n_ff),
        in_specs=[pl.BlockSpec(memory_space=pl.ANY),
                  pl.BlockSpec((1, d), lambda i, j, te, nu, code: (0, 0)),
                  pl.BlockSpec((1, d, tf), lambda i, j, te, nu, code: (te[i], 0, ff_idx(i, j, nu))),
                  pl.BlockSpec((1, d, tf), lambda i, j, te, nu, code: (te[i], 0, ff_idx(i, j, nu))),
                  pl.BlockSpec((1, tf, d), lambda i, j, te, nu, code: (te[i], ff_idx(i, j, nu), 0))],
        out_specs=pl.BlockSpec(memory_space=pl.ANY),
        scratch_shapes=[pltpu.VMEM((6 + pl.cdiv(tf, d), tm, d), F32), pltpu.VMEM((tm, d), BF16),
                        pltpu.SemaphoreType.DMA((3,)), pltpu.SemaphoreType.DMA((3,))],
    )
    return pl.pallas_call(
        kern,
        grid_spec=grid_spec,
        out_shape=jax.ShapeDtypeStruct((n_tiles * tm, d), F32),
        compiler_params=_params(2),
    )(tile_expert, n_used, code, h, g.reshape(1, d), wg, wu, wd)


def _combine_kernel(h_ref, route_ref, y1_ref, y2_ref, o_ref):
    route = route_ref[...]
    o_ref[...] = h_ref[...] + route[:, 4:5] * y1_ref[...] + route[:, 5:6] * y2_ref[...]


def moe_combine(h, route, y, tm):
    t, d = h.shape
    nb = t // tm
    return pl.pallas_call(
        _combine_kernel,
        grid=(nb,),
        in_specs=[pl.BlockSpec((tm, d), lambda i: (i, 0)),
                  pl.BlockSpec((tm, LANES), lambda i: (i, 0)),
                  pl.BlockSpec((tm, d), lambda i: (i, 0)),
                  pl.BlockSpec((tm, d), lambda i: (nb + i, 0))],
        out_specs=pl.BlockSpec((tm, d), lambda i: (i, 0)),
        out_shape=jax.ShapeDtypeStruct((t, d), F32),
        compiler_params=_params(1),
    )(h, route, y, y)


def _rope_tables(seq):
    half = ATT_HEAD_DIM // 2
    inv_freq = ROPE_THETA ** (-jnp.arange(half, dtype=F32) / half)
    ang = jnp.arange(seq, dtype=jnp.int32).astype(F32)[:, None] * inv_freq[None, :]
    reps = LANES // half
    cos = jnp.tile(jnp.cos(ang), (1, reps))
    sign = jnp.tile(jnp.concatenate([-jnp.ones((half,), F32), jnp.ones((half,), F32)]), LANES // ATT_HEAD_DIM)
    sin = jnp.tile(jnp.sin(ang), (1, reps)) * sign[None, :]
    return cos, sin


def _even_layer(h, bsz, seq, ln_mix, ln_ffn, w_in, conv_w, conv_b, w_a, b_a, w_i, b_i, lam,
                q_norm, k_norm, sinks, w_out, f_gate, f_up, f_down):
    lru_w = conv_w.shape[1]
    q_heads = sinks.shape[0]
    q_w = q_heads * ATT_HEAD_DIM
    kv_w = (w_in.shape[1] - 2 * lru_w - q_w) // 2
    kv_heads = kv_w // ATT_HEAD_DIM

    proj = rms_matmul(h, ln_mix, w_in.astype(BF16), tm=512, tn=w_in.shape[1])
    gates_w = jnp.concatenate([block_diag(*w_a), block_diag(*w_i)], axis=1).astype(BF16)
    y_lru = rglru(proj, conv_w, conv_b, gates_w, jnp.concatenate([b_a, b_i]), lam, bsz, seq, lru_w, tm=256)
    cos, sin = _rope_tables(seq)
    tile2 = lambda g: jnp.tile(g, LANES // ATT_HEAD_DIM).reshape(1, LANES)
    y_att = swa(proj, sinks, cos, sin, tile2(q_norm), tile2(k_norm), bsz, seq,
                q_col=2 * lru_w, k_col=2 * lru_w + q_w, v_col=2 * lru_w + q_w + kv_w,
                q_heads=q_heads, kv_heads=kv_heads)
    w_out = w_out.astype(BF16)
    h = proj_res(h, [y_lru, y_att], [w_out[:lru_w], w_out[lru_w:]], tm=512)
    return ffn(h, ln_ffn, f_gate.astype(BF16), f_up.astype(BF16), f_down.astype(BF16), tm=512, tf=1408)


def _odd_layer(h, bsz, seq, ln_mix, ln_ffn, w_in, conv_w, a_log, dt_bias, out_norm, w_out, router_w,
               m_gate, m_up, m_down):
    t, d = h.shape
    heads = a_log.shape[0]
    width = heads * GDN_HEAD_DIM
    cols = w_in.shape[1]
    pad = (-cols) % (11 * LANES)
    w_in_p = jnp.pad(w_in, ((0, 0), (0, pad))).astype(BF16)
    proj = rms_matmul(h, ln_mix, w_in_p, tm=512, tn=w_in_p.shape[1])
    lane_pad = lambda v: jnp.pad(v, (0, LANES - heads)).reshape(1, LANES)
    y = gdn(proj, conv_w, lane_pad(a_log), lane_pad(dt_bias), out_norm, bsz, seq, heads, tc=256)
    h = proj_res(h, [y], [w_out.astype(BF16)], tm=512)

    route, counts = router(h, ln_ffn, jnp.pad(router_w, ((0, 0), (0, LANES - N_EXPERTS))), tm=512)
    tm = MOE_TILE
    n_tiles = (2 * t) // tm + N_EXPERTS + 1
    counts = counts[0, :N_EXPERTS].astype(I32)
    tiles_per = (counts + tm - 1) // tm
    tile_end = jnp.cumsum(tiles_per)
    starts = (tile_end - tiles_per) * tm
    n_used = tile_end[-1:]
    tile_ids = jnp.minimum(jnp.arange(n_tiles, dtype=I32), n_used - 1)
    tile_expert = jnp.sum((tile_ids[:, None] >= tile_end[None, :]).astype(I32), axis=1)
    e1 = route[:, 0].astype(I32)
    e2 = route[:, 1].astype(I32)
    slot1 = starts[e1] + route[:, 2].astype(I32)
    slot2 = starts[e2] + route[:, 3].astype(I32)
    tok = jnp.arange(t, dtype=I32)
    idx = jnp.arange((n_tiles + 3) * tm, dtype=I32)
    slot = idx - tm
    seg = jnp.sum((slot[:, None] >= (tile_end * tm)[None, :]).astype(I32), axis=1)
    seg_c = jnp.minimum(seg, N_EXPERTS - 1)
    routed = (slot >= 0) & (seg < N_EXPERTS) & (slot - starts[seg_c] < counts[seg_c])
    scattered = idx < n_tiles * tm
    dump_id = jnp.cumsum((scattered & ~routed).astype(I32)) - 1
    code = jnp.where(scattered, 2 * t + dump_id, 0).at[jnp.concatenate([slot1, slot2]) + tm].set(
        jnp.concatenate([tok, tok + t]))

    y = moe_experts(tile_expert, n_used, code, h, ln_ffn, m_gate.astype(BF16), m_up.astype(BF16),
                    m_down.astype(BF16), n_tiles, tm, MOE_FF_CHUNK)
    return moe_combine(h, route, y, tm=512)


def kernel(x, ln_mix, ln_ffn, e_w_in, e_lru_conv_w, e_lru_conv_b, e_lru_w_a, e_lru_b_a, e_lru_w_i, e_lru_b_i, e_lru_lambda, e_q_norm, e_k_norm, e_sinks, e_w_out, e_ffn_w_gate, e_ffn_w_up, e_ffn_w_down, o_w_in, o_conv_w, o_a_log, o_dt_bias, o_out_norm, o_w_out, o_router, o_moe_w_gate, o_moe_w_up, o_moe_w_down):
    bsz, seq, d = x.shape
    h = x.reshape(bsz * seq, d)
    for layer in range(ln_mix.shape[0]):
        j = layer // 2
        if layer % 2 == 0:
            h = _even_layer(h, bsz, seq, ln_mix[layer], ln_ffn[layer], e_w_in[j], e_lru_conv_w[j],
                            e_lru_conv_b[j], e_lru_w_a[j], e_lru_b_a[j], e_lru_w_i[j], e_lru_b_i[j],
                            e_lru_lambda[j], e_q_norm[j], e_k_norm[j], e_sinks[j], e_w_out[j],
                            e_ffn_w_gate[j], e_ffn_w_up[j], e_ffn_w_down[j])
        else:
            h = _odd_layer(h, bsz, seq, ln_mix[layer], ln_ffn[layer], o_w_in[j], o_conv_w[j], o_a_log[j],
                           o_dt_bias[j], o_out_norm[j], o_w_out[j], o_router[j], o_moe_w_gate[j],
                           o_moe_w_up[j], o_moe_w_down[j])
    return h.reshape(bsz, seq, d)
```

```python
import functools

import jax
import jax.numpy as jnp
from jax import lax
from jax.scipy.linalg import block_diag
from jax.experimental import pallas as pl
from jax.experimental.pallas import tpu as pltpu

F32 = jnp.float32
BF16 = jnp.bfloat16
I32 = jnp.int32

EPS = 1e-6
LANES = 128
SUBLANES = 8
MXU_WIDTH = 256
VMEM_LIMIT = 52 * 1024 * 1024

CONV_WIDTH = 4
LRU_C = 8.0
ATT_HEAD_DIM = 64
ATT_WINDOW = 128
ROPE_THETA = 10000.0
GDN_HEAD_DIM = 128
GDN_CHUNK = 64
N_EXPERTS = 8
MOE_TILE = 512
MOE_BLOCK = 512
MOE_FF_CHUNK = 1792


def _params(n_axes, vmem=VMEM_LIMIT):
    return pltpu.CompilerParams(dimension_semantics=("arbitrary",) * n_axes, vmem_limit_bytes=vmem)


def _rms(x, g):
    return x * lax.rsqrt(jnp.mean(x * x, axis=-1, keepdims=True) + EPS) * g


def _dot(a, b):
    return jnp.dot(a, b, preferred_element_type=F32)


def _dot_nt(a, b):
    return lax.dot_general(a, b, (((1,), (1,)), ((), ())), preferred_element_type=F32)


def _split_bf16(x):
    hi = x.astype(BF16)
    lo = (x - hi.astype(F32)).astype(BF16)
    return hi, lo


def _rms_matmul_kernel(x_ref, g_ref, w_ref, o_ref, xn_ref):
    @pl.when(pl.program_id(1) == 0)
    def _():
        xn_ref[...] = _rms(x_ref[...], g_ref[...]).astype(BF16)

    o_ref[...] = _dot(xn_ref[...], w_ref[...])


def rms_matmul(x, g, w, tm, tn):
    t, d = x.shape
    n = w.shape[1]
    return pl.pallas_call(
        _rms_matmul_kernel,
        grid=(t // tm, n // tn),
        in_specs=[pl.BlockSpec((tm, d), lambda i, j: (i, 0)),
                  pl.BlockSpec((1, d), lambda i, j: (0, 0)),
                  pl.BlockSpec((d, tn), lambda i, j: (0, j))],
        out_specs=pl.BlockSpec((tm, tn), lambda i, j: (i, j)),
        out_shape=jax.ShapeDtypeStruct((t, n), F32),
        scratch_shapes=[pltpu.VMEM((tm, d), BF16)],
        compiler_params=_params(2),
    )(x, g.reshape(1, d), w)


def _causal_conv(ext_ref, x, cw, tm):
    ext_ref[SUBLANES:SUBLANES + tm, :] = x
    base = SUBLANES - (CONV_WIDTH - 1)
    acc = cw[0:1, :] * ext_ref[base:base + tm, :]
    for j in range(1, CONV_WIDTH):
        acc = acc + cw[j:j + 1, :] * ext_ref[base + j:base + j + tm, :]
    ext_ref[0:SUBLANES, :] = x[tm - SUBLANES:tm, :]
    return acc


def _rglru_kernel(xb_ref, gate_ref, cw_ref, cb_ref, wg_ref, bg_ref, lam_ref, o_ref, ext_ref, h_ref,
                  *, tm, width):
    @pl.when(pl.program_id(1) == 0)
    def _():
        ext_ref[0:SUBLANES, :] = jnp.zeros((SUBLANES, width), F32)
        h_ref[...] = jnp.zeros_like(h_ref)

    xc = _causal_conv(ext_ref, xb_ref[...], cw_ref[...], tm) + cb_ref[...]
    z = _dot(xc.astype(BF16), wg_ref[...]) + bg_ref[...]
    r = jax.nn.sigmoid(z[:, :width])
    gi = jax.nn.sigmoid(z[:, width:])
    log_a = -LRU_C * r * jax.nn.softplus(-lam_ref[...])
    a = jnp.exp(log_a)
    u = jnp.sqrt(1.0 - a * a) * (gi * xc)

    row = lax.broadcasted_iota(I32, (tm, width), 0)
    d = 1
    while d < tm:
        keep = row >= d
        u = u + jnp.where(keep, a * pltpu.roll(u, d, 0), 0.0)
        a = jnp.where(keep, a * pltpu.roll(a, d, 0), a)
        d *= 2
    h = u + a * h_ref[0:1, :]
    h_ref[...] = jnp.broadcast_to(h[tm - 1:tm, :], h_ref.shape)
    o_ref[...] = (h * jax.nn.gelu(gate_ref[...])).astype(o_ref.dtype)


def rglru(proj, cw, cb, wg, bg, lam, bsz, seq, width, tm):
    nt = seq // tm
    kern = functools.partial(_rglru_kernel, tm=tm, width=width)
    row = lambda c: pl.BlockSpec((1, c), lambda b, i: (0, 0))
    return pl.pallas_call(
        kern,
        grid=(bsz, nt),
        in_specs=[pl.BlockSpec((tm, width), lambda b, i: (b * nt + i, 0)),
                  pl.BlockSpec((tm, width), lambda b, i: (b * nt + i, 1)),
                  pl.BlockSpec((CONV_WIDTH, width), lambda b, i: (0, 0)),
                  row(width),
                  pl.BlockSpec((width, 2 * width), lambda b, i: (0, 0)),
                  row(2 * width),
                  row(width)],
        out_specs=pl.BlockSpec((tm, width), lambda b, i: (b * nt + i, 0)),
        out_shape=jax.ShapeDtypeStruct((bsz * seq, width), BF16),
        scratch_shapes=[pltpu.VMEM((tm + SUBLANES, width), F32), pltpu.VMEM((SUBLANES, width), F32)],
        compiler_params=_params(2),
    )(proj, proj, cw, cb.reshape(1, -1), wg, bg.reshape(1, -1), lam.reshape(1, -1))


def _swa_kernel(sink_ref, q_ref, k_ref, v_ref, cos_ref, sin_ref, qg_ref, kg_ref, o_ref, kall_ref, vall_ref,
                *, q_heads, kv_heads):
    w = ATT_WINDOW
    n = pl.program_id(1)
    group = q_heads // kv_heads
    assert kv_heads * ATT_HEAD_DIM == LANES and group % 2 == 0

    @pl.when(n == 0)
    def _():
        kall_ref[0:w, :] = jnp.zeros((w, LANES), BF16)
        vall_ref[0:w, :] = jnp.zeros((w, LANES), BF16)

    @pl.when(n > 0)
    def _():
        kall_ref[0:w, :] = kall_ref[w:2 * w, :]
        vall_ref[0:w, :] = vall_ref[w:2 * w, :]

    lane = lax.broadcasted_iota(I32, (w, LANES), 1)
    first_half = (lane % ATT_HEAD_DIM) < (ATT_HEAD_DIM // 2)
    lane_head = lane // ATT_HEAD_DIM
    bd_r = lax.broadcasted_iota(I32, (LANES, LANES), 0) // ATT_HEAD_DIM
    bd_c = lax.broadcasted_iota(I32, (LANES, LANES), 1) // ATT_HEAD_DIM
    head_ones = (bd_r == bd_c).astype(BF16)
    cos = cos_ref[...]
    sin = sin_ref[...]

    def norm_rope(x, g):
        hi, lo = _split_bf16(x * x)
        ss = _dot(hi, head_ones) + _dot(lo, head_ones)
        xn = x * lax.rsqrt(ss * (1.0 / ATT_HEAD_DIM) + EPS) * g
        rot = jnp.where(first_half, pltpu.roll(xn, LANES - ATT_HEAD_DIM // 2, 1),
                        pltpu.roll(xn, ATT_HEAD_DIM // 2, 1))
        return xn * cos + rot * sin

    kall_ref[w:2 * w, :] = norm_rope(k_ref[...], kg_ref[...]).astype(BF16)
    vall_ref[w:2 * w, :] = v_ref[...].astype(BF16)
    qc = [norm_rope(q_ref[:, LANES * c:LANES * (c + 1)], qg_ref[...]) for c in range(q_heads // 2)]

    rows = group * w
    qi = lax.broadcasted_iota(I32, (rows, 2 * w), 0) % w
    kj = lax.broadcasted_iota(I32, (rows, 2 * w), 1)
    valid = (kj > qi) & (kj <= qi + w) & ((n > 0) | (kj >= w))
    row_head = lax.broadcasted_iota(I32, (rows, 1), 0) // w

    outs = []
    for g in range(kv_heads):
        parts = []
        for hh in range(group):
            h = g * group + hh
            x = qc[h // 2]
            if h % 2 != g:
                x = pltpu.roll(x, ATT_HEAD_DIM, 1)
            parts.append(jnp.where(lane_head == g, x, 0.0))
        qs = jnp.concatenate(parts, axis=0).astype(BF16)
        s = _dot_nt(qs, kall_ref[...]) * (ATT_HEAD_DIM ** -0.5)
        s = jnp.where(valid, s, -1e30)
        sink = jnp.zeros((rows, 1), F32)
        for hh in range(group):
            sink = jnp.where(row_head == hh, sink_ref[g * group + hh], sink)
        m = jnp.maximum(jnp.max(s, axis=-1, keepdims=True), sink)
        p = jnp.exp(s - m)
        denom = jnp.sum(p, axis=-1, keepdims=True) + jnp.exp(sink - m)
        pv = _dot(p.astype(BF16), vall_ref[...]) / denom
        outs.extend(pv[w * hh:w * (hh + 1), :] for hh in range(group))

    for c in range(q_heads // 2):
        g = (2 * c) // group
        a, b = outs[2 * c], outs[2 * c + 1]
        if g == 1:
            a = pltpu.roll(a, ATT_HEAD_DIM, 1)
        else:
            b = pltpu.roll(b, ATT_HEAD_DIM, 1)
        o_ref[:, LANES * c:LANES * (c + 1)] = jnp.where(lane_head == 0, a, b).astype(o_ref.dtype)


def swa(proj, sinks, cos, sin, qg, kg, bsz, seq, q_col, k_col, v_col, q_heads, kv_heads):
    w = ATT_WINDOW
    nb = seq // w
    qw = q_heads * ATT_HEAD_DIM
    kern = functools.partial(_swa_kernel, q_heads=q_heads, kv_heads=kv_heads)
    return pl.pallas_call(
        kern,
        grid=(bsz, nb),
        in_specs=[pl.BlockSpec(memory_space=pltpu.SMEM),
                  pl.BlockSpec((w, qw), lambda b, i: (b * nb + i, q_col // qw)),
                  pl.BlockSpec((w, LANES), lambda b, i: (b * nb + i, k_col // LANES)),
                  pl.BlockSpec((w, LANES), lambda b, i: (b * nb + i, v_col // LANES)),
                  pl.BlockSpec((w, LANES), lambda b, i: (i, 0)),
                  pl.BlockSpec((w, LANES), lambda b, i: (i, 0)),
                  pl.BlockSpec((1, LANES), lambda b, i: (0, 0)),
                  pl.BlockSpec((1, LANES), lambda b, i: (0, 0))],
        out_specs=pl.BlockSpec((w, qw), lambda b, i: (b * nb + i, 0)),
        out_shape=jax.ShapeDtypeStruct((bsz * seq, qw), BF16),
        scratch_shapes=[pltpu.VMEM((2 * w, LANES), BF16), pltpu.VMEM((2 * w, LANES), BF16)],
        compiler_params=_params(2),
    )(sinks, proj, proj, proj, cos, sin, qg, kg)


def _proj_res_kernel(*refs, n_in):
    h_ref, o_ref = refs[0], refs[-1]
    acc = h_ref[...]
    for y_ref, w_ref in zip(refs[1:1 + n_in], refs[1 + n_in:1 + 2 * n_in]):
        acc = acc + _dot(y_ref[...], w_ref[...])
    o_ref[...] = acc


def proj_res(h, ys, ws, tm):
    t, d = h.shape
    kern = functools.partial(_proj_res_kernel, n_in=len(ys))
    return pl.pallas_call(
        kern,
        grid=(t // tm,),
        in_specs=([pl.BlockSpec((tm, d), lambda i: (i, 0))]
                  + [pl.BlockSpec((tm, y.shape[1]), lambda i: (i, 0)) for y in ys]
                  + [pl.BlockSpec(w.shape, lambda i: (0, 0)) for w in ws]),
        out_specs=pl.BlockSpec((tm, d), lambda i: (i, 0)),
        out_shape=jax.ShapeDtypeStruct((t, d), F32),
        compiler_params=_params(1),
    )(h, *ys, *ws)


def _ffn_kernel(h_ref, g_ref, wg_ref, wu_ref, wd_ref, o_ref, xn_ref):
    j = pl.program_id(1)

    @pl.when(j == 0)
    def _():
        x = h_ref[...]
        xn_ref[...] = _rms(x, g_ref[...]).astype(BF16)
        o_ref[...] = x

    xn = xn_ref[...]
    act = (jax.nn.silu(_dot(xn, wg_ref[...])) * _dot(xn, wu_ref[...])).astype(BF16)
    o_ref[...] += _dot(act, wd_ref[...])


def ffn(h, g, wg, wu, wd, tm, tf):
    t, d = h.shape
    f = wg.shape[1]
    return pl.pallas_call(
        _ffn_kernel,
        grid=(t // tm, f // tf),
        in_specs=[pl.BlockSpec((tm, d), lambda i, j: (i, 0)),
                  pl.BlockSpec((1, d), lambda i, j: (0, 0)),
                  pl.BlockSpec((d, tf), lambda i, j: (0, j)),
                  pl.BlockSpec((d, tf), lambda i, j: (0, j)),
                  pl.BlockSpec((tf, d), lambda i, j: (j, 0))],
        out_specs=pl.BlockSpec((tm, d), lambda i, j: (i, 0)),
        out_shape=jax.ShapeDtypeStruct((t, d), F32),
        scratch_shapes=[pltpu.VMEM((tm, d), BF16)],
        compiler_params=_params(2),
    )(h, g.reshape(1, d), wg, wu, wd)


def _gdn_kernel(qkv_ref, gate_ref, ab_ref, cw_ref, alog_ref, dtb_ref, onorm_ref, o_ref,
                ext_ref, q_ref, k_ref, v_ref, gc_ref, beta_ref, *state_refs, tc, heads):
    hd = GDN_HEAD_DIM
    c = GDN_CHUNK
    width = heads * hd

    @pl.when(pl.program_id(1) == 0)
    def _():
        ext_ref[0:SUBLANES, :] = jnp.zeros((SUBLANES, 3 * width), F32)
        for s_ref in state_refs:
            s_ref[...] = jnp.zeros_like(s_ref)

    qkv = jax.nn.silu(_causal_conv(ext_ref, qkv_ref[...], cw_ref[...], tc))
    for h in range(heads):
        q = qkv[:, hd * h:hd * (h + 1)]
        k = qkv[:, width + hd * h:width + hd * (h + 1)]
        q_ref[:, hd * h:hd * (h + 1)] = (q * lax.rsqrt(jnp.sum(q * q, axis=-1, keepdims=True) + EPS)
                                          * (hd ** -0.5))
        k_ref[:, hd * h:hd * (h + 1)] = k * lax.rsqrt(jnp.sum(k * k, axis=-1, keepdims=True) + EPS)
    v_ref[...] = qkv[:, 2 * width:]

    ab = ab_ref[...]
    g = -jnp.exp(alog_ref[...]) * jax.nn.softplus(ab + dtb_ref[...])
    beta_ref[...] = pltpu.roll(jax.nn.sigmoid(ab), LANES - heads, 1)
    row = lax.broadcasted_iota(I32, (tc, LANES), 0) % c
    d = 1
    while d < c:
        g = g + jnp.where(row >= d, pltpu.roll(g, d, 0), 0.0)
        d *= 2
    gc_ref[...] = g

    ri = lax.broadcasted_iota(I32, (c, c), 0)
    ci = lax.broadcasted_iota(I32, (c, c), 1)
    tri = ri >= ci
    strict = ri > ci
    eye = (ri == ci).astype(F32)

    def chunk_body(n, carry):
        r0 = pl.multiple_of(n * c, c)
        rows = pl.ds(r0, c)
        gc_all = gc_ref[rows, :]
        gr_all = gc_all.T
        beta_all = beta_ref[rows, :]
        hs = range(heads)
        cols = [slice(hd * h, hd * (h + 1)) for h in hs]
        q = [q_ref[rows, cols[h]] for h in hs]
        k = [k_ref[rows, cols[h]] for h in hs]
        v = [v_ref[rows, cols[h]] for h in hs]
        gcol = [gc_all[:, h:h + 1] for h in hs]
        grow = [gr_all[h:h + 1, :] for h in hs]
        beta = [beta_all[:, h:h + 1] for h in hs]
        qk_kk = [_dot_nt(jnp.concatenate([q[h], k[h]], axis=0).astype(BF16), k[h].astype(BF16)) for h in hs]
        decay = [jnp.where(tri, jnp.exp(jnp.where(tri, gcol[h] - grow[h], 0.0)), 0.0) for h in hs]
        m = [-jnp.where(strict, beta[h] * qk_kk[h][c:, :] * decay[h], 0.0) for h in hs]
        p = [eye + m[h] for h in hs]
        e = [_dot(m[h].astype(BF16), m[h].astype(BF16)) for h in hs]
        lvl = 2
        while 2 * lvl < c:
            r = [_dot(jnp.concatenate([p[h], e[h]], axis=0).astype(BF16), e[h].astype(BF16)) for h in hs]
            p = [p[h] + r[h][:c, :] for h in hs]
            e = [r[h][c:, :] for h in hs]
            lvl *= 2
        p = [p[h] + _dot(p[h].astype(BF16), e[h].astype(BF16)) for h in hs]
        eg = [jnp.exp(gcol[h]) for h in hs]
        uw = [_dot(p[h].astype(BF16),
                   jnp.concatenate([beta[h] * v[h], (beta[h] * eg[h]) * k[h]], axis=1).astype(BF16)) for h in hs]
        a_qk = [jnp.where(tri, qk_kk[h][:c, :] * decay[h], 0.0) for h in hs]
        g_last = [gcol[h][c - 1:c, :] for h in hs]
        s = [state_refs[h][...] for h in hs]
        ws_qs = [_dot(jnp.concatenate([uw[h][:, hd:], q[h] * eg[h]], axis=0).astype(BF16), s[h].astype(BF16))
                 for h in hs]
        vb = [(uw[h][:, :hd] - ws_qs[h][:c, :]).astype(BF16) for h in hs]
        o = [ws_qs[h][c:, :] + _dot(a_qk[h].astype(BF16), vb[h]) for h in hs]
        s_new = [s[h] * jnp.exp(g_last[h]) + _dot((k[h] * jnp.exp(g_last[h] - gcol[h])).T.astype(BF16), vb[h])
                 for h in hs]
        for h in hs:
            state_refs[h][...] = s_new[h]
            on = o[h] * lax.rsqrt(jnp.mean(o[h] * o[h], axis=-1, keepdims=True) + EPS) * onorm_ref[...]
            o_ref[rows, cols[h]] = (on * jax.nn.silu(gate_ref[rows, cols[h]])).astype(o_ref.dtype)
        return carry

    lax.fori_loop(0, tc // c, chunk_body, 0)


def gdn(proj, cw, alog, dtb, onorm, bsz, seq, heads, tc):
    nt = seq // tc
    width = heads * GDN_HEAD_DIM
    kern = functools.partial(_gdn_kernel, tc=tc, heads=heads)
    row = lambda n: pl.BlockSpec((1, n), lambda b, i: (0, 0))
    return pl.pallas_call(
        kern,
        grid=(bsz, nt),
        in_specs=[pl.BlockSpec((tc, 3 * width), lambda b, i: (b * nt + i, 0)),
                  pl.BlockSpec((tc, width), lambda b, i: (b * nt + i, 3)),
                  pl.BlockSpec((tc, LANES), lambda b, i: (b * nt + i, 4 * width // LANES)),
                  pl.BlockSpec((CONV_WIDTH, 3 * width), lambda b, i: (0, 0)),
                  row(LANES), row(LANES), row(GDN_HEAD_DIM)],
        out_specs=pl.BlockSpec((tc, width), lambda b, i: (b * nt + i, 0)),
        out_shape=jax.ShapeDtypeStruct((bsz * seq, width), BF16),
        scratch_shapes=[pltpu.VMEM((tc + SUBLANES, 3 * width), F32),
                        pltpu.VMEM((tc, width), F32), pltpu.VMEM((tc, width), F32),
                        pltpu.VMEM((tc, width), F32),
                        pltpu.VMEM((tc, LANES), F32), pltpu.VMEM((tc, LANES), F32),
                        *[pltpu.VMEM((GDN_HEAD_DIM, GDN_HEAD_DIM), F32) for _ in range(heads)]],
        compiler_params=_params(2),
    )(proj, proj, proj, cw, alog, dtb, onorm.reshape(1, -1))


def _router_kernel(h_ref, g_ref, r_ref, route_ref, cnt_ref, *, tm):
    t_hi, t_lo = _split_bf16(_rms(h_ref[...], g_ref[...]))
    r_hi, r_lo = _split_bf16(r_ref[...])
    logits = _dot(t_hi, r_hi) + (_dot(t_lo, r_hi) + _dot(t_hi, r_lo))
    lane = lax.broadcasted_iota(I32, (tm, LANES), 1)
    lane_f = lane.astype(F32)
    neg = -jnp.inf
    lg = jnp.where(lane < N_EXPERTS, logits, neg)
    m1 = jnp.max(lg, axis=-1, keepdims=True)
    i1 = jnp.min(jnp.where(lg == m1, lane_f, float(LANES)), axis=-1, keepdims=True)
    oh1 = lane_f == i1
    lg2 = jnp.where(oh1, neg, lg)
    m2 = jnp.max(lg2, axis=-1, keepdims=True)
    i2 = jnp.min(jnp.where(lg2 == m2, lane_f, float(LANES)), axis=-1, keepdims=True)
    oh2 = lane_f == i2
    e2 = jnp.exp(m2 - m1)
    w1 = 1.0 / (1.0 + e2)
    w2 = e2 / (1.0 + e2)

    both = (oh1 | oh2).astype(BF16)
    ri = lax.broadcasted_iota(I32, (tm, tm), 0)
    ci = lax.broadcasted_iota(I32, (tm, tm), 1)
    pos = _dot((ri > ci).astype(BF16), both)
    rank1 = jnp.sum(jnp.where(oh1, pos, 0.0), axis=-1, keepdims=True)
    rank2 = jnp.sum(jnp.where(oh2, pos, 0.0), axis=-1, keepdims=True)
    cnt_ref[0] = jnp.broadcast_to(jnp.sum(both.astype(F32), axis=0, keepdims=True), (SUBLANES, LANES))

    out = jnp.where(lane == 0, i1, 0.0)
    out = jnp.where(lane == 1, i2, out)
    out = jnp.where(lane == 2, rank1, out)
    out = jnp.where(lane == 3, rank2, out)
    out = jnp.where(lane == 4, w1, out)
    route_ref[...] = jnp.where(lane == 5, w2, out)


def router(h, g, r, tm):
    t, d = h.shape
    kern = functools.partial(_router_kernel, tm=tm)
    return pl.pallas_call(
        kern,
        grid=(t // tm,),
        in_specs=[pl.BlockSpec((tm, d), lambda i: (i, 0)),
                  pl.BlockSpec((1, d), lambda i: (0, 0)),
                  pl.BlockSpec((d, LANES), lambda i: (0, 0))],
        out_specs=[pl.BlockSpec((tm, LANES), lambda i: (i, 0)),
                   pl.BlockSpec((1, SUBLANES, LANES), lambda i: (i, 0, 0))],
        out_shape=[jax.ShapeDtypeStruct((t, LANES), F32),
                   jax.ShapeDtypeStruct((t // tm, SUBLANES, LANES), F32)],
        compiler_params=_params(1),
    )(h, g.reshape(1, d), r)


def _local_positions(route, off_row):
    lane_f = lax.broadcasted_iota(I32, route.shape, 1).astype(F32)
    lpos1 = jnp.sum(jnp.where(lane_f == route[:, 0:1], off_row, 0.0), axis=-1, keepdims=True) + route[:, 2:3]
    lpos2 = jnp.sum(jnp.where(lane_f == route[:, 1:2], off_row, 0.0), axis=-1, keepdims=True) + route[:, 3:4]
    return lpos1, lpos2


def _run_copies(n_rows, max_bit, make):
    out = []
    for k in range(max_bit, -1, -1):
        bit = k + 3
        offset = pl.multiple_of((n_rows >> (bit + 1)) << (bit + 1), SUBLANES)
        out.append((((n_rows >> bit) & 1) == 1, make(offset, SUBLANES << k)))
    return out


def _start_all(copies):
    for pred, cp in copies:
        pl.when(pred)(cp.start)


def _wait_all(copies):
    for pred, cp in copies:
        pl.when(pred)(cp.wait)


def _dispatch_kernel(seg_ref, cnt_ref, off_ref, tail_ref, h_ref, g_ref, route_ref, offrow_ref, x_hbm,
                     local, zeros, sem, zsem, *, tm, max_bit, tile_rows):
    b = pl.program_id(0)
    nb = pl.num_programs(0)
    slot = b % 2

    def block_copies(blk, s):
        out = []
        for e in range(N_EXPERTS):
            seg, off = seg_ref[blk * N_EXPERTS + e], off_ref[blk * N_EXPERTS + e]
            out += _run_copies(cnt_ref[blk * N_EXPERTS + e], max_bit, lambda o, n, seg=seg, off=off:
                               pltpu.make_async_copy(local.at[s, pl.ds(pl.multiple_of(off + o, SUBLANES), n)],
                                                     x_hbm.at[pl.ds(pl.multiple_of(seg + o, SUBLANES), n)],
                                                     sem.at[s]))
        return out

    def tail_copies():
        out = []
        for e in range(N_EXPERTS):
            start = tail_ref[e]
            out += _run_copies(tail_ref[N_EXPERTS + e], max_bit - 1, lambda o, n, start=start:
                               pltpu.make_async_copy(zeros.at[pl.ds(0, n)],
                                                     x_hbm.at[pl.ds(pl.multiple_of(start + o, SUBLANES), n)], zsem))
        return out

    @pl.when(b >= 1)
    def _():
        _wait_all(block_copies(b - 1, 1 - slot))

    t = _rms(h_ref[...], g_ref[...]).astype(BF16)
    lpos1, lpos2 = _local_positions(route_ref[...], offrow_ref[0])
    rows_f = lax.broadcasted_iota(I32, (tm, local.shape[1]), 1).astype(F32)
    place = ((rows_f == lpos1) | (rows_f == lpos2)).astype(BF16)
    local[slot] = lax.dot_general(place, t, (((0,), (0,)), ((), ())), preferred_element_type=F32)
    _start_all(block_copies(b, slot))

    @pl.when(b == nb - 1)
    def _():
        zeros[...] = jnp.zeros_like(zeros)
        _start_all(tail_copies())
        _wait_all(block_copies(b, slot))
        _wait_all(tail_copies())

        def fill_idle_tile(tile, carry):
            cp = pltpu.make_async_copy(zeros, x_hbm.at[pl.ds(pl.multiple_of(tile * tile_rows, tile_rows), tile_rows)],
                                       zsem)
            cp.start()
            cp.wait()
            return carry
        lax.fori_loop(tail_ref[2 * N_EXPERTS], x_hbm.shape[0] // tile_rows, fill_idle_tile, 0)


def moe_dispatch(seg, cnt8, off, tail, h, g, route, off_rows, n_rows, tm, local_rows, tile_rows):
    t, d = h.shape
    max_bit = (tm // SUBLANES).bit_length() - 1
    assert tile_rows >= tm // 2 and n_rows % tile_rows == 0
    kern = functools.partial(_dispatch_kernel, tm=tm, max_bit=max_bit, tile_rows=tile_rows)
    grid_spec = pltpu.PrefetchScalarGridSpec(
        num_scalar_prefetch=4,
        grid=(t // tm,),
        in_specs=[pl.BlockSpec((tm, d), lambda i, *_: (i, 0)),
                  pl.BlockSpec((1, d), lambda i, *_: (0, 0)),
                  pl.BlockSpec((tm, LANES), lambda i, *_: (i, 0)),
                  pl.BlockSpec((1, 1, LANES), lambda i, *_: (i, 0, 0))],
        out_specs=pl.BlockSpec(memory_space=pl.ANY),
        scratch_shapes=[pltpu.VMEM((2, local_rows, d), F32), pltpu.VMEM((tile_rows, d), F32),
                        pltpu.SemaphoreType.DMA((2,)), pltpu.SemaphoreType.DMA(())],
    )
    return pl.pallas_call(
        kern,
        grid_spec=grid_spec,
        out_shape=jax.ShapeDtypeStruct((n_rows, d), F32),
        compiler_params=_params(1),
    )(seg, cnt8, off, tail, h, g.reshape(1, d), route, off_rows)


def _moe_kernel(te_ref, nu_ref, x_ref, wg_ref, wu_ref, wd_ref, y_ref, xn_ref):
    i = pl.program_id(0)
    j = pl.program_id(1)
    n_used = nu_ref[0]

    @pl.when(i < n_used)
    def _():
        @pl.when(j == 0)
        def _():
            xn_ref[...] = x_ref[...].astype(BF16)

        xn = xn_ref[...]
        act = (jax.nn.silu(_dot(xn, wg_ref[0])) * _dot(xn, wu_ref[0])).astype(BF16)
        part = _dot(act, wd_ref[0])

        @pl.when(j == 0)
        def _():
            y_ref[...] = part

        @pl.when(j > 0)
        def _():
            y_ref[...] += part

    @pl.when((i >= n_used) & (j == 0))
    def _():
        y_ref[...] = jnp.zeros_like(y_ref)


def moe_experts(tile_expert, n_used, x, wg, wu, wd, tm, tf):
    n_rows, d = x.shape
    f = wg.shape[2]
    n_ff = f // tf

    def ff_idx(i, j, nu):
        return jnp.where(i < nu[0], j, n_ff - 1)

    grid_spec = pltpu.PrefetchScalarGridSpec(
        num_scalar_prefetch=2,
        grid=(n_rows // tm, n_ff),
        in_specs=[pl.BlockSpec((tm, d), lambda i, j, te, nu: (jnp.minimum(i, nu[0] - 1), 0)),
                  pl.BlockSpec((1, d, tf), lambda i, j, te, nu: (te[i], 0, ff_idx(i, j, nu))),
                  pl.BlockSpec((1, d, tf), lambda i, j, te, nu: (te[i], 0, ff_idx(i, j, nu))),
                  pl.BlockSpec((1, tf, d), lambda i, j, te, nu: (te[i], ff_idx(i, j, nu), 0))],
        out_specs=pl.BlockSpec((tm, d), lambda i, j, te, nu: (i, 0)),
        scratch_shapes=[pltpu.VMEM((tm, d), BF16)],
    )
    return pl.pallas_call(
        _moe_kernel,
        grid_spec=grid_spec,
        out_shape=jax.ShapeDtypeStruct((n_rows, d), F32),
        compiler_params=_params(2),
    )(tile_expert, n_used, x, wg, wu, wd)


def _combine_kernel(seg_ref, cnt_ref, off_ref, h_ref, route_ref, offrow_ref, y_hbm, o_ref, local, sem,
                    *, tm, max_bit):
    b = pl.program_id(0)
    nb = pl.num_programs(0)
    slot = b % 2

    def block_copies(blk, s):
        out = []
        for e in range(N_EXPERTS):
            seg, off = seg_ref[blk * N_EXPERTS + e], off_ref[blk * N_EXPERTS + e]
            out += _run_copies(cnt_ref[blk * N_EXPERTS + e], max_bit, lambda o, n, seg=seg, off=off:
                               pltpu.make_async_copy(y_hbm.at[pl.ds(pl.multiple_of(seg + o, SUBLANES), n)],
                                                     local.at[s, pl.ds(pl.multiple_of(off + o, SUBLANES), n)],
                                                     sem.at[s]))
        return out

    @pl.when(b == 0)
    def _():
        local[...] = jnp.zeros_like(local)
        _start_all(block_copies(0, 0))

    _wait_all(block_copies(b, slot))

    @pl.when(b + 1 < nb)
    def _():
        _start_all(block_copies(b + 1, 1 - slot))

    route = route_ref[...]
    lpos1, lpos2 = _local_positions(route, offrow_ref[0])
    rows_f = lax.broadcasted_iota(I32, (tm, local.shape[1]), 1).astype(F32)
    gate = jnp.where(rows_f == lpos1, route[:, 4:5], 0.0) + jnp.where(rows_f == lpos2, route[:, 5:6], 0.0)
    g_hi, g_lo = _split_bf16(gate)
    y_hi, y_lo = _split_bf16(local[slot])
    o_ref[...] = h_ref[...] + (_dot(g_hi, y_hi) + (_dot(g_lo, y_hi) + _dot(g_hi, y_lo)))


def moe_combine(seg, cnt8, off, h, route, off_rows, y, tm, local_rows):
    t, d = h.shape
    max_bit = (tm // SUBLANES).bit_length() - 1
    kern = functools.partial(_combine_kernel, tm=tm, max_bit=max_bit)
    grid_spec = pltpu.PrefetchScalarGridSpec(
        num_scalar_prefetch=3,
        grid=(t // tm,),
        in_specs=[pl.BlockSpec((tm, d), lambda i, *_: (i, 0)),
                  pl.BlockSpec((tm, LANES), lambda i, *_: (i, 0)),
                  pl.BlockSpec((1, 1, LANES), lambda i, *_: (i, 0, 0)),
                  pl.BlockSpec(memory_space=pl.ANY)],
        out_specs=pl.BlockSpec((tm, d), lambda i, *_: (i, 0)),
        scratch_shapes=[pltpu.VMEM((2, local_rows, d), F32), pltpu.SemaphoreType.DMA((2,))],
    )
    return pl.pallas_call(
        kern,
        grid_spec=grid_spec,
        out_shape=jax.ShapeDtypeStruct((t, d), F32),
        compiler_params=_params(1),
    )(seg, cnt8, off, h, route, off_rows, y)


def _rope_tables(seq):
    half = ATT_HEAD_DIM // 2
    inv_freq = ROPE_THETA ** (-jnp.arange(half, dtype=F32) / half)
    ang = jnp.arange(seq, dtype=jnp.int32).astype(F32)[:, None] * inv_freq[None, :]
    reps = LANES // half
    cos = jnp.tile(jnp.cos(ang), (1, reps))
    sign = jnp.tile(jnp.concatenate([-jnp.ones((half,), F32), jnp.ones((half,), F32)]), LANES // ATT_HEAD_DIM)
    sin = jnp.tile(jnp.sin(ang), (1, reps)) * sign[None, :]
    return cos, sin


def _even_layer(h, bsz, seq, ln_mix, ln_ffn, w_in, conv_w, conv_b, w_a, b_a, w_i, b_i, lam,
                q_norm, k_norm, sinks, w_out, f_gate, f_up, f_down):
    lru_w = conv_w.shape[1]
    q_heads = sinks.shape[0]
    q_w = q_heads * ATT_HEAD_DIM
    kv_w = (w_in.shape[1] - 2 * lru_w - q_w) // 2
    kv_heads = kv_w // ATT_HEAD_DIM

    proj = rms_matmul(h, ln_mix, w_in.astype(BF16), tm=512, tn=w_in.shape[1])
    gates_w = jnp.concatenate([block_diag(*w_a), block_diag(*w_i)], axis=1).astype(BF16)
    y_lru = rglru(proj, conv_w, conv_b, gates_w, jnp.concatenate([b_a, b_i]), lam, bsz, seq, lru_w, tm=256)
    cos, sin = _rope_tables(seq)
    tile2 = lambda g: jnp.tile(g, LANES // ATT_HEAD_DIM).reshape(1, LANES)
    y_att = swa(proj, sinks, cos, sin, tile2(q_norm), tile2(k_norm), bsz, seq,
                q_col=2 * lru_w, k_col=2 * lru_w + q_w, v_col=2 * lru_w + q_w + kv_w,
                q_heads=q_heads, kv_heads=kv_heads)
    w_out = w_out.astype(BF16)
    h = proj_res(h, [y_lru, y_att], [w_out[:lru_w], w_out[lru_w:]], tm=512)
    return ffn(h, ln_ffn, f_gate.astype(BF16), f_up.astype(BF16), f_down.astype(BF16), tm=512, tf=1408)


def _odd_layer(h, bsz, seq, ln_mix, ln_ffn, w_in, conv_w, a_log, dt_bias, out_norm, w_out, router_w,
               m_gate, m_up, m_down):
    t, d = h.shape
    heads = a_log.shape[0]
    width = heads * GDN_HEAD_DIM
    cols = w_in.shape[1]
    pad = (-cols) % (11 * LANES)
    w_in_p = jnp.pad(w_in, ((0, 0), (0, pad))).astype(BF16)
    proj = rms_matmul(h, ln_mix, w_in_p, tm=512, tn=w_in_p.shape[1])
    lane_pad = lambda v: jnp.pad(v, (0, LANES - heads)).reshape(1, LANES)
    y = gdn(proj, conv_w, lane_pad(a_log), lane_pad(dt_bias), out_norm, bsz, seq, heads, tc=256)
    h = proj_res(h, [y], [w_out.astype(BF16)], tm=512)

    tb = MOE_BLOCK
    tm = MOE_TILE
    nblk = t // tb
    route, cnt = router(h, ln_ffn, jnp.pad(router_w, ((0, 0), (0, LANES - N_EXPERTS))), tm=tb)
    cnt8 = (cnt[:, 0, :N_EXPERTS].astype(I32) + SUBLANES - 1) // SUBLANES * SUBLANES
    total8 = jnp.sum(cnt8, axis=0)
    tiles_per = (total8 + tm - 1) // tm
    tile_end = jnp.cumsum(tiles_per)
    starts = (tile_end - tiles_per) * tm
    n_used = tile_end[-1:]
    n_tiles = (2 * t + nblk * N_EXPERTS * (SUBLANES - 1)) // tm + N_EXPERTS
    tile_ids = jnp.minimum(jnp.arange(n_tiles, dtype=I32), n_used - 1)
    tile_expert = jnp.sum((tile_ids[:, None] >= tile_end[None, :]).astype(I32), axis=1)
    seg = (starts[None, :] + jnp.cumsum(cnt8, axis=0) - cnt8).reshape(-1)
    off = jnp.cumsum(cnt8, axis=1) - cnt8
    off_rows = jnp.pad(off.astype(F32), ((0, 0), (0, LANES - N_EXPERTS))).reshape(nblk, 1, LANES)
    tail = jnp.concatenate([starts + total8, tiles_per * tm - total8, n_used])
    local_rows = -(-(2 * tb + N_EXPERTS * (SUBLANES - 1)) // LANES) * LANES
    cnt8, off = cnt8.reshape(-1), off.reshape(-1)

    x = moe_dispatch(seg, cnt8, off, tail, h, ln_ffn, route, off_rows, n_tiles * tm, tb, local_rows, tm)
    y = moe_experts(tile_expert, n_used, x, m_gate.astype(BF16), m_up.astype(BF16), m_down.astype(BF16),
                    tm, MOE_FF_CHUNK)
    return moe_combine(seg, cnt8, off, h, route, off_rows, y, tb, local_rows)


def kernel(x, ln_mix, ln_ffn, e_w_in, e_lru_conv_w, e_lru_conv_b, e_lru_w_a, e_lru_b_a, e_lru_w_i, e_lru_b_i, e_lru_lambda, e_q_norm, e_k_norm, e_sinks, e_w_out, e_ffn_w_gate, e_ffn_w_up, e_ffn_w_down, o_w_in, o_conv_w, o_a_log, o_dt_bias, o_out_norm, o_w_out, o_router, o_moe_w_gate, o_moe_w_up, o_moe_w_down):
    bsz, seq, d = x.shape
    h = x.reshape(bsz * seq, d)
    for layer in range(ln_mix.shape[0]):
        j = layer // 2
        if layer % 2 == 0:
            h = _even_layer(h, bsz, seq, ln_mix[layer], ln_ffn[layer], e_w_in[j], e_lru_conv_w[j],
                            e_lru_conv_b[j], e_lru_w_a[j], e_lru_b_a[j], e_lru_w_i[j], e_lru_b_i[j],
                            e_lru_lambda[j], e_q_norm[j], e_k_norm[j], e_sinks[j], e_w_out[j],
                            e_ffn_w_gate[j], e_ffn_w_up[j], e_ffn_w_down[j])
        else:
            h = _odd_layer(h, bsz, seq, ln_mix[layer], ln_ffn[layer], o_w_in[j], o_conv_w[j], o_a_log[j],
                           o_dt_bias[j], o_out_norm[j], o_w_out[j], o_router[j], o_moe_w_gate[j],
                           o_moe_w_up[j], o_moe_w_down[j])
    return h.reshape(bsz, seq, d)
```

```python
import functools

import jax
import jax.numpy as jnp
from jax import lax
from jax.scipy.linalg import block_diag
from jax.experimental import pallas as pl
from jax.experimental.pallas import tpu as pltpu

F32 = jnp.float32
BF16 = jnp.bfloat16
I32 = jnp.int32

EPS = 1e-6
LANES = 128
SUBLANES = 8
MXU_WIDTH = 256
VMEM_LIMIT = 52 * 1024 * 1024

CONV_WIDTH = 4
LRU_C = 8.0
ATT_HEAD_DIM = 64
ATT_WINDOW = 128
ROPE_THETA = 10000.0
GDN_HEAD_DIM = 128
GDN_CHUNK = 64
GDN_PREP_CHUNKS = 2
N_EXPERTS = 8
MOE_TILE = 512
MOE_BLOCK = 512
MOE_FF_CHUNK = 1792


def _params(n_axes, vmem=VMEM_LIMIT):
    return pltpu.CompilerParams(dimension_semantics=("arbitrary",) * n_axes, vmem_limit_bytes=vmem)


def _rms(x, g):
    return x * lax.rsqrt(jnp.mean(x * x, axis=-1, keepdims=True) + EPS) * g


def _dot(a, b):
    return jnp.dot(a, b, preferred_element_type=F32)


def _dot_nt(a, b):
    return lax.dot_general(a, b, (((1,), (1,)), ((), ())), preferred_element_type=F32)


def _split_bf16(x):
    hi = x.astype(BF16)
    lo = (x - hi.astype(F32)).astype(BF16)
    return hi, lo


def _rms_matmul_kernel(x_ref, g_ref, w_ref, o_ref, xn_ref):
    @pl.when(pl.program_id(1) == 0)
    def _():
        xn_ref[...] = _rms(x_ref[...], g_ref[...]).astype(BF16)

    o_ref[...] = _dot(xn_ref[...], w_ref[...])


def rms_matmul(x, g, w, tm, tn):
    t, d = x.shape
    n = w.shape[1]
    return pl.pallas_call(
        _rms_matmul_kernel,
        grid=(t // tm, n // tn),
        in_specs=[pl.BlockSpec((tm, d), lambda i, j: (i, 0)),
                  pl.BlockSpec((1, d), lambda i, j: (0, 0)),
                  pl.BlockSpec((d, tn), lambda i, j: (0, j))],
        out_specs=pl.BlockSpec((tm, tn), lambda i, j: (i, j)),
        out_shape=jax.ShapeDtypeStruct((t, n), F32),
        scratch_shapes=[pltpu.VMEM((tm, d), BF16)],
        compiler_params=_params(2),
    )(x, g.reshape(1, d), w)


def _causal_conv(ext_ref, x, cw, tm):
    ext_ref[SUBLANES:SUBLANES + tm, :] = x
    base = SUBLANES - (CONV_WIDTH - 1)
    acc = cw[0:1, :] * ext_ref[base:base + tm, :]
    for j in range(1, CONV_WIDTH):
        acc = acc + cw[j:j + 1, :] * ext_ref[base + j:base + j + tm, :]
    ext_ref[0:SUBLANES, :] = x[tm - SUBLANES:tm, :]
    return acc


def _rglru_kernel(xb_ref, gate_ref, cw_ref, cb_ref, wg_ref, bg_ref, lam_ref, o_ref, ext_ref, h_ref,
                  *, tm, width):
    @pl.when(pl.program_id(1) == 0)
    def _():
        ext_ref[0:SUBLANES, :] = jnp.zeros((SUBLANES, width), F32)
        h_ref[...] = jnp.zeros_like(h_ref)

    xc = _causal_conv(ext_ref, xb_ref[...], cw_ref[...], tm) + cb_ref[...]
    z = _dot(xc.astype(BF16), wg_ref[...]) + bg_ref[...]
    r = jax.nn.sigmoid(z[:, :width])
    gi = jax.nn.sigmoid(z[:, width:])
    log_a = -LRU_C * r * jax.nn.softplus(-lam_ref[...])
    a = jnp.exp(log_a)
    u = jnp.sqrt(1.0 - a * a) * (gi * xc)

    row = lax.broadcasted_iota(I32, (tm, width), 0)
    d = 1
    while d < tm:
        keep = row >= d
        u = u + jnp.where(keep, a * pltpu.roll(u, d, 0), 0.0)
        a = jnp.where(keep, a * pltpu.roll(a, d, 0), a)
        d *= 2
    h = u + a * h_ref[0:1, :]
    h_ref[...] = jnp.broadcast_to(h[tm - 1:tm, :], h_ref.shape)
    o_ref[...] = (h * jax.nn.gelu(gate_ref[...])).astype(o_ref.dtype)


def rglru(proj, cw, cb, wg, bg, lam, bsz, seq, width, tm):
    nt = seq // tm
    kern = functools.partial(_rglru_kernel, tm=tm, width=width)
    row = lambda c: pl.BlockSpec((1, c), lambda b, i: (0, 0))
    return pl.pallas_call(
        kern,
        grid=(bsz, nt),
        in_specs=[pl.BlockSpec((tm, width), lambda b, i: (b * nt + i, 0)),
                  pl.BlockSpec((tm, width), lambda b, i: (b * nt + i, 1)),
                  pl.BlockSpec((CONV_WIDTH, width), lambda b, i: (0, 0)),
                  row(width),
                  pl.BlockSpec((width, 2 * width), lambda b, i: (0, 0)),
                  row(2 * width),
                  row(width)],
        out_specs=pl.BlockSpec((tm, width), lambda b, i: (b * nt + i, 0)),
        out_shape=jax.ShapeDtypeStruct((bsz * seq, width), BF16),
        scratch_shapes=[pltpu.VMEM((tm + SUBLANES, width), F32), pltpu.VMEM((SUBLANES, width), F32)],
        compiler_params=_params(2),
    )(proj, proj, cw, cb.reshape(1, -1), wg, bg.reshape(1, -1), lam.reshape(1, -1))


def _swa_kernel(sink_ref, q_ref, k_ref, v_ref, cos_ref, sin_ref, qg_ref, kg_ref, o_ref, kall_ref, vall_ref,
                *, q_heads, kv_heads):
    w = ATT_WINDOW
    n = pl.program_id(1)
    group = q_heads // kv_heads
    assert kv_heads * ATT_HEAD_DIM == LANES and group % 2 == 0

    @pl.when(n == 0)
    def _():
        kall_ref[0:w, :] = jnp.zeros((w, LANES), BF16)
        vall_ref[0:w, :] = jnp.zeros((w, LANES), BF16)

    @pl.when(n > 0)
    def _():
        kall_ref[0:w, :] = kall_ref[w:2 * w, :]
        vall_ref[0:w, :] = vall_ref[w:2 * w, :]

    lane = lax.broadcasted_iota(I32, (w, LANES), 1)
    first_half = (lane % ATT_HEAD_DIM) < (ATT_HEAD_DIM // 2)
    lane_head = lane // ATT_HEAD_DIM
    bd_r = lax.broadcasted_iota(I32, (LANES, LANES), 0) // ATT_HEAD_DIM
    bd_c = lax.broadcasted_iota(I32, (LANES, LANES), 1) // ATT_HEAD_DIM
    head_ones = (bd_r == bd_c).astype(BF16)
    cos = cos_ref[...]
    sin = sin_ref[...]

    def norm_rope(x, g):
        hi, lo = _split_bf16(x * x)
        ss = _dot(hi, head_ones) + _dot(lo, head_ones)
        xn = x * lax.rsqrt(ss * (1.0 / ATT_HEAD_DIM) + EPS) * g
        rot = jnp.where(first_half, pltpu.roll(xn, LANES - ATT_HEAD_DIM // 2, 1),
                        pltpu.roll(xn, ATT_HEAD_DIM // 2, 1))
        return xn * cos + rot * sin

    kall_ref[w:2 * w, :] = norm_rope(k_ref[...], kg_ref[...]).astype(BF16)
    vall_ref[w:2 * w, :] = v_ref[...].astype(BF16)
    qc = [norm_rope(q_ref[:, LANES * c:LANES * (c + 1)], qg_ref[...]) for c in range(q_heads // 2)]

    rows = group * w
    qi = lax.broadcasted_iota(I32, (rows, 2 * w), 0) % w
    kj = lax.broadcasted_iota(I32, (rows, 2 * w), 1)
    valid = (kj > qi) & (kj <= qi + w) & ((n > 0) | (kj >= w))
    row_head = lax.broadcasted_iota(I32, (rows, 1), 0) // w

    outs = []
    for g in range(kv_heads):
        parts = []
        for hh in range(group):
            h = g * group + hh
            x = qc[h // 2]
            if h % 2 != g:
                x = pltpu.roll(x, ATT_HEAD_DIM, 1)
            parts.append(jnp.where(lane_head == g, x, 0.0))
        qs = jnp.concatenate(parts, axis=0).astype(BF16)
        s = _dot_nt(qs, kall_ref[...]) * (ATT_HEAD_DIM ** -0.5)
        s = jnp.where(valid, s, -1e30)
        sink = jnp.zeros((rows, 1), F32)
        for hh in range(group):
            sink = jnp.where(row_head == hh, sink_ref[g * group + hh], sink)
        m = jnp.maximum(jnp.max(s, axis=-1, keepdims=True), sink)
        p = jnp.exp(s - m)
        denom = jnp.sum(p, axis=-1, keepdims=True) + jnp.exp(sink - m)
        pv = _dot(p.astype(BF16), vall_ref[...]) / denom
        outs.extend(pv[w * hh:w * (hh + 1), :] for hh in range(group))

    for c in range(q_heads // 2):
        g = (2 * c) // group
        a, b = outs[2 * c], outs[2 * c + 1]
        if g == 1:
            a = pltpu.roll(a, ATT_HEAD_DIM, 1)
        else:
            b = pltpu.roll(b, ATT_HEAD_DIM, 1)
        o_ref[:, LANES * c:LANES * (c + 1)] = jnp.where(lane_head == 0, a, b).astype(o_ref.dtype)


def swa(proj, sinks, cos, sin, qg, kg, bsz, seq, q_col, k_col, v_col, q_heads, kv_heads):
    w = ATT_WINDOW
    nb = seq // w
    qw = q_heads * ATT_HEAD_DIM
    kern = functools.partial(_swa_kernel, q_heads=q_heads, kv_heads=kv_heads)
    return pl.pallas_call(
        kern,
        grid=(bsz, nb),
        in_specs=[pl.BlockSpec(memory_space=pltpu.SMEM),
                  pl.BlockSpec((w, qw), lambda b, i: (b * nb + i, q_col // qw)),
                  pl.BlockSpec((w, LANES), lambda b, i: (b * nb + i, k_col // LANES)),
                  pl.BlockSpec((w, LANES), lambda b, i: (b * nb + i, v_col // LANES)),
                  pl.BlockSpec((w, LANES), lambda b, i: (i, 0)),
                  pl.BlockSpec((w, LANES), lambda b, i: (i, 0)),
                  pl.BlockSpec((1, LANES), lambda b, i: (0, 0)),
                  pl.BlockSpec((1, LANES), lambda b, i: (0, 0))],
        out_specs=pl.BlockSpec((w, qw), lambda b, i: (b * nb + i, 0)),
        out_shape=jax.ShapeDtypeStruct((bsz * seq, qw), BF16),
        scratch_shapes=[pltpu.VMEM((2 * w, LANES), BF16), pltpu.VMEM((2 * w, LANES), BF16)],
        compiler_params=_params(2),
    )(sinks, proj, proj, proj, cos, sin, qg, kg)


def _proj_res_kernel(*refs, n_in):
    h_ref, o_ref = refs[0], refs[-1]
    acc = h_ref[...]
    for y_ref, w_ref in zip(refs[1:1 + n_in], refs[1 + n_in:1 + 2 * n_in]):
        acc = acc + _dot(y_ref[...], w_ref[...])
    o_ref[...] = acc


def proj_res(h, ys, ws, tm):
    t, d = h.shape
    kern = functools.partial(_proj_res_kernel, n_in=len(ys))
    return pl.pallas_call(
        kern,
        grid=(t // tm,),
        in_specs=([pl.BlockSpec((tm, d), lambda i: (i, 0))]
                  + [pl.BlockSpec((tm, y.shape[1]), lambda i: (i, 0)) for y in ys]
                  + [pl.BlockSpec(w.shape, lambda i: (0, 0)) for w in ws]),
        out_specs=pl.BlockSpec((tm, d), lambda i: (i, 0)),
        out_shape=jax.ShapeDtypeStruct((t, d), F32),
        compiler_params=_params(1),
    )(h, *ys, *ws)


def _ffn_kernel(h_ref, g_ref, wg_ref, wu_ref, wd_ref, o_ref, xn_ref):
    j = pl.program_id(1)

    @pl.when(j == 0)
    def _():
        x = h_ref[...]
        xn_ref[...] = _rms(x, g_ref[...]).astype(BF16)
        o_ref[...] = x

    xn = xn_ref[...]
    act = (jax.nn.silu(_dot(xn, wg_ref[...])) * _dot(xn, wu_ref[...])).astype(BF16)
    o_ref[...] += _dot(act, wd_ref[...])


def ffn(h, g, wg, wu, wd, tm, tf):
    t, d = h.shape
    f = wg.shape[1]
    return pl.pallas_call(
        _ffn_kernel,
        grid=(t // tm, f // tf),
        in_specs=[pl.BlockSpec((tm, d), lambda i, j: (i, 0)),
                  pl.BlockSpec((1, d), lambda i, j: (0, 0)),
                  pl.BlockSpec((d, tf), lambda i, j: (0, j)),
                  pl.BlockSpec((d, tf), lambda i, j: (0, j)),
                  pl.BlockSpec((tf, d), lambda i, j: (j, 0))],
        out_specs=pl.BlockSpec((tm, d), lambda i, j: (i, 0)),
        out_shape=jax.ShapeDtypeStruct((t, d), F32),
        scratch_shapes=[pltpu.VMEM((tm, d), BF16)],
        compiler_params=_params(2),
    )(h, g.reshape(1, d), wg, wu, wd)


def _gdn_kernel(qkv_ref, gate_ref, ab_ref, cw_ref, alog_ref, dtb_ref, onorm_ref, o_ref,
                ext_ref, q_ref, k_ref, v_ref, gc_ref, beta_ref, uw_ref, aqk_ref, qd_ref, kd_ref, *state_refs,
                tc, heads):
    hd = GDN_HEAD_DIM
    c = GDN_CHUNK
    width = heads * hd

    @pl.when(pl.program_id(1) == 0)
    def _():
        ext_ref[0:SUBLANES, :] = jnp.zeros((SUBLANES, 3 * width), F32)
        for s_ref in state_refs:
            s_ref[...] = jnp.zeros_like(s_ref)

    qkv = jax.nn.silu(_causal_conv(ext_ref, qkv_ref[...], cw_ref[...], tc))
    for h in range(heads):
        q = qkv[:, hd * h:hd * (h + 1)]
        k = qkv[:, width + hd * h:width + hd * (h + 1)]
        q_ref[:, hd * h:hd * (h + 1)] = (q * lax.rsqrt(jnp.sum(q * q, axis=-1, keepdims=True) + EPS)
                                          * (hd ** -0.5))
        k_ref[:, hd * h:hd * (h + 1)] = k * lax.rsqrt(jnp.sum(k * k, axis=-1, keepdims=True) + EPS)
    v_ref[...] = qkv[:, 2 * width:]

    ab = ab_ref[...]
    g = -jnp.exp(alog_ref[...]) * jax.nn.softplus(ab + dtb_ref[...])
    beta_ref[...] = pltpu.roll(jax.nn.sigmoid(ab), LANES - heads, 1)
    row = lax.broadcasted_iota(I32, (tc, LANES), 0) % c
    d = 1
    while d < c:
        g = g + jnp.where(row >= d, pltpu.roll(g, d, 0), 0.0)
        d *= 2
    gc_ref[...] = g

    ri = lax.broadcasted_iota(I32, (c, c), 0)
    ci = lax.broadcasted_iota(I32, (c, c), 1)
    tri = ri >= ci
    strict = ri > ci
    eye = (ri == ci).astype(F32)

    hs = range(heads)
    cols = [slice(hd * h, hd * (h + 1)) for h in hs]
    n_chunks = tc // c
    rows = [slice(c * n, c * (n + 1)) for n in range(n_chunks)]

    for g0 in range(0, n_chunks, GDN_PREP_CHUNKS):
        items = [(n, h) for n in range(g0, min(g0 + GDN_PREP_CHUNKS, n_chunks)) for h in hs]
        gc_all = {n: gc_ref[rows[n], :] for n, _ in items}
        gr_all = {n: gc_all[n].T for n in gc_all}
        beta_all = {n: beta_ref[rows[n], :] for n in gc_all}
        gcol = {(n, h): gc_all[n][:, h:h + 1] for n, h in items}
        grow = {(n, h): gr_all[n][h:h + 1, :] for n, h in items}
        beta = {(n, h): beta_all[n][:, h:h + 1] for n, h in items}
        qk_kk = {(n, h): _dot_nt(jnp.concatenate([q_ref[rows[n], cols[h]], k_ref[rows[n], cols[h]]],
                                                 axis=0).astype(BF16), k_ref[rows[n], cols[h]].astype(BF16))
                 for n, h in items}
        decay = {it: jnp.where(tri, jnp.exp(jnp.where(tri, gcol[it] - grow[it], 0.0)), 0.0) for it in items}
        m = {it: -jnp.where(strict, beta[it] * qk_kk[it][c:, :] * decay[it], 0.0) for it in items}
        p = {it: eye + m[it] for it in items}
        e = {it: _dot(m[it].astype(BF16), m[it].astype(BF16)) for it in items}
        lvl = 2
        while 2 * lvl < c:
            r = {it: _dot(jnp.concatenate([p[it], e[it]], axis=0).astype(BF16), e[it].astype(BF16)) for it in items}
            p = {it: p[it] + r[it][:c, :] for it in items}
            e = {it: r[it][c:, :] for it in items}
            lvl *= 2
        p = {it: p[it] + _dot(p[it].astype(BF16), e[it].astype(BF16)) for it in items}
        for n, h in items:
            it = (n, h)
            k = k_ref[rows[n], cols[h]]
            eg = jnp.exp(gcol[it])
            rhs = jnp.concatenate([beta[it] * v_ref[rows[n], cols[h]], (beta[it] * eg) * k], axis=1)
            uw_ref[rows[n], 2 * hd * h:2 * hd * (h + 1)] = _dot(p[it].astype(BF16), rhs.astype(BF16))
            aqk_ref[rows[n], LANES * h:LANES * h + c] = jnp.where(tri, qk_kk[it][:c, :] * decay[it], 0.0)
            qd_ref[rows[n], cols[h]] = q_ref[rows[n], cols[h]] * eg
            kd_ref[rows[n], cols[h]] = k * jnp.exp(gcol[it][c - 1:c, :] - gcol[it])

    for n in range(n_chunks):
        gc_n = gc_ref[rows[n], :]
        g_last = [jnp.exp(gc_n[c - 1:c, h:h + 1]) for h in hs]
        s = [state_refs[h][...] for h in hs]
        ws_qs = [_dot(jnp.concatenate([uw_ref[rows[n], 2 * hd * h + hd:2 * hd * (h + 1)], qd_ref[rows[n], cols[h]]],
                                      axis=0).astype(BF16), s[h].astype(BF16)) for h in hs]
        vb = [(uw_ref[rows[n], 2 * hd * h:2 * hd * h + hd] - ws_qs[h][:c, :]).astype(BF16) for h in hs]
        o = [ws_qs[h][c:, :] + _dot(aqk_ref[rows[n], LANES * h:LANES * h + c].astype(BF16), vb[h]) for h in hs]
        s_new = [s[h] * g_last[h] + _dot(kd_ref[rows[n], cols[h]].T.astype(BF16), vb[h]) for h in hs]
        for h in hs:
            state_refs[h][...] = s_new[h]
            on = o[h] * lax.rsqrt(jnp.mean(o[h] * o[h], axis=-1, keepdims=True) + EPS) * onorm_ref[...]
            o_ref[rows[n], cols[h]] = (on * jax.nn.silu(gate_ref[rows[n], cols[h]])).astype(o_ref.dtype)


def gdn(proj, cw, alog, dtb, onorm, bsz, seq, heads, tc):
    nt = seq // tc
    width = heads * GDN_HEAD_DIM
    kern = functools.partial(_gdn_kernel, tc=tc, heads=heads)
    row = lambda n: pl.BlockSpec((1, n), lambda b, i: (0, 0))
    return pl.pallas_call(
        kern,
        grid=(bsz, nt),
        in_specs=[pl.BlockSpec((tc, 3 * width), lambda b, i: (b * nt + i, 0)),
                  pl.BlockSpec((tc, width), lambda b, i: (b * nt + i, 3)),
                  pl.BlockSpec((tc, LANES), lambda b, i: (b * nt + i, 4 * width // LANES)),
                  pl.BlockSpec((CONV_WIDTH, 3 * width), lambda b, i: (0, 0)),
                  row(LANES), row(LANES), row(GDN_HEAD_DIM)],
        out_specs=pl.BlockSpec((tc, width), lambda b, i: (b * nt + i, 0)),
        out_shape=jax.ShapeDtypeStruct((bsz * seq, width), BF16),
        scratch_shapes=[pltpu.VMEM((tc + SUBLANES, 3 * width), F32),
                        pltpu.VMEM((tc, width), F32), pltpu.VMEM((tc, width), F32),
                        pltpu.VMEM((tc, width), F32),
                        pltpu.VMEM((tc, LANES), F32), pltpu.VMEM((tc, LANES), F32),
                        pltpu.VMEM((tc, 2 * width), F32), pltpu.VMEM((tc, heads * LANES), F32),
                        pltpu.VMEM((tc, width), F32), pltpu.VMEM((tc, width), F32),
                        *[pltpu.VMEM((GDN_HEAD_DIM, GDN_HEAD_DIM), F32) for _ in range(heads)]],
        compiler_params=_params(2),
    )(proj, proj, proj, cw, alog, dtb, onorm.reshape(1, -1))


def _router_kernel(h_ref, g_ref, r_ref, route_ref, cnt_ref, *, tm):
    t_hi, t_lo = _split_bf16(_rms(h_ref[...], g_ref[...]))
    r_hi, r_lo = _split_bf16(r_ref[...])
    logits = _dot(t_hi, r_hi) + (_dot(t_lo, r_hi) + _dot(t_hi, r_lo))
    lane = lax.broadcasted_iota(I32, (tm, LANES), 1)
    lane_f = lane.astype(F32)
    neg = -jnp.inf
    lg = jnp.where(lane < N_EXPERTS, logits, neg)
    m1 = jnp.max(lg, axis=-1, keepdims=True)
    i1 = jnp.min(jnp.where(lg == m1, lane_f, float(LANES)), axis=-1, keepdims=True)
    oh1 = lane_f == i1
    lg2 = jnp.where(oh1, neg, lg)
    m2 = jnp.max(lg2, axis=-1, keepdims=True)
    i2 = jnp.min(jnp.where(lg2 == m2, lane_f, float(LANES)), axis=-1, keepdims=True)
    oh2 = lane_f == i2
    e2 = jnp.exp(m2 - m1)
    w1 = 1.0 / (1.0 + e2)
    w2 = e2 / (1.0 + e2)

    both = (oh1 | oh2).astype(BF16)
    ri = lax.broadcasted_iota(I32, (tm, tm), 0)
    ci = lax.broadcasted_iota(I32, (tm, tm), 1)
    pos = _dot((ri > ci).astype(BF16), both)
    rank1 = jnp.sum(jnp.where(oh1, pos, 0.0), axis=-1, keepdims=True)
    rank2 = jnp.sum(jnp.where(oh2, pos, 0.0), axis=-1, keepdims=True)
    cnt_ref[0] = jnp.broadcast_to(jnp.sum(both.astype(F32), axis=0, keepdims=True), (SUBLANES, LANES))

    out = jnp.where(lane == 0, i1, 0.0)
    out = jnp.where(lane == 1, i2, out)
    out = jnp.where(lane == 2, rank1, out)
    out = jnp.where(lane == 3, rank2, out)
    out = jnp.where(lane == 4, w1, out)
    route_ref[...] = jnp.where(lane == 5, w2, out)


def router(h, g, r, tm):
    t, d = h.shape
    kern = functools.partial(_router_kernel, tm=tm)
    return pl.pallas_call(
        kern,
        grid=(t // tm,),
        in_specs=[pl.BlockSpec((tm, d), lambda i: (i, 0)),
                  pl.BlockSpec((1, d), lambda i: (0, 0)),
                  pl.BlockSpec((d, LANES), lambda i: (0, 0))],
        out_specs=[pl.BlockSpec((tm, LANES), lambda i: (i, 0)),
                   pl.BlockSpec((1, SUBLANES, LANES), lambda i: (i, 0, 0))],
        out_shape=[jax.ShapeDtypeStruct((t, LANES), F32),
                   jax.ShapeDtypeStruct((t // tm, SUBLANES, LANES), F32)],
        compiler_params=_params(1),
    )(h, g.reshape(1, d), r)


def _local_positions(route, off_row):
    lane_f = lax.broadcasted_iota(I32, route.shape, 1).astype(F32)
    lpos1 = jnp.sum(jnp.where(lane_f == route[:, 0:1], off_row, 0.0), axis=-1, keepdims=True) + route[:, 2:3]
    lpos2 = jnp.sum(jnp.where(lane_f == route[:, 1:2], off_row, 0.0), axis=-1, keepdims=True) + route[:, 3:4]
    return lpos1, lpos2


def _run_copies(n_rows, max_bit, make):
    out = []
    for k in range(max_bit, -1, -1):
        bit = k + 3
        offset = pl.multiple_of((n_rows >> (bit + 1)) << (bit + 1), SUBLANES)
        out.append((((n_rows >> bit) & 1) == 1, make(offset, SUBLANES << k)))
    return out


def _start_all(copies):
    for pred, cp in copies:
        pl.when(pred)(cp.start)


def _wait_all(copies):
    for pred, cp in copies:
        pl.when(pred)(cp.wait)


def _dispatch_kernel(seg_ref, cnt_ref, off_ref, tail_ref, h_ref, g_ref, route_ref, offrow_ref, x_hbm,
                     local, zeros, sem, zsem, *, tm, max_bit, tile_rows):
    b = pl.program_id(0)
    nb = pl.num_programs(0)
    slot = b % 2

    def block_copies(blk, s):
        out = []
        for e in range(N_EXPERTS):
            seg, off = seg_ref[blk * N_EXPERTS + e], off_ref[blk * N_EXPERTS + e]
            out += _run_copies(cnt_ref[blk * N_EXPERTS + e], max_bit, lambda o, n, seg=seg, off=off:
                               pltpu.make_async_copy(local.at[s, pl.ds(pl.multiple_of(off + o, SUBLANES), n)],
                                                     x_hbm.at[pl.ds(pl.multiple_of(seg + o, SUBLANES), n)],
                                                     sem.at[s]))
        return out

    def tail_copies():
        out = []
        for e in range(N_EXPERTS):
            start = tail_ref[e]
            out += _run_copies(tail_ref[N_EXPERTS + e], max_bit - 1, lambda o, n, start=start:
                               pltpu.make_async_copy(zeros.at[pl.ds(0, n)],
                                                     x_hbm.at[pl.ds(pl.multiple_of(start + o, SUBLANES), n)], zsem))
        return out

    @pl.when(b >= 2)
    def _():
        _wait_all(block_copies(b - 2, slot))

    t = _rms(h_ref[...], g_ref[...]).astype(BF16)
    lpos1, lpos2 = _local_positions(route_ref[...], offrow_ref[0])
    rows_f = lax.broadcasted_iota(I32, (tm, local.shape[1]), 1).astype(F32)
    place = ((rows_f == lpos1) | (rows_f == lpos2)).astype(BF16)
    local[slot] = lax.dot_general(place, t, (((0,), (0,)), ((), ())), preferred_element_type=F32)
    _start_all(block_copies(b, slot))

    @pl.when(b == nb - 1)
    def _():
        zeros[...] = jnp.zeros_like(zeros)
        _start_all(tail_copies())

        @pl.when(b >= 1)
        def _():
            _wait_all(block_copies(b - 1, 1 - slot))

        _wait_all(block_copies(b, slot))
        _wait_all(tail_copies())

        def fill_idle_tile(tile, carry):
            cp = pltpu.make_async_copy(zeros, x_hbm.at[pl.ds(pl.multiple_of(tile * tile_rows, tile_rows), tile_rows)],
                                       zsem)
            cp.start()
            cp.wait()
            return carry
        lax.fori_loop(tail_ref[2 * N_EXPERTS], x_hbm.shape[0] // tile_rows, fill_idle_tile, 0)


def moe_dispatch(seg, cnt8, off, tail, h, g, route, off_rows, n_rows, tm, local_rows, tile_rows):
    t, d = h.shape
    max_bit = (tm // SUBLANES).bit_length() - 1
    assert tile_rows >= tm // 2 and n_rows % tile_rows == 0
    kern = functools.partial(_dispatch_kernel, tm=tm, max_bit=max_bit, tile_rows=tile_rows)
    grid_spec = pltpu.PrefetchScalarGridSpec(
        num_scalar_prefetch=4,
        grid=(t // tm,),
        in_specs=[pl.BlockSpec((tm, d), lambda i, *_: (i, 0)),
                  pl.BlockSpec((1, d), lambda i, *_: (0, 0)),
                  pl.BlockSpec((tm, LANES), lambda i, *_: (i, 0)),
                  pl.BlockSpec((1, 1, LANES), lambda i, *_: (i, 0, 0))],
        out_specs=pl.BlockSpec(memory_space=pl.ANY),
        scratch_shapes=[pltpu.VMEM((2, local_rows, d), F32), pltpu.VMEM((tile_rows, d), F32),
                        pltpu.SemaphoreType.DMA((2,)), pltpu.SemaphoreType.DMA(())],
    )
    return pl.pallas_call(
        kern,
        grid_spec=grid_spec,
        out_shape=jax.ShapeDtypeStruct((n_rows, d), F32),
        compiler_params=_params(1),
    )(seg, cnt8, off, tail, h, g.reshape(1, d), route, off_rows)


def _moe_kernel(te_ref, nu_ref, x_ref, wg_ref, wu_ref, wd_ref, y_ref, xn_ref):
    i = pl.program_id(0)
    j = pl.program_id(1)
    n_used = nu_ref[0]

    @pl.when(i < n_used)
    def _():
        @pl.when(j == 0)
        def _():
            xn_ref[...] = x_ref[...].astype(BF16)

        xn = xn_ref[...]
        act = (jax.nn.silu(_dot(xn, wg_ref[0])) * _dot(xn, wu_ref[0])).astype(BF16)
        part = _dot(act, wd_ref[0])

        @pl.when(j == 0)
        def _():
            y_ref[...] = part

        @pl.when(j > 0)
        def _():
            y_ref[...] += part

    @pl.when((i >= n_used) & (j == 0))
    def _():
        y_ref[...] = jnp.zeros_like(y_ref)


def moe_experts(tile_expert, n_used, x, wg, wu, wd, tm, tf):
    n_rows, d = x.shape
    f = wg.shape[2]
    n_ff = f // tf

    def ff_idx(i, j, nu):
        return jnp.where(i < nu[0], j, n_ff - 1)

    grid_spec = pltpu.PrefetchScalarGridSpec(
        num_scalar_prefetch=2,
        grid=(n_rows // tm, n_ff),
        in_specs=[pl.BlockSpec((tm, d), lambda i, j, te, nu: (jnp.minimum(i, nu[0] - 1), 0)),
                  pl.BlockSpec((1, d, tf), lambda i, j, te, nu: (te[i], 0, ff_idx(i, j, nu))),
                  pl.BlockSpec((1, d, tf), lambda i, j, te, nu: (te[i], 0, ff_idx(i, j, nu))),
                  pl.BlockSpec((1, tf, d), lambda i, j, te, nu: (te[i], ff_idx(i, j, nu), 0))],
        out_specs=pl.BlockSpec((tm, d), lambda i, j, te, nu: (i, 0)),
        scratch_shapes=[pltpu.VMEM((tm, d), BF16)],
    )
    return pl.pallas_call(
        _moe_kernel,
        grid_spec=grid_spec,
        out_shape=jax.ShapeDtypeStruct((n_rows, d), F32),
        compiler_params=_params(2),
    )(tile_expert, n_used, x, wg, wu, wd)


def _combine_kernel(seg_ref, cnt_ref, off_ref, h_ref, route_ref, offrow_ref, y_hbm, o_ref, local, sem,
                    *, tm, max_bit):
    b = pl.program_id(0)
    nb = pl.num_programs(0)
    slot = b % 2

    def block_copies(blk, s):
        out = []
        for e in range(N_EXPERTS):
            seg, off = seg_ref[blk * N_EXPERTS + e], off_ref[blk * N_EXPERTS + e]
            out += _run_copies(cnt_ref[blk * N_EXPERTS + e], max_bit, lambda o, n, seg=seg, off=off:
                               pltpu.make_async_copy(y_hbm.at[pl.ds(pl.multiple_of(seg + o, SUBLANES), n)],
                                                     local.at[s, pl.ds(pl.multiple_of(off + o, SUBLANES), n)],
                                                     sem.at[s]))
        return out

    @pl.when(b == 0)
    def _():
        local[...] = jnp.zeros_like(local)
        _start_all(block_copies(0, 0))

    _wait_all(block_copies(b, slot))

    @pl.when(b + 1 < nb)
    def _():
        _start_all(block_copies(b + 1, 1 - slot))

    route = route_ref[...]
    lpos1, lpos2 = _local_positions(route, offrow_ref[0])
    rows_f = lax.broadcasted_iota(I32, (tm, local.shape[1]), 1).astype(F32)
    gate = jnp.where(rows_f == lpos1, route[:, 4:5], 0.0) + jnp.where(rows_f == lpos2, route[:, 5:6], 0.0)
    g_hi, g_lo = _split_bf16(gate)
    y_hi, y_lo = _split_bf16(local[slot])
    o_ref[...] = h_ref[...] + (_dot(g_hi, y_hi) + (_dot(g_lo, y_hi) + _dot(g_hi, y_lo)))


def moe_combine(seg, cnt8, off, h, route, off_rows, y, tm, local_rows):
    t, d = h.shape
    max_bit = (tm // SUBLANES).bit_length() - 1
    kern = functools.partial(_combine_kernel, tm=tm, max_bit=max_bit)
    grid_spec = pltpu.PrefetchScalarGridSpec(
        num_scalar_prefetch=3,
        grid=(t // tm,),
        in_specs=[pl.BlockSpec((tm, d), lambda i, *_: (i, 0)),
                  pl.BlockSpec((tm, LANES), lambda i, *_: (i, 0)),
                  pl.BlockSpec((1, 1, LANES), lambda i, *_: (i, 0, 0)),
                  pl.BlockSpec(memory_space=pl.ANY)],
        out_specs=pl.BlockSpec((tm, d), lambda i, *_: (i, 0)),
        scratch_shapes=[pltpu.VMEM((2, local_rows, d), F32), pltpu.SemaphoreType.DMA((2,))],
    )
    return pl.pallas_call(
        kern,
        grid_spec=grid_spec,
        out_shape=jax.ShapeDtypeStruct((t, d), F32),
        compiler_params=_params(1),
    )(seg, cnt8, off, h, route, off_rows, y)


def _rope_tables(seq):
    half = ATT_HEAD_DIM // 2
    inv_freq = ROPE_THETA ** (-jnp.arange(half, dtype=F32) / half)
    ang = jnp.arange(seq, dtype=jnp.int32).astype(F32)[:, None] * inv_freq[None, :]
    reps = LANES // half
    cos = jnp.tile(jnp.cos(ang), (1, reps))
    sign = jnp.tile(jnp.concatenate([-jnp.ones((half,), F32), jnp.ones((half,), F32)]), LANES // ATT_HEAD_DIM)
    sin = jnp.tile(jnp.sin(ang), (1, reps)) * sign[None, :]
    return cos, sin


def _even_layer(h, bsz, seq, ln_mix, ln_ffn, w_in, conv_w, conv_b, w_a, b_a, w_i, b_i, lam,
                q_norm, k_norm, sinks, w_out, f_gate, f_up, f_down):
    lru_w = conv_w.shape[1]
    q_heads = sinks.shape[0]
    q_w = q_heads * ATT_HEAD_DIM
    kv_w = (w_in.shape[1] - 2 * lru_w - q_w) // 2
    kv_heads = kv_w // ATT_HEAD_DIM

    proj = rms_matmul(h, ln_mix, w_in.astype(BF16), tm=512, tn=w_in.shape[1])
    gates_w = jnp.concatenate([block_diag(*w_a), block_diag(*w_i)], axis=1).astype(BF16)
    y_lru = rglru(proj, conv_w, conv_b, gates_w, jnp.concatenate([b_a, b_i]), lam, bsz, seq, lru_w, tm=256)
    cos, sin = _rope_tables(seq)
    tile2 = lambda g: jnp.tile(g, LANES // ATT_HEAD_DIM).reshape(1, LANES)
    y_att = swa(proj, sinks, cos, sin, tile2(q_norm), tile2(k_norm), bsz, seq,
                q_col=2 * lru_w, k_col=2 * lru_w + q_w, v_col=2 * lru_w + q_w + kv_w,
                q_heads=q_heads, kv_heads=kv_heads)
    w_out = w_out.astype(BF16)
    h = proj_res(h, [y_lru, y_att], [w_out[:lru_w], w_out[lru_w:]], tm=512)
    return ffn(h, ln_ffn, f_gate.astype(BF16), f_up.astype(BF16), f_down.astype(BF16), tm=512, tf=1408)


def _odd_layer(h, bsz, seq, ln_mix, ln_ffn, w_in, conv_w, a_log, dt_bias, out_norm, w_out, router_w,
               m_gate, m_up, m_down):
    t, d = h.shape
    heads = a_log.shape[0]
    width = heads * GDN_HEAD_DIM
    cols = w_in.shape[1]
    pad = (-cols) % (11 * LANES)
    w_in_p = jnp.pad(w_in, ((0, 0), (0, pad))).astype(BF16)
    proj = rms_matmul(h, ln_mix, w_in_p, tm=512, tn=w_in_p.shape[1])
    lane_pad = lambda v: jnp.pad(v, (0, LANES - heads)).reshape(1, LANES)
    y = gdn(proj, conv_w, lane_pad(a_log), lane_pad(dt_bias), out_norm, bsz, seq, heads, tc=256)
    h = proj_res(h, [y], [w_out.astype(BF16)], tm=512)

    tb = MOE_BLOCK
    tm = MOE_TILE
    nblk = t // tb
    route, cnt = router(h, ln_ffn, jnp.pad(router_w, ((0, 0), (0, LANES - N_EXPERTS))), tm=tb)
    cnt8 = (cnt[:, 0, :N_EXPERTS].astype(I32) + SUBLANES - 1) // SUBLANES * SUBLANES
    total8 = jnp.sum(cnt8, axis=0)
    tiles_per = (total8 + tm - 1) // tm
    tile_end = jnp.cumsum(tiles_per)
    starts = (tile_end - tiles_per) * tm
    n_used = tile_end[-1:]
    n_tiles = (2 * t + nblk * N_EXPERTS * (SUBLANES - 1)) // tm + N_EXPERTS
    tile_ids = jnp.minimum(jnp.arange(n_tiles, dtype=I32), n_used - 1)
    tile_expert = jnp.sum((tile_ids[:, None] >= tile_end[None, :]).astype(I32), axis=1)
    seg = (starts[None, :] + jnp.cumsum(cnt8, axis=0) - cnt8).reshape(-1)
    off = jnp.cumsum(cnt8, axis=1) - cnt8
    off_rows = jnp.pad(off.astype(F32), ((0, 0), (0, LANES - N_EXPERTS))).reshape(nblk, 1, LANES)
    tail = jnp.concatenate([starts + total8, tiles_per * tm - total8, n_used])
    local_rows = -(-(2 * tb + N_EXPERTS * (SUBLANES - 1)) // LANES) * LANES
    cnt8, off = cnt8.reshape(-1), off.reshape(-1)

    x = moe_dispatch(seg, cnt8, off, tail, h, ln_ffn, route, off_rows, n_tiles * tm, tb, local_rows, tm)
    y = moe_experts(tile_expert, n_used, x, m_gate.astype(BF16), m_up.astype(BF16), m_down.astype(BF16),
                    tm, MOE_FF_CHUNK)
    return moe_combine(seg, cnt8, off, h, route, off_rows, y, tb, local_rows)


def kernel(x, ln_mix, ln_ffn, e_w_in, e_lru_conv_w, e_lru_conv_b, e_lru_w_a, e_lru_b_a, e_lru_w_i, e_lru_b_i, e_lru_lambda, e_q_norm, e_k_norm, e_sinks, e_w_out, e_ffn_w_gate, e_ffn_w_up, e_ffn_w_down, o_w_in, o_conv_w, o_a_log, o_dt_bias, o_out_norm, o_w_out, o_router, o_moe_w_gate, o_moe_w_up, o_moe_w_down):
    bsz, seq, d = x.shape
    h = x.reshape(bsz * seq, d)
    for layer in range(ln_mix.shape[0]):
        j = layer // 2
        if layer % 2 == 0:
            h = _even_layer(h, bsz, seq, ln_mix[layer], ln_ffn[layer], e_w_in[j], e_lru_conv_w[j],
                            e_lru_conv_b[j], e_lru_w_a[j], e_lru_b_a[j], e_lru_w_i[j], e_lru_b_i[j],
                            e_lru_lambda[j], e_q_norm[j], e_k_norm[j], e_sinks[j], e_w_out[j],
                            e_ffn_w_gate[j], e_ffn_w_up[j], e_ffn_w_down[j])
        else:
            h = _odd_layer(h, bsz, seq, ln_mix[layer], ln_ffn[layer], o_w_in[j], o_conv_w[j], o_a_log[j],
                           o_dt_bias[j], o_out_norm[j], o_w_out[j], o_router[j], o_moe_w_gate[j],
                           o_moe_w_up[j], o_moe_w_down[j])
    return h.reshape(bsz, seq, d)
```

```python
import functools

import jax
import jax.numpy as jnp
from jax import lax
from jax.scipy.linalg import block_diag
from jax.experimental import pallas as pl
from jax.experimental.pallas import tpu as pltpu

F32 = jnp.float32
BF16 = jnp.bfloat16
I32 = jnp.int32

EPS = 1e-6
LANES = 128
SUBLANES = 8
MXU_WIDTH = 256
VMEM_LIMIT = 52 * 1024 * 1024

CONV_WIDTH = 4
LRU_C = 8.0
ATT_HEAD_DIM = 64
ATT_WINDOW = 128
ROPE_THETA = 10000.0
GDN_HEAD_DIM = 128
GDN_CHUNK = 64
GDN_PREP_CHUNKS = 2
N_EXPERTS = 8
MOE_TILE = 512
MOE_BLOCK = 512
MOE_FF_CHUNK = 1792


def _params(n_axes, vmem=VMEM_LIMIT):
    return pltpu.CompilerParams(dimension_semantics=("arbitrary",) * n_axes, vmem_limit_bytes=vmem)


def _rms(x, g):
    return x * lax.rsqrt(jnp.mean(x * x, axis=-1, keepdims=True) + EPS) * g


def _dot(a, b):
    return jnp.dot(a, b, preferred_element_type=F32)


def _dot_nt(a, b):
    return lax.dot_general(a, b, (((1,), (1,)), ((), ())), preferred_element_type=F32)


def _split_bf16(x):
    hi = x.astype(BF16)
    lo = (x - hi.astype(F32)).astype(BF16)
    return hi, lo


def _rms_matmul_kernel(x_ref, g_ref, w_ref, o_ref, xn_ref):
    @pl.when(pl.program_id(1) == 0)
    def _():
        xn_ref[...] = _rms(x_ref[...], g_ref[...]).astype(BF16)

    o_ref[...] = _dot(xn_ref[...], w_ref[...])


def rms_matmul(x, g, w, tm, tn):
    t, d = x.shape
    n = w.shape[1]
    return pl.pallas_call(
        _rms_matmul_kernel,
        grid=(t // tm, n // tn),
        in_specs=[pl.BlockSpec((tm, d), lambda i, j: (i, 0)),
                  pl.BlockSpec((1, d), lambda i, j: (0, 0)),
                  pl.BlockSpec((d, tn), lambda i, j: (0, j))],
        out_specs=pl.BlockSpec((tm, tn), lambda i, j: (i, j)),
        out_shape=jax.ShapeDtypeStruct((t, n), F32),
        scratch_shapes=[pltpu.VMEM((tm, d), BF16)],
        compiler_params=_params(2),
    )(x, g.reshape(1, d), w)


def _causal_conv(ext_ref, x, cw, tm):
    ext_ref[SUBLANES:SUBLANES + tm, :] = x
    base = SUBLANES - (CONV_WIDTH - 1)
    acc = cw[0:1, :] * ext_ref[base:base + tm, :]
    for j in range(1, CONV_WIDTH):
        acc = acc + cw[j:j + 1, :] * ext_ref[base + j:base + j + tm, :]
    ext_ref[0:SUBLANES, :] = x[tm - SUBLANES:tm, :]
    return acc


def _rglru_kernel(xb_ref, gate_ref, cw_ref, cb_ref, wg_ref, bg_ref, lam_ref, o_ref, ext_ref, h_ref,
                  *, tm, width):
    @pl.when(pl.program_id(1) == 0)
    def _():
        ext_ref[0:SUBLANES, :] = jnp.zeros((SUBLANES, width), F32)
        h_ref[...] = jnp.zeros_like(h_ref)

    xc = _causal_conv(ext_ref, xb_ref[...], cw_ref[...], tm) + cb_ref[...]
    z = _dot(xc.astype(BF16), wg_ref[...]) + bg_ref[...]
    r = jax.nn.sigmoid(z[:, :width])
    gi = jax.nn.sigmoid(z[:, width:])
    log_a = -LRU_C * r * jax.nn.softplus(-lam_ref[...])
    a = jnp.exp(log_a)
    u = jnp.sqrt(1.0 - a * a) * (gi * xc)

    row = lax.broadcasted_iota(I32, (tm, width), 0)
    d = 1
    while d < tm:
        keep = row >= d
        u = u + jnp.where(keep, a * pltpu.roll(u, d, 0), 0.0)
        a = jnp.where(keep, a * pltpu.roll(a, d, 0), a)
        d *= 2
    h = u + a * h_ref[0:1, :]
    h_ref[...] = jnp.broadcast_to(h[tm - 1:tm, :], h_ref.shape)
    o_ref[...] = (h * jax.nn.gelu(gate_ref[...])).astype(o_ref.dtype)


def rglru(proj, cw, cb, wg, bg, lam, bsz, seq, width, tm):
    nt = seq // tm
    kern = functools.partial(_rglru_kernel, tm=tm, width=width)
    row = lambda c: pl.BlockSpec((1, c), lambda b, i: (0, 0))
    return pl.pallas_call(
        kern,
        grid=(bsz, nt),
        in_specs=[pl.BlockSpec((tm, width), lambda b, i: (b * nt + i, 0)),
                  pl.BlockSpec((tm, width), lambda b, i: (b * nt + i, 1)),
                  pl.BlockSpec((CONV_WIDTH, width), lambda b, i: (0, 0)),
                  row(width),
                  pl.BlockSpec((width, 2 * width), lambda b, i: (0, 0)),
                  row(2 * width),
                  row(width)],
        out_specs=pl.BlockSpec((tm, width), lambda b, i: (b * nt + i, 0)),
        out_shape=jax.ShapeDtypeStruct((bsz * seq, width), BF16),
        scratch_shapes=[pltpu.VMEM((tm + SUBLANES, width), F32), pltpu.VMEM((SUBLANES, width), F32)],
        compiler_params=_params(2),
    )(proj, proj, cw, cb.reshape(1, -1), wg, bg.reshape(1, -1), lam.reshape(1, -1))


def _swa_kernel(sink_ref, q_ref, k_ref, v_ref, cos_ref, sin_ref, qg_ref, kg_ref, o_ref, kall_ref, vall_ref,
                *, q_heads, kv_heads):
    w = ATT_WINDOW
    n = pl.program_id(1)
    group = q_heads // kv_heads
    assert kv_heads * ATT_HEAD_DIM == LANES and group % 2 == 0

    @pl.when(n == 0)
    def _():
        kall_ref[0:w, :] = jnp.zeros((w, LANES), BF16)
        vall_ref[0:w, :] = jnp.zeros((w, LANES), BF16)

    @pl.when(n > 0)
    def _():
        kall_ref[0:w, :] = kall_ref[w:2 * w, :]
        vall_ref[0:w, :] = vall_ref[w:2 * w, :]

    lane = lax.broadcasted_iota(I32, (w, LANES), 1)
    first_half = (lane % ATT_HEAD_DIM) < (ATT_HEAD_DIM // 2)
    lane_head = lane // ATT_HEAD_DIM
    bd_r = lax.broadcasted_iota(I32, (LANES, LANES), 0) // ATT_HEAD_DIM
    bd_c = lax.broadcasted_iota(I32, (LANES, LANES), 1) // ATT_HEAD_DIM
    head_ones = (bd_r == bd_c).astype(BF16)
    cos = cos_ref[...]
    sin = sin_ref[...]

    def norm_rope(x, g):
        hi, lo = _split_bf16(x * x)
        ss = _dot(hi, head_ones) + _dot(lo, head_ones)
        xn = x * lax.rsqrt(ss * (1.0 / ATT_HEAD_DIM) + EPS) * g
        rot = jnp.where(first_half, pltpu.roll(xn, LANES - ATT_HEAD_DIM // 2, 1),
                        pltpu.roll(xn, ATT_HEAD_DIM // 2, 1))
        return xn * cos + rot * sin

    kall_ref[w:2 * w, :] = norm_rope(k_ref[...], kg_ref[...]).astype(BF16)
    vall_ref[w:2 * w, :] = v_ref[...].astype(BF16)
    qc = [norm_rope(q_ref[:, LANES * c:LANES * (c + 1)], qg_ref[...]) for c in range(q_heads // 2)]

    rows = group * w
    qi = lax.broadcasted_iota(I32, (rows, 2 * w), 0) % w
    kj = lax.broadcasted_iota(I32, (rows, 2 * w), 1)
    valid = (kj > qi) & (kj <= qi + w) & ((n > 0) | (kj >= w))
    row_head = lax.broadcasted_iota(I32, (rows, 1), 0) // w

    outs = []
    for g in range(kv_heads):
        parts = []
        for hh in range(group):
            h = g * group + hh
            x = qc[h // 2]
            if h % 2 != g:
                x = pltpu.roll(x, ATT_HEAD_DIM, 1)
            parts.append(jnp.where(lane_head == g, x, 0.0))
        qs = jnp.concatenate(parts, axis=0).astype(BF16)
        s = _dot_nt(qs, kall_ref[...]) * (ATT_HEAD_DIM ** -0.5)
        s = jnp.where(valid, s, -1e30)
        sink = jnp.zeros((rows, 1), F32)
        for hh in range(group):
            sink = jnp.where(row_head == hh, sink_ref[g * group + hh], sink)
        m = jnp.maximum(jnp.max(s, axis=-1, keepdims=True), sink)
        p = jnp.exp(s - m)
        denom = jnp.sum(p, axis=-1, keepdims=True) + jnp.exp(sink - m)
        pv = _dot(p.astype(BF16), vall_ref[...]) / denom
        outs.extend(pv[w * hh:w * (hh + 1), :] for hh in range(group))

    for c in range(q_heads // 2):
        g = (2 * c) // group
        a, b = outs[2 * c], outs[2 * c + 1]
        if g == 1:
            a = pltpu.roll(a, ATT_HEAD_DIM, 1)
        else:
            b = pltpu.roll(b, ATT_HEAD_DIM, 1)
        o_ref[:, LANES * c:LANES * (c + 1)] = jnp.where(lane_head == 0, a, b).astype(o_ref.dtype)


def swa(proj, sinks, cos, sin, qg, kg, bsz, seq, q_col, k_col, v_col, q_heads, kv_heads):
    w = ATT_WINDOW
    nb = seq // w
    qw = q_heads * ATT_HEAD_DIM
    kern = functools.partial(_swa_kernel, q_heads=q_heads, kv_heads=kv_heads)
    return pl.pallas_call(
        kern,
        grid=(bsz, nb),
        in_specs=[pl.BlockSpec(memory_space=pltpu.SMEM),
                  pl.BlockSpec((w, qw), lambda b, i: (b * nb + i, q_col // qw)),
                  pl.BlockSpec((w, LANES), lambda b, i: (b * nb + i, k_col // LANES)),
                  pl.BlockSpec((w, LANES), lambda b, i: (b * nb + i, v_col // LANES)),
                  pl.BlockSpec((w, LANES), lambda b, i: (i, 0)),
                  pl.BlockSpec((w, LANES), lambda b, i: (i, 0)),
                  pl.BlockSpec((1, LANES), lambda b, i: (0, 0)),
                  pl.BlockSpec((1, LANES), lambda b, i: (0, 0))],
        out_specs=pl.BlockSpec((w, qw), lambda b, i: (b * nb + i, 0)),
        out_shape=jax.ShapeDtypeStruct((bsz * seq, qw), BF16),
        scratch_shapes=[pltpu.VMEM((2 * w, LANES), BF16), pltpu.VMEM((2 * w, LANES), BF16)],
        compiler_params=_params(2),
    )(sinks, proj, proj, proj, cos, sin, qg, kg)


def _mix_specs(ys, ws, tm, index_map):
    return ([pl.BlockSpec((tm, y.shape[1]), index_map(lambda i: (i, 0))) for y in ys]
            + [pl.BlockSpec(w.shape, index_map(lambda i: (0, 0))) for w in ws])


def _mix_residual(h_ref, mix_refs):
    n_in = len(mix_refs) // 2
    x = h_ref[...]
    for y_ref, w_ref in zip(mix_refs[:n_in], mix_refs[n_in:]):
        x = x + _dot(y_ref[...], w_ref[...])
    return x


def _ffn_kernel(h_ref, *refs):
    *mix_refs, g_ref, wg_ref, wu_ref, wd_ref, o_ref, xn_ref = refs
    j = pl.program_id(1)

    @pl.when(j == 0)
    def _():
        x = _mix_residual(h_ref, mix_refs)
        xn_ref[...] = _rms(x, g_ref[...]).astype(BF16)
        o_ref[...] = x

    xn = xn_ref[...]
    act = (jax.nn.silu(_dot(xn, wg_ref[...])) * _dot(xn, wu_ref[...])).astype(BF16)
    o_ref[...] += _dot(act, wd_ref[...])


def ffn(h, ys, ws, g, wg, wu, wd, tm, tf):
    t, d = h.shape
    f = wg.shape[1]
    return pl.pallas_call(
        _ffn_kernel,
        grid=(t // tm, f // tf),
        in_specs=([pl.BlockSpec((tm, d), lambda i, j: (i, 0))]
                  + _mix_specs(ys, ws, tm, lambda fn: lambda i, j: fn(i))
                  + [pl.BlockSpec((1, d), lambda i, j: (0, 0)),
                     pl.BlockSpec((d, tf), lambda i, j: (0, j)),
                     pl.BlockSpec((d, tf), lambda i, j: (0, j)),
                     pl.BlockSpec((tf, d), lambda i, j: (j, 0))]),
        out_specs=pl.BlockSpec((tm, d), lambda i, j: (i, 0)),
        out_shape=jax.ShapeDtypeStruct((t, d), F32),
        scratch_shapes=[pltpu.VMEM((tm, d), BF16)],
        compiler_params=_params(2),
    )(h, *ys, *ws, g.reshape(1, d), wg, wu, wd)


def _gdn_kernel(qkv_ref, gate_ref, ab_ref, cw_ref, alog_ref, dtb_ref, onorm_ref, o_ref,
                ext_ref, q_ref, k_ref, v_ref, gc_ref, beta_ref, uw_ref, aqk_ref, qd_ref, kd_ref, *state_refs,
                tc, heads):
    hd = GDN_HEAD_DIM
    c = GDN_CHUNK
    width = heads * hd

    @pl.when(pl.program_id(1) == 0)
    def _():
        ext_ref[0:SUBLANES, :] = jnp.zeros((SUBLANES, 3 * width), F32)
        for s_ref in state_refs:
            s_ref[...] = jnp.zeros_like(s_ref)

    qkv = jax.nn.silu(_causal_conv(ext_ref, qkv_ref[...], cw_ref[...], tc))
    for h in range(heads):
        q = qkv[:, hd * h:hd * (h + 1)]
        k = qkv[:, width + hd * h:width + hd * (h + 1)]
        q_ref[:, hd * h:hd * (h + 1)] = (q * lax.rsqrt(jnp.sum(q * q, axis=-1, keepdims=True) + EPS)
                                          * (hd ** -0.5))
        k_ref[:, hd * h:hd * (h + 1)] = k * lax.rsqrt(jnp.sum(k * k, axis=-1, keepdims=True) + EPS)
    v_ref[...] = qkv[:, 2 * width:]

    ab = ab_ref[...]
    g = -jnp.exp(alog_ref[...]) * jax.nn.softplus(ab + dtb_ref[...])
    beta_ref[...] = pltpu.roll(jax.nn.sigmoid(ab), LANES - heads, 1)
    row = lax.broadcasted_iota(I32, (tc, LANES), 0) % c
    d = 1
    while d < c:
        g = g + jnp.where(row >= d, pltpu.roll(g, d, 0), 0.0)
        d *= 2
    gc_ref[...] = g

    ri = lax.broadcasted_iota(I32, (c, c), 0)
    ci = lax.broadcasted_iota(I32, (c, c), 1)
    tri = ri >= ci
    strict = ri > ci
    eye = (ri == ci).astype(F32)

    hs = range(heads)
    cols = [slice(hd * h, hd * (h + 1)) for h in hs]
    n_chunks = tc // c
    rows = [slice(c * n, c * (n + 1)) for n in range(n_chunks)]

    for g0 in range(0, n_chunks, GDN_PREP_CHUNKS):
        items = [(n, h) for n in range(g0, min(g0 + GDN_PREP_CHUNKS, n_chunks)) for h in hs]
        gc_all = {n: gc_ref[rows[n], :] for n, _ in items}
        gr_all = {n: gc_all[n].T for n in gc_all}
        beta_all = {n: beta_ref[rows[n], :] for n in gc_all}
        gcol = {(n, h): gc_all[n][:, h:h + 1] for n, h in items}
        grow = {(n, h): gr_all[n][h:h + 1, :] for n, h in items}
        beta = {(n, h): beta_all[n][:, h:h + 1] for n, h in items}
        qk_kk = {(n, h): _dot_nt(jnp.concatenate([q_ref[rows[n], cols[h]], k_ref[rows[n], cols[h]]],
                                                 axis=0).astype(BF16), k_ref[rows[n], cols[h]].astype(BF16))
                 for n, h in items}
        decay = {it: jnp.where(tri, jnp.exp(jnp.where(tri, gcol[it] - grow[it], 0.0)), 0.0) for it in items}
        m = {it: -jnp.where(strict, beta[it] * qk_kk[it][c:, :] * decay[it], 0.0) for it in items}
        p = {it: eye + m[it] for it in items}
        e = {it: _dot(m[it].astype(BF16), m[it].astype(BF16)) for it in items}
        lvl = 2
        while 2 * lvl < c:
            r = {it: _dot(jnp.concatenate([p[it], e[it]], axis=0).astype(BF16), e[it].astype(BF16)) for it in items}
            p = {it: p[it] + r[it][:c, :] for it in items}
            e = {it: r[it][c:, :] for it in items}
            lvl *= 2
        p = {it: p[it] + _dot(p[it].astype(BF16), e[it].astype(BF16)) for it in items}
        for n, h in items:
            it = (n, h)
            k = k_ref[rows[n], cols[h]]
            eg = jnp.exp(gcol[it])
            rhs = jnp.concatenate([beta[it] * v_ref[rows[n], cols[h]], (beta[it] * eg) * k], axis=1)
            uw_ref[rows[n], 2 * hd * h:2 * hd * (h + 1)] = _dot(p[it].astype(BF16), rhs.astype(BF16))
            aqk_ref[rows[n], LANES * h:LANES * h + c] = jnp.where(tri, qk_kk[it][:c, :] * decay[it], 0.0)
            qd_ref[rows[n], cols[h]] = q_ref[rows[n], cols[h]] * eg
            kd_ref[rows[n], cols[h]] = k * jnp.exp(gcol[it][c - 1:c, :] - gcol[it])

    for n in range(n_chunks):
        gc_n = gc_ref[rows[n], :]
        g_last = [jnp.exp(gc_n[c - 1:c, h:h + 1]) for h in hs]
        s = [state_refs[h][...] for h in hs]
        ws_qs = [_dot(jnp.concatenate([uw_ref[rows[n], 2 * hd * h + hd:2 * hd * (h + 1)], qd_ref[rows[n], cols[h]]],
                                      axis=0).astype(BF16), s[h].astype(BF16)) for h in hs]
        vb = [(uw_ref[rows[n], 2 * hd * h:2 * hd * h + hd] - ws_qs[h][:c, :]).astype(BF16) for h in hs]
        o = [ws_qs[h][c:, :] + _dot(aqk_ref[rows[n], LANES * h:LANES * h + c].astype(BF16), vb[h]) for h in hs]
        s_new = [s[h] * g_last[h] + _dot(kd_ref[rows[n], cols[h]].T.astype(BF16), vb[h]) for h in hs]
        for h in hs:
            state_refs[h][...] = s_new[h]
            on = o[h] * lax.rsqrt(jnp.mean(o[h] * o[h], axis=-1, keepdims=True) + EPS) * onorm_ref[...]
            o_ref[rows[n], cols[h]] = (on * jax.nn.silu(gate_ref[rows[n], cols[h]])).astype(o_ref.dtype)


def gdn(proj, cw, alog, dtb, onorm, bsz, seq, heads, tc):
    nt = seq // tc
    width = heads * GDN_HEAD_DIM
    kern = functools.partial(_gdn_kernel, tc=tc, heads=heads)
    row = lambda n: pl.BlockSpec((1, n), lambda b, i: (0, 0))
    return pl.pallas_call(
        kern,
        grid=(bsz, nt),
        in_specs=[pl.BlockSpec((tc, 3 * width), lambda b, i: (b * nt + i, 0)),
                  pl.BlockSpec((tc, width), lambda b, i: (b * nt + i, 3)),
                  pl.BlockSpec((tc, LANES), lambda b, i: (b * nt + i, 4 * width // LANES)),
                  pl.BlockSpec((CONV_WIDTH, 3 * width), lambda b, i: (0, 0)),
                  row(LANES), row(LANES), row(GDN_HEAD_DIM)],
        out_specs=pl.BlockSpec((tc, width), lambda b, i: (b * nt + i, 0)),
        out_shape=jax.ShapeDtypeStruct((bsz * seq, width), BF16),
        scratch_shapes=[pltpu.VMEM((tc + SUBLANES, 3 * width), F32),
                        pltpu.VMEM((tc, width), F32), pltpu.VMEM((tc, width), F32),
                        pltpu.VMEM((tc, width), F32),
                        pltpu.VMEM((tc, LANES), F32), pltpu.VMEM((tc, LANES), F32),
                        pltpu.VMEM((tc, 2 * width), F32), pltpu.VMEM((tc, heads * LANES), F32),
                        pltpu.VMEM((tc, width), F32), pltpu.VMEM((tc, width), F32),
                        *[pltpu.VMEM((GDN_HEAD_DIM, GDN_HEAD_DIM), F32) for _ in range(heads)]],
        compiler_params=_params(2),
    )(proj, proj, proj, cw, alog, dtb, onorm.reshape(1, -1))


def _router_kernel(h_ref, *refs, tm):
    *mix_refs, g_ref, r_ref, x_ref, route_ref, cnt_ref = refs
    x = _mix_residual(h_ref, mix_refs)
    x_ref[...] = x
    t_hi, t_lo = _split_bf16(_rms(x, g_ref[...]))
    r_hi, r_lo = _split_bf16(r_ref[...])
    logits = _dot(t_hi, r_hi) + (_dot(t_lo, r_hi) + _dot(t_hi, r_lo))
    lane = lax.broadcasted_iota(I32, (tm, LANES), 1)
    lane_f = lane.astype(F32)
    neg = -jnp.inf
    lg = jnp.where(lane < N_EXPERTS, logits, neg)
    m1 = jnp.max(lg, axis=-1, keepdims=True)
    i1 = jnp.min(jnp.where(lg == m1, lane_f, float(LANES)), axis=-1, keepdims=True)
    oh1 = lane_f == i1
    lg2 = jnp.where(oh1, neg, lg)
    m2 = jnp.max(lg2, axis=-1, keepdims=True)
    i2 = jnp.min(jnp.where(lg2 == m2, lane_f, float(LANES)), axis=-1, keepdims=True)
    oh2 = lane_f == i2
    e2 = jnp.exp(m2 - m1)
    w1 = 1.0 / (1.0 + e2)
    w2 = e2 / (1.0 + e2)

    both = (oh1 | oh2).astype(BF16)
    ri = lax.broadcasted_iota(I32, (tm, tm), 0)
    ci = lax.broadcasted_iota(I32, (tm, tm), 1)
    pos = _dot((ri > ci).astype(BF16), both)
    rank1 = jnp.sum(jnp.where(oh1, pos, 0.0), axis=-1, keepdims=True)
    rank2 = jnp.sum(jnp.where(oh2, pos, 0.0), axis=-1, keepdims=True)
    cnt_ref[0] = jnp.broadcast_to(jnp.sum(both.astype(F32), axis=0, keepdims=True), (SUBLANES, LANES))

    out = jnp.where(lane == 0, i1, 0.0)
    out = jnp.where(lane == 1, i2, out)
    out = jnp.where(lane == 2, rank1, out)
    out = jnp.where(lane == 3, rank2, out)
    out = jnp.where(lane == 4, w1, out)
    route_ref[...] = jnp.where(lane == 5, w2, out)


def router(h, ys, ws, g, r, tm):
    t, d = h.shape
    kern = functools.partial(_router_kernel, tm=tm)
    return pl.pallas_call(
        kern,
        grid=(t // tm,),
        in_specs=([pl.BlockSpec((tm, d), lambda i: (i, 0))]
                  + _mix_specs(ys, ws, tm, lambda fn: fn)
                  + [pl.BlockSpec((1, d), lambda i: (0, 0)),
                     pl.BlockSpec((d, LANES), lambda i: (0, 0))]),
        out_specs=[pl.BlockSpec((tm, d), lambda i: (i, 0)),
                   pl.BlockSpec((tm, LANES), lambda i: (i, 0)),
                   pl.BlockSpec((1, SUBLANES, LANES), lambda i: (i, 0, 0))],
        out_shape=[jax.ShapeDtypeStruct((t, d), F32),
                   jax.ShapeDtypeStruct((t, LANES), F32),
                   jax.ShapeDtypeStruct((t // tm, SUBLANES, LANES), F32)],
        compiler_params=_params(1),
    )(h, *ys, *ws, g.reshape(1, d), r)


def _local_positions(route, off_row):
    lane_f = lax.broadcasted_iota(I32, route.shape, 1).astype(F32)
    lpos1 = jnp.sum(jnp.where(lane_f == route[:, 0:1], off_row, 0.0), axis=-1, keepdims=True) + route[:, 2:3]
    lpos2 = jnp.sum(jnp.where(lane_f == route[:, 1:2], off_row, 0.0), axis=-1, keepdims=True) + route[:, 3:4]
    return lpos1, lpos2


def _run_copies(n_rows, max_bit, make):
    out = []
    for k in range(max_bit, -1, -1):
        bit = k + 3
        offset = pl.multiple_of((n_rows >> (bit + 1)) << (bit + 1), SUBLANES)
        out.append((((n_rows >> bit) & 1) == 1, make(offset, SUBLANES << k)))
    return out


def _start_all(copies):
    for pred, cp in copies:
        pl.when(pred)(cp.start)


def _wait_all(copies):
    for pred, cp in copies:
        pl.when(pred)(cp.wait)


def _dispatch_kernel(seg_ref, cnt_ref, off_ref, tail_ref, h_ref, g_ref, route_ref, offrow_ref, x_hbm,
                     local, zeros, sem, zsem, *, tm, max_bit, tile_rows):
    b = pl.program_id(0)
    nb = pl.num_programs(0)
    slot = b % 2

    def block_copies(blk, s):
        out = []
        for e in range(N_EXPERTS):
            seg, off = seg_ref[blk * N_EXPERTS + e], off_ref[blk * N_EXPERTS + e]
            out += _run_copies(cnt_ref[blk * N_EXPERTS + e], max_bit, lambda o, n, seg=seg, off=off:
                               pltpu.make_async_copy(local.at[s, pl.ds(pl.multiple_of(off + o, SUBLANES), n)],
                                                     x_hbm.at[pl.ds(pl.multiple_of(seg + o, SUBLANES), n)],
                                                     sem.at[s]))
        return out

    def tail_copies():
        out = []
        for e in range(N_EXPERTS):
            start = tail_ref[e]
            out += _run_copies(tail_ref[N_EXPERTS + e], max_bit - 1, lambda o, n, start=start:
                               pltpu.make_async_copy(zeros.at[pl.ds(0, n)],
                                                     x_hbm.at[pl.ds(pl.multiple_of(start + o, SUBLANES), n)], zsem))
        return out

    @pl.when(b >= 2)
    def _():
        _wait_all(block_copies(b - 2, slot))

    t = _rms(h_ref[...], g_ref[...]).astype(BF16)
    lpos1, lpos2 = _local_positions(route_ref[...], offrow_ref[0])
    rows_f = lax.broadcasted_iota(I32, (tm, local.shape[1]), 1).astype(F32)
    place = ((rows_f == lpos1) | (rows_f == lpos2)).astype(BF16)
    local[slot] = lax.dot_general(place, t, (((0,), (0,)), ((), ())), preferred_element_type=F32)
    _start_all(block_copies(b, slot))

    @pl.when(b == nb - 1)
    def _():
        zeros[...] = jnp.zeros_like(zeros)
        _start_all(tail_copies())

        @pl.when(b >= 1)
        def _():
            _wait_all(block_copies(b - 1, 1 - slot))

        _wait_all(block_copies(b, slot))
        _wait_all(tail_copies())

        def fill_idle_tile(tile, carry):
            cp = pltpu.make_async_copy(zeros, x_hbm.at[pl.ds(pl.multiple_of(tile * tile_rows, tile_rows), tile_rows)],
                                       zsem)
            cp.start()
            cp.wait()
            return carry
        lax.fori_loop(tail_ref[2 * N_EXPERTS], x_hbm.shape[0] // tile_rows, fill_idle_tile, 0)


def moe_dispatch(seg, cnt8, off, tail, h, g, route, off_rows, n_rows, tm, local_rows, tile_rows):
    t, d = h.shape
    max_bit = (tm // SUBLANES).bit_length() - 1
    assert tile_rows >= tm // 2 and n_rows % tile_rows == 0
    kern = functools.partial(_dispatch_kernel, tm=tm, max_bit=max_bit, tile_rows=tile_rows)
    grid_spec = pltpu.PrefetchScalarGridSpec(
        num_scalar_prefetch=4,
        grid=(t // tm,),
        in_specs=[pl.BlockSpec((tm, d), lambda i, *_: (i, 0)),
                  pl.BlockSpec((1, d), lambda i, *_: (0, 0)),
                  pl.BlockSpec((tm, LANES), lambda i, *_: (i, 0)),
                  pl.BlockSpec((1, 1, LANES), lambda i, *_: (i, 0, 0))],
        out_specs=pl.BlockSpec(memory_space=pl.ANY),
        scratch_shapes=[pltpu.VMEM((2, local_rows, d), F32), pltpu.VMEM((tile_rows, d), F32),
                        pltpu.SemaphoreType.DMA((2,)), pltpu.SemaphoreType.DMA(())],
    )
    return pl.pallas_call(
        kern,
        grid_spec=grid_spec,
        out_shape=jax.ShapeDtypeStruct((n_rows, d), F32),
        compiler_params=_params(1),
    )(seg, cnt8, off, tail, h, g.reshape(1, d), route, off_rows)


def _moe_kernel(te_ref, nu_ref, x_ref, wg_ref, wu_ref, wd_ref, y_ref, xn_ref):
    i = pl.program_id(0)
    j = pl.program_id(1)
    n_used = nu_ref[0]

    @pl.when(i < n_used)
    def _():
        @pl.when(j == 0)
        def _():
            xn_ref[...] = x_ref[...].astype(BF16)

        xn = xn_ref[...]
        act = (jax.nn.silu(_dot(xn, wg_ref[0])) * _dot(xn, wu_ref[0])).astype(BF16)
        part = _dot(act, wd_ref[0])

        @pl.when(j == 0)
        def _():
            y_ref[...] = part

        @pl.when(j > 0)
        def _():
            y_ref[...] += part

    @pl.when((i >= n_used) & (j == 0))
    def _():
        y_ref[...] = jnp.zeros_like(y_ref)


def moe_experts(tile_expert, n_used, x, wg, wu, wd, tm, tf):
    n_rows, d = x.shape
    f = wg.shape[2]
    n_ff = f // tf

    def ff_idx(i, j, nu):
        return jnp.where(i < nu[0], j, n_ff - 1)

    grid_spec = pltpu.PrefetchScalarGridSpec(
        num_scalar_prefetch=2,
        grid=(n_rows // tm, n_ff),
        in_specs=[pl.BlockSpec((tm, d), lambda i, j, te, nu: (jnp.minimum(i, nu[0] - 1), 0)),
                  pl.BlockSpec((1, d, tf), lambda i, j, te, nu: (te[i], 0, ff_idx(i, j, nu))),
                  pl.BlockSpec((1, d, tf), lambda i, j, te, nu: (te[i], 0, ff_idx(i, j, nu))),
                  pl.BlockSpec((1, tf, d), lambda i, j, te, nu: (te[i], ff_idx(i, j, nu), 0))],
        out_specs=pl.BlockSpec((tm, d), lambda i, j, te, nu: (i, 0)),
        scratch_shapes=[pltpu.VMEM((tm, d), BF16)],
    )
    return pl.pallas_call(
        _moe_kernel,
        grid_spec=grid_spec,
        out_shape=jax.ShapeDtypeStruct((n_rows, d), F32),
        compiler_params=_params(2),
    )(tile_expert, n_used, x, wg, wu, wd)


def _combine_kernel(seg_ref, cnt_ref, off_ref, h_ref, route_ref, offrow_ref, y_hbm, o_ref, local, sem,
                    *, tm, max_bit):
    b = pl.program_id(0)
    nb = pl.num_programs(0)
    slot = b % 2

    def block_copies(blk, s):
        out = []
        for e in range(N_EXPERTS):
            seg, off = seg_ref[blk * N_EXPERTS + e], off_ref[blk * N_EXPERTS + e]
            out += _run_copies(cnt_ref[blk * N_EXPERTS + e], max_bit, lambda o, n, seg=seg, off=off:
                               pltpu.make_async_copy(y_hbm.at[pl.ds(pl.multiple_of(seg + o, SUBLANES), n)],
                                                     local.at[s, pl.ds(pl.multiple_of(off + o, SUBLANES), n)],
                                                     sem.at[s]))
        return out

    @pl.when(b == 0)
    def _():
        local[...] = jnp.zeros_like(local)
        _start_all(block_copies(0, 0))

    _wait_all(block_copies(b, slot))

    @pl.when(b + 1 < nb)
    def _():
        _start_all(block_copies(b + 1, 1 - slot))

    route = route_ref[...]
    lpos1, lpos2 = _local_positions(route, offrow_ref[0])
    rows_f = lax.broadcasted_iota(I32, (tm, local.shape[1]), 1).astype(F32)
    gate = jnp.where(rows_f == lpos1, route[:, 4:5], 0.0) + jnp.where(rows_f == lpos2, route[:, 5:6], 0.0)
    g_hi, g_lo = _split_bf16(gate)
    y_hi, y_lo = _split_bf16(local[slot])
    o_ref[...] = h_ref[...] + (_dot(g_hi, y_hi) + (_dot(g_lo, y_hi) + _dot(g_hi, y_lo)))


def moe_combine(seg, cnt8, off, h, route, off_rows, y, tm, local_rows):
    t, d = h.shape
    max_bit = (tm // SUBLANES).bit_length() - 1
    kern = functools.partial(_combine_kernel, tm=tm, max_bit=max_bit)
    grid_spec = pltpu.PrefetchScalarGridSpec(
        num_scalar_prefetch=3,
        grid=(t // tm,),
        in_specs=[pl.BlockSpec((tm, d), lambda i, *_: (i, 0)),
                  pl.BlockSpec((tm, LANES), lambda i, *_: (i, 0)),
                  pl.BlockSpec((1, 1, LANES), lambda i, *_: (i, 0, 0)),
                  pl.BlockSpec(memory_space=pl.ANY)],
        out_specs=pl.BlockSpec((tm, d), lambda i, *_: (i, 0)),
        scratch_shapes=[pltpu.VMEM((2, local_rows, d), F32), pltpu.SemaphoreType.DMA((2,))],
    )
    return pl.pallas_call(
        kern,
        grid_spec=grid_spec,
        out_shape=jax.ShapeDtypeStruct((t, d), F32),
        compiler_params=_params(1),
    )(seg, cnt8, off, h, route, off_rows, y)


def _rope_tables(seq):
    half = ATT_HEAD_DIM // 2
    inv_freq = ROPE_THETA ** (-jnp.arange(half, dtype=F32) / half)
    ang = jnp.arange(seq, dtype=jnp.int32).astype(F32)[:, None] * inv_freq[None, :]
    reps = LANES // half
    cos = jnp.tile(jnp.cos(ang), (1, reps))
    sign = jnp.tile(jnp.concatenate([-jnp.ones((half,), F32), jnp.ones((half,), F32)]), LANES // ATT_HEAD_DIM)
    sin = jnp.tile(jnp.sin(ang), (1, reps)) * sign[None, :]
    return cos, sin


def _even_layer(h, bsz, seq, ln_mix, ln_ffn, w_in, conv_w, conv_b, w_a, b_a, w_i, b_i, lam,
                q_norm, k_norm, sinks, w_out, f_gate, f_up, f_down):
    lru_w = conv_w.shape[1]
    q_heads = sinks.shape[0]
    q_w = q_heads * ATT_HEAD_DIM
    kv_w = (w_in.shape[1] - 2 * lru_w - q_w) // 2
    kv_heads = kv_w // ATT_HEAD_DIM

    proj = rms_matmul(h, ln_mix, w_in.astype(BF16), tm=512, tn=w_in.shape[1])
    gates_w = jnp.concatenate([block_diag(*w_a), block_diag(*w_i)], axis=1).astype(BF16)
    y_lru = rglru(proj, conv_w, conv_b, gates_w, jnp.concatenate([b_a, b_i]), lam, bsz, seq, lru_w, tm=256)
    cos, sin = _rope_tables(seq)
    tile2 = lambda g: jnp.tile(g, LANES // ATT_HEAD_DIM).reshape(1, LANES)
    y_att = swa(proj, sinks, cos, sin, tile2(q_norm), tile2(k_norm), bsz, seq,
                q_col=2 * lru_w, k_col=2 * lru_w + q_w, v_col=2 * lru_w + q_w + kv_w,
                q_heads=q_heads, kv_heads=kv_heads)
    w_out = w_out.astype(BF16)
    return ffn(h, [y_lru, y_att], [w_out[:lru_w], w_out[lru_w:]], ln_ffn, f_gate.astype(BF16), f_up.astype(BF16),
               f_down.astype(BF16), tm=512, tf=1408)


def _odd_layer(h, bsz, seq, ln_mix, ln_ffn, w_in, conv_w, a_log, dt_bias, out_norm, w_out, router_w,
               m_gate, m_up, m_down):
    t, d = h.shape
    heads = a_log.shape[0]
    width = heads * GDN_HEAD_DIM
    cols = w_in.shape[1]
    pad = (-cols) % (11 * LANES)
    w_in_p = jnp.pad(w_in, ((0, 0), (0, pad))).astype(BF16)
    proj = rms_matmul(h, ln_mix, w_in_p, tm=512, tn=w_in_p.shape[1])
    lane_pad = lambda v: jnp.pad(v, (0, LANES - heads)).reshape(1, LANES)
    y = gdn(proj, conv_w, lane_pad(a_log), lane_pad(dt_bias), out_norm, bsz, seq, heads, tc=256)
    tb = MOE_BLOCK
    tm = MOE_TILE
    nblk = t // tb
    h, route, cnt = router(h, [y], [w_out.astype(BF16)], ln_ffn,
                           jnp.pad(router_w, ((0, 0), (0, LANES - N_EXPERTS))), tm=tb)
    cnt8 = (cnt[:, 0, :N_EXPERTS].astype(I32) + SUBLANES - 1) // SUBLANES * SUBLANES
    total8 = jnp.sum(cnt8, axis=0)
    tiles_per = (total8 + tm - 1) // tm
    tile_end = jnp.cumsum(tiles_per)
    starts = (tile_end - tiles_per) * tm
    n_used = tile_end[-1:]
    n_tiles = (2 * t + nblk * N_EXPERTS * (SUBLANES - 1)) // tm + N_EXPERTS
    tile_ids = jnp.minimum(jnp.arange(n_tiles, dtype=I32), n_used - 1)
    tile_expert = jnp.sum((tile_ids[:, None] >= tile_end[None, :]).astype(I32), axis=1)
    seg = (starts[None, :] + jnp.cumsum(cnt8, axis=0) - cnt8).reshape(-1)
    off = jnp.cumsum(cnt8, axis=1) - cnt8
    off_rows = jnp.pad(off.astype(F32), ((0, 0), (0, LANES - N_EXPERTS))).reshape(nblk, 1, LANES)
    tail = jnp.concatenate([starts + total8, tiles_per * tm - total8, n_used])
    local_rows = -(-(2 * tb + N_EXPERTS * (SUBLANES - 1)) // LANES) * LANES
    cnt8, off = cnt8.reshape(-1), off.reshape(-1)

    x = moe_dispatch(seg, cnt8, off, tail, h, ln_ffn, route, off_rows, n_tiles * tm, tb, local_rows, tm)
    y = moe_experts(tile_expert, n_used, x, m_gate.astype(BF16), m_up.astype(BF16), m_down.astype(BF16),
                    tm, MOE_FF_CHUNK)
    return moe_combine(seg, cnt8, off, h, route, off_rows, y, tb, local_rows)


def kernel(x, ln_mix, ln_ffn, e_w_in, e_lru_conv_w, e_lru_conv_b, e_lru_w_a, e_lru_b_a, e_lru_w_i, e_lru_b_i, e_lru_lambda, e_q_norm, e_k_norm, e_sinks, e_w_out, e_ffn_w_gate, e_ffn_w_up, e_ffn_w_down, o_w_in, o_conv_w, o_a_log, o_dt_bias, o_out_norm, o_w_out, o_router, o_moe_w_gate, o_moe_w_up, o_moe_w_down):
    bsz, seq, d = x.shape
    h = x.reshape(bsz * seq, d)
    for layer in range(ln_mix.shape[0]):
        j = layer // 2
        if layer % 2 == 0:
            h = _even_layer(h, bsz, seq, ln_mix[layer], ln_ffn[layer], e_w_in[j], e_lru_conv_w[j],
                            e_lru_conv_b[j], e_lru_w_a[j], e_lru_b_a[j], e_lru_w_i[j], e_lru_b_i[j],
                            e_lru_lambda[j], e_q_norm[j], e_k_norm[j], e_sinks[j], e_w_out[j],
                            e_ffn_w_gate[j], e_ffn_w_up[j], e_ffn_w_down[j])
        else:
            h = _odd_layer(h, bsz, seq, ln_mix[layer], ln_ffn[layer], o_w_in[j], o_conv_w[j], o_a_log[j],
                           o_dt_bias[j], o_out_norm[j], o_w_out[j], o_router[j], o_moe_w_gate[j],
                           o_moe_w_up[j], o_moe_w_down[j])
    return h.reshape(bsz, seq, d)
```

```python
import functools

import jax
import jax.numpy as jnp
from jax import lax
from jax.scipy.linalg import block_diag
from jax.experimental import pallas as pl
from jax.experimental.pallas import tpu as pltpu

F32 = jnp.float32
BF16 = jnp.bfloat16
I32 = jnp.int32

EPS = 1e-6
LANES = 128
SUBLANES = 8
MXU_WIDTH = 256
VMEM_LIMIT = 52 * 1024 * 1024

CONV_WIDTH = 4
LRU_C = 8.0
ATT_HEAD_DIM = 64
ATT_WINDOW = 128
ROPE_THETA = 10000.0
GDN_HEAD_DIM = 128
GDN_CHUNK = 64
GDN_PREP_CHUNKS = 2
N_EXPERTS = 8
MOE_TILE = 512
MOE_BLOCK = 512
MOE_FF_CHUNK = 1792


def _params(n_axes, vmem=VMEM_LIMIT):
    return pltpu.CompilerParams(dimension_semantics=("arbitrary",) * n_axes, vmem_limit_bytes=vmem)


def _rms(x, g):
    return x * lax.rsqrt(jnp.mean(x * x, axis=-1, keepdims=True) + EPS) * g


def _sigmoid(x):
    return 0.5 * (1.0 + jnp.tanh(0.5 * x))


def _silu(x):
    return x * _sigmoid(x)


def _dot(a, b):
    return jnp.dot(a, b, preferred_element_type=F32)


def _dot_nt(a, b):
    return lax.dot_general(a, b, (((1,), (1,)), ((), ())), preferred_element_type=F32)


def _split_bf16(x):
    hi = x.astype(BF16)
    lo = (x - hi.astype(F32)).astype(BF16)
    return hi, lo


def _rms_matmul_kernel(x_ref, g_ref, w_ref, o_ref, xn_ref):
    @pl.when(pl.program_id(1) == 0)
    def _():
        xn_ref[...] = _rms(x_ref[...], g_ref[...]).astype(BF16)

    o_ref[...] = _dot(xn_ref[...], w_ref[...])


def rms_matmul(x, g, w, tm, tn):
    t, d = x.shape
    n = w.shape[1]
    return pl.pallas_call(
        _rms_matmul_kernel,
        grid=(t // tm, n // tn),
        in_specs=[pl.BlockSpec((tm, d), lambda i, j: (i, 0)),
                  pl.BlockSpec((1, d), lambda i, j: (0, 0)),
                  pl.BlockSpec((d, tn), lambda i, j: (0, j))],
        out_specs=pl.BlockSpec((tm, tn), lambda i, j: (i, j)),
        out_shape=jax.ShapeDtypeStruct((t, n), F32),
        scratch_shapes=[pltpu.VMEM((tm, d), BF16)],
        compiler_params=_params(2),
    )(x, g.reshape(1, d), w)


def _causal_conv(tail_ref, x, cw, tm):
    prev = tail_ref[...]
    row = lax.broadcasted_iota(I32, prev.shape, 0)
    acc = None
    for j in range(CONV_WIDTH - 1):
        s = CONV_WIDTH - 1 - j
        xs = pltpu.roll(x, s, 0)
        head = jnp.where(row < s, pltpu.roll(prev, s, 0), xs[0:SUBLANES, :])
        term = cw[j:j + 1, :] * jnp.concatenate([head, xs[SUBLANES:, :]], axis=0)
        acc = term if acc is None else acc + term
    tail_ref[...] = x[tm - SUBLANES:tm, :]
    return acc + cw[CONV_WIDTH - 1:CONV_WIDTH, :] * x


def _rglru_kernel(xb_ref, gate_ref, cw_ref, cb_ref, wg_ref, bg_ref, lam_ref, o_ref, tail_ref, h_ref,
                  *, tm, width):
    @pl.when(pl.program_id(1) == 0)
    def _():
        tail_ref[...] = jnp.zeros_like(tail_ref)
        h_ref[...] = jnp.zeros_like(h_ref)

    xc = _causal_conv(tail_ref, xb_ref[...], cw_ref[...], tm) + cb_ref[...]
    z = _dot(xc.astype(BF16), wg_ref[...]) + bg_ref[...]
    r = _sigmoid(z[:, :width])
    gi = _sigmoid(z[:, width:])
    log_a = -LRU_C * r * jax.nn.softplus(-lam_ref[...])
    a = jnp.exp(log_a)
    u = jnp.sqrt(1.0 - a * a) * (gi * xc)

    row = lax.broadcasted_iota(I32, (tm, width), 0) % SUBLANES
    d = 1
    while d < SUBLANES:
        keep = row >= d
        u = u + jnp.where(keep, a * pltpu.roll(u, d, 0), 0.0)
        a = jnp.where(keep, a * pltpu.roll(a, d, 0), a)
        d *= 2
    carry = h_ref[0:1, :]
    groups = []
    for r0 in range(0, tm, SUBLANES):
        hg = u[r0:r0 + SUBLANES, :] + a[r0:r0 + SUBLANES, :] * carry
        carry = hg[SUBLANES - 1:SUBLANES, :]
        groups.append(hg)
    h = jnp.concatenate(groups, axis=0)
    h_ref[...] = jnp.broadcast_to(carry, h_ref.shape)
    o_ref[...] = (h * jax.nn.gelu(gate_ref[...])).astype(o_ref.dtype)


def rglru(proj, cw, cb, wg, bg, lam, bsz, seq, width, tm):
    nt = seq // tm
    kern = functools.partial(_rglru_kernel, tm=tm, width=width)
    row = lambda c: pl.BlockSpec((1, c), lambda b, i: (0, 0))
    return pl.pallas_call(
        kern,
        grid=(bsz, nt),
        in_specs=[pl.BlockSpec((tm, width), lambda b, i: (b * nt + i, 0)),
                  pl.BlockSpec((tm, width), lambda b, i: (b * nt + i, 1)),
                  pl.BlockSpec((CONV_WIDTH, width), lambda b, i: (0, 0)),
                  row(width),
                  pl.BlockSpec((width, 2 * width), lambda b, i: (0, 0)),
                  row(2 * width),
                  row(width)],
        out_specs=pl.BlockSpec((tm, width), lambda b, i: (b * nt + i, 0)),
        out_shape=jax.ShapeDtypeStruct((bsz * seq, width), BF16),
        scratch_shapes=[pltpu.VMEM((SUBLANES, width), F32), pltpu.VMEM((SUBLANES, width), F32)],
        compiler_params=_params(2),
    )(proj, proj, cw, cb.reshape(1, -1), wg, bg.reshape(1, -1), lam.reshape(1, -1))


def _swa_kernel(sink_ref, q_ref, k_ref, v_ref, cos_ref, sin_ref, qg_ref, kg_ref, o_ref, kall_ref, vall_ref,
                *, q_heads, kv_heads):
    w = ATT_WINDOW
    n = pl.program_id(1)
    group = q_heads // kv_heads
    assert kv_heads * ATT_HEAD_DIM == LANES and group % 2 == 0

    @pl.when(n == 0)
    def _():
        kall_ref[0:w, :] = jnp.zeros((w, LANES), BF16)
        vall_ref[0:w, :] = jnp.zeros((w, LANES), BF16)

    @pl.when(n > 0)
    def _():
        kall_ref[0:w, :] = kall_ref[w:2 * w, :]
        vall_ref[0:w, :] = vall_ref[w:2 * w, :]

    lane = lax.broadcasted_iota(I32, (w, LANES), 1)
    first_half = (lane % ATT_HEAD_DIM) < (ATT_HEAD_DIM // 2)
    lane_head = lane // ATT_HEAD_DIM
    bd_r = lax.broadcasted_iota(I32, (LANES, LANES), 0) // ATT_HEAD_DIM
    bd_c = lax.broadcasted_iota(I32, (LANES, LANES), 1) // ATT_HEAD_DIM
    head_ones = (bd_r == bd_c).astype(BF16)
    cos = cos_ref[...]
    sin = sin_ref[...]

    def norm_rope(x, g):
        hi, lo = _split_bf16(x * x)
        ss = _dot(hi, head_ones) + _dot(lo, head_ones)
        xn = x * lax.rsqrt(ss * (1.0 / ATT_HEAD_DIM) + EPS) * g
        rot = jnp.where(first_half, pltpu.roll(xn, LANES - ATT_HEAD_DIM // 2, 1),
                        pltpu.roll(xn, ATT_HEAD_DIM // 2, 1))
        return xn * cos + rot * sin

    kall_ref[w:2 * w, :] = norm_rope(k_ref[...], kg_ref[...]).astype(BF16)
    vall_ref[w:2 * w, :] = v_ref[...].astype(BF16)
    qc = [norm_rope(q_ref[:, LANES * c:LANES * (c + 1)], qg_ref[...]) for c in range(q_heads // 2)]

    rows = group * w
    qi = lax.broadcasted_iota(I32, (rows, 2 * w), 0) % w
    kj = lax.broadcasted_iota(I32, (rows, 2 * w), 1)
    valid = (kj > qi) & (kj <= qi + w) & ((n > 0) | (kj >= w))
    row_head = lax.broadcasted_iota(I32, (rows, 1), 0) // w

    outs = []
    for g in range(kv_heads):
        parts = []
        for hh in range(group):
            h = g * group + hh
            x = qc[h // 2]
            if h % 2 != g:
                x = pltpu.roll(x, ATT_HEAD_DIM, 1)
            parts.append(jnp.where(lane_head == g, x, 0.0))
        qs = jnp.concatenate(parts, axis=0).astype(BF16)
        s = _dot_nt(qs, kall_ref[...]) * (ATT_HEAD_DIM ** -0.5)
        s = jnp.where(valid, s, -1e30)
        sink = jnp.zeros((rows, 1), F32)
        for hh in range(group):
            sink = jnp.where(row_head == hh, sink_ref[g * group + hh], sink)
        m = jnp.maximum(jnp.max(s, axis=-1, keepdims=True), sink)
        p = jnp.exp(s - m)
        denom = jnp.sum(p, axis=-1, keepdims=True) + jnp.exp(sink - m)
        pv = _dot(p.astype(BF16), vall_ref[...]) / denom
        outs.extend(pv[w * hh:w * (hh + 1), :] for hh in range(group))

    for c in range(q_heads // 2):
        g = (2 * c) // group
        a, b = outs[2 * c], outs[2 * c + 1]
        if g == 1:
            a = pltpu.roll(a, ATT_HEAD_DIM, 1)
        else:
            b = pltpu.roll(b, ATT_HEAD_DIM, 1)
        o_ref[:, LANES * c:LANES * (c + 1)] = jnp.where(lane_head == 0, a, b).astype(o_ref.dtype)


def swa(proj, sinks, cos, sin, qg, kg, bsz, seq, q_col, k_col, v_col, q_heads, kv_heads):
    w = ATT_WINDOW
    nb = seq // w
    qw = q_heads * ATT_HEAD_DIM
    kern = functools.partial(_swa_kernel, q_heads=q_heads, kv_heads=kv_heads)
    return pl.pallas_call(
        kern,
        grid=(bsz, nb),
        in_specs=[pl.BlockSpec(memory_space=pltpu.SMEM),
                  pl.BlockSpec((w, qw), lambda b, i: (b * nb + i, q_col // qw)),
                  pl.BlockSpec((w, LANES), lambda b, i: (b * nb + i, k_col // LANES)),
                  pl.BlockSpec((w, LANES), lambda b, i: (b * nb + i, v_col // LANES)),
                  pl.BlockSpec((w, LANES), lambda b, i: (i, 0)),
                  pl.BlockSpec((w, LANES), lambda b, i: (i, 0)),
                  pl.BlockSpec((1, LANES), lambda b, i: (0, 0)),
                  pl.BlockSpec((1, LANES), lambda b, i: (0, 0))],
        out_specs=pl.BlockSpec((w, qw), lambda b, i: (b * nb + i, 0)),
        out_shape=jax.ShapeDtypeStruct((bsz * seq, qw), BF16),
        scratch_shapes=[pltpu.VMEM((2 * w, LANES), BF16), pltpu.VMEM((2 * w, LANES), BF16)],
        compiler_params=_params(2),
    )(sinks, proj, proj, proj, cos, sin, qg, kg)


def _mix_specs(ys, ws, tm, index_map):
    return ([pl.BlockSpec((tm, y.shape[1]), index_map(lambda i: (i, 0))) for y in ys]
            + [pl.BlockSpec(w.shape, index_map(lambda i: (0, 0))) for w in ws])


def _mix_residual(h_ref, mix_refs):
    n_in = len(mix_refs) // 2
    x = h_ref[...]
    for y_ref, w_ref in zip(mix_refs[:n_in], mix_refs[n_in:]):
        x = x + _dot(y_ref[...], w_ref[...])
    return x


def _ffn_kernel(h_ref, *refs):
    *mix_refs, g_ref, wg_ref, wu_ref, wd_ref, o_ref, xn_ref = refs
    j = pl.program_id(1)

    @pl.when(j == 0)
    def _():
        x = _mix_residual(h_ref, mix_refs)
        xn_ref[...] = _rms(x, g_ref[...]).astype(BF16)
        o_ref[...] = x

    xn = xn_ref[...]
    act = (_silu(_dot(xn, wg_ref[...])) * _dot(xn, wu_ref[...])).astype(BF16)
    o_ref[...] += _dot(act, wd_ref[...])


def ffn(h, ys, ws, g, wg, wu, wd, tm, tf):
    t, d = h.shape
    f = wg.shape[1]
    return pl.pallas_call(
        _ffn_kernel,
        grid=(t // tm, f // tf),
        in_specs=([pl.BlockSpec((tm, d), lambda i, j: (i, 0))]
                  + _mix_specs(ys, ws, tm, lambda fn: lambda i, j: fn(i))
                  + [pl.BlockSpec((1, d), lambda i, j: (0, 0)),
                     pl.BlockSpec((d, tf), lambda i, j: (0, j)),
                     pl.BlockSpec((d, tf), lambda i, j: (0, j)),
                     pl.BlockSpec((tf, d), lambda i, j: (j, 0))]),
        out_specs=pl.BlockSpec((tm, d), lambda i, j: (i, 0)),
        out_shape=jax.ShapeDtypeStruct((t, d), F32),
        scratch_shapes=[pltpu.VMEM((tm, d), BF16)],
        compiler_params=_params(2),
    )(h, *ys, *ws, g.reshape(1, d), wg, wu, wd)


def _gdn_kernel(qkv_ref, gate_ref, ab_ref, cw_ref, alog_ref, dtb_ref, onorm_ref, o_ref,
                tail_ref, q_ref, k_ref, v_ref, gc_ref, beta_ref, uw_ref, aqk_ref, qd_ref, kd_ref, *state_refs,
                tc, heads):
    hd = GDN_HEAD_DIM
    c = GDN_CHUNK
    width = heads * hd

    @pl.when(pl.program_id(1) == 0)
    def _():
        tail_ref[...] = jnp.zeros_like(tail_ref)
        for s_ref in state_refs:
            s_ref[...] = jnp.zeros_like(s_ref)

    qkv = _silu(_causal_conv(tail_ref, qkv_ref[...], cw_ref[...], tc))
    for h in range(heads):
        q = qkv[:, hd * h:hd * (h + 1)]
        k = qkv[:, width + hd * h:width + hd * (h + 1)]
        q_ref[:, hd * h:hd * (h + 1)] = (q * lax.rsqrt(jnp.sum(q * q, axis=-1, keepdims=True) + EPS)
                                          * (hd ** -0.5))
        k_ref[:, hd * h:hd * (h + 1)] = k * lax.rsqrt(jnp.sum(k * k, axis=-1, keepdims=True) + EPS)
    v_ref[...] = qkv[:, 2 * width:]

    ab = ab_ref[...]
    g = -jnp.exp(alog_ref[...]) * jax.nn.softplus(ab + dtb_ref[...])
    beta_ref[...] = pltpu.roll(jax.nn.sigmoid(ab), LANES - heads, 1)
    row = lax.broadcasted_iota(I32, (tc, LANES), 0) % c
    d = 1
    while d < c:
        g = g + jnp.where(row >= d, pltpu.roll(g, d, 0), 0.0)
        d *= 2
    gc_ref[...] = g

    ri = lax.broadcasted_iota(I32, (c, c), 0)
    ci = lax.broadcasted_iota(I32, (c, c), 1)
    tri = ri >= ci
    strict = ri > ci
    eye = (ri == ci).astype(F32)

    hs = range(heads)
    cols = [slice(hd * h, hd * (h + 1)) for h in hs]
    n_chunks = tc // c
    rows = [slice(c * n, c * (n + 1)) for n in range(n_chunks)]

    for g0 in range(0, n_chunks, GDN_PREP_CHUNKS):
        items = [(n, h) for n in range(g0, min(g0 + GDN_PREP_CHUNKS, n_chunks)) for h in hs]
        gc_all = {n: gc_ref[rows[n], :] for n, _ in items}
        gr_all = {n: gc_all[n].T for n in gc_all}
        beta_all = {n: beta_ref[rows[n], :] for n in gc_all}
        gcol = {(n, h): gc_all[n][:, h:h + 1] for n, h in items}
        grow = {(n, h): gr_all[n][h:h + 1, :] for n, h in items}
        beta = {(n, h): beta_all[n][:, h:h + 1] for n, h in items}
        qk_kk = {(n, h): _dot_nt(jnp.concatenate([q_ref[rows[n], cols[h]], k_ref[rows[n], cols[h]]],
                                                 axis=0).astype(BF16), k_ref[rows[n], cols[h]].astype(BF16))
                 for n, h in items}
        decay = {it: jnp.where(tri, jnp.exp(jnp.where(tri, gcol[it] - grow[it], 0.0)), 0.0) for it in items}
        m = {it: -jnp.where(strict, beta[it] * qk_kk[it][c:, :] * decay[it], 0.0) for it in items}
        p = {it: eye + m[it] for it in items}
        e = {it: _dot(m[it].astype(BF16), m[it].astype(BF16)) for it in items}
        lvl = 2
        while 2 * lvl < c:
            r = {it: _dot(jnp.concatenate([p[it], e[it]], axis=0).astype(BF16), e[it].astype(BF16)) for it in items}
            p = {it: p[it] + r[it][:c, :] for it in items}
            e = {it: r[it][c:, :] for it in items}
            lvl *= 2
        p = {it: p[it] + _dot(p[it].astype(BF16), e[it].astype(BF16)) for it in items}
        for n, h in items:
            it = (n, h)
            k = k_ref[rows[n], cols[h]]
            eg = jnp.exp(gcol[it])
            rhs = jnp.concatenate([beta[it] * v_ref[rows[n], cols[h]], (beta[it] * eg) * k], axis=1)
            uw_ref[rows[n], 2 * hd * h:2 * hd * (h + 1)] = _dot(p[it].astype(BF16), rhs.astype(BF16))
            aqk_ref[rows[n], LANES * h:LANES * h + c] = jnp.where(tri, qk_kk[it][:c, :] * decay[it], 0.0)
            qd_ref[rows[n], cols[h]] = q_ref[rows[n], cols[h]] * eg
            kd_ref[rows[n], cols[h]] = k * jnp.exp(gcol[it][c - 1:c, :] - gcol[it])

    for n in range(n_chunks):
        gc_n = gc_ref[rows[n], :]
        g_last = [jnp.exp(gc_n[c - 1:c, h:h + 1]) for h in hs]
        s = [state_refs[h][...] for h in hs]
        ws_qs = [_dot(jnp.concatenate([uw_ref[rows[n], 2 * hd * h + hd:2 * hd * (h + 1)], qd_ref[rows[n], cols[h]]],
                                      axis=0).astype(BF16), s[h].astype(BF16)) for h in hs]
        vb = [(uw_ref[rows[n], 2 * hd * h:2 * hd * h + hd] - ws_qs[h][:c, :]).astype(BF16) for h in hs]
        o = [ws_qs[h][c:, :] + _dot(aqk_ref[rows[n], LANES * h:LANES * h + c].astype(BF16), vb[h]) for h in hs]
        s_new = [s[h] * g_last[h] + _dot(kd_ref[rows[n], cols[h]].T.astype(BF16), vb[h]) for h in hs]
        for h in hs:
            state_refs[h][...] = s_new[h]
            on = o[h] * lax.rsqrt(jnp.mean(o[h] * o[h], axis=-1, keepdims=True) + EPS) * onorm_ref[...]
            o_ref[rows[n], cols[h]] = (on * _silu(gate_ref[rows[n], cols[h]])).astype(o_ref.dtype)


def gdn(proj, cw, alog, dtb, onorm, bsz, seq, heads, tc):
    nt = seq // tc
    width = heads * GDN_HEAD_DIM
    kern = functools.partial(_gdn_kernel, tc=tc, heads=heads)
    row = lambda n: pl.BlockSpec((1, n), lambda b, i: (0, 0))
    return pl.pallas_call(
        kern,
        grid=(bsz, nt),
        in_specs=[pl.BlockSpec((tc, 3 * width), lambda b, i: (b * nt + i, 0)),
                  pl.BlockSpec((tc, width), lambda b, i: (b * nt + i, 3)),
                  pl.BlockSpec((tc, LANES), lambda b, i: (b * nt + i, 4 * width // LANES)),
                  pl.BlockSpec((CONV_WIDTH, 3 * width), lambda b, i: (0, 0)),
                  row(LANES), row(LANES), row(GDN_HEAD_DIM)],
        out_specs=pl.BlockSpec((tc, width), lambda b, i: (b * nt + i, 0)),
        out_shape=jax.ShapeDtypeStruct((bsz * seq, width), BF16),
        scratch_shapes=[pltpu.VMEM((SUBLANES, 3 * width), F32),
                        pltpu.VMEM((tc, width), F32), pltpu.VMEM((tc, width), F32),
                        pltpu.VMEM((tc, width), F32),
                        pltpu.VMEM((tc, LANES), F32), pltpu.VMEM((tc, LANES), F32),
                        pltpu.VMEM((tc, 2 * width), F32), pltpu.VMEM((tc, heads * LANES), F32),
                        pltpu.VMEM((tc, width), F32), pltpu.VMEM((tc, width), F32),
                        *[pltpu.VMEM((GDN_HEAD_DIM, GDN_HEAD_DIM), F32) for _ in range(heads)]],
        compiler_params=_params(2),
    )(proj, proj, proj, cw, alog, dtb, onorm.reshape(1, -1))


def _router_kernel(h_ref, *refs, tm):
    *mix_refs, g_ref, r_ref, x_ref, route_ref, cnt_ref = refs
    x = _mix_residual(h_ref, mix_refs)
    x_ref[...] = x
    t_hi, t_lo = _split_bf16(_rms(x, g_ref[...]))
    r_hi, r_lo = _split_bf16(r_ref[...])
    logits = _dot(t_hi, r_hi) + (_dot(t_lo, r_hi) + _dot(t_hi, r_lo))
    lane = lax.broadcasted_iota(I32, (tm, LANES), 1)
    lane_f = lane.astype(F32)
    neg = -jnp.inf
    lg = jnp.where(lane < N_EXPERTS, logits, neg)
    m1 = jnp.max(lg, axis=-1, keepdims=True)
    i1 = jnp.min(jnp.where(lg == m1, lane_f, float(LANES)), axis=-1, keepdims=True)
    oh1 = lane_f == i1
    lg2 = jnp.where(oh1, neg, lg)
    m2 = jnp.max(lg2, axis=-1, keepdims=True)
    i2 = jnp.min(jnp.where(lg2 == m2, lane_f, float(LANES)), axis=-1, keepdims=True)
    oh2 = lane_f == i2
    e2 = jnp.exp(m2 - m1)
    w1 = 1.0 / (1.0 + e2)
    w2 = e2 / (1.0 + e2)

    both = (oh1 | oh2).astype(BF16)
    ri = lax.broadcasted_iota(I32, (tm, tm), 0)
    ci = lax.broadcasted_iota(I32, (tm, tm), 1)
    pos = _dot((ri > ci).astype(BF16), both)
    rank1 = jnp.sum(jnp.where(oh1, pos, 0.0), axis=-1, keepdims=True)
    rank2 = jnp.sum(jnp.where(oh2, pos, 0.0), axis=-1, keepdims=True)
    cnt_ref[0] = jnp.broadcast_to(jnp.sum(both.astype(F32), axis=0, keepdims=True), (SUBLANES, LANES))

    out = jnp.where(lane == 0, i1, 0.0)
    out = jnp.where(lane == 1, i2, out)
    out = jnp.where(lane == 2, rank1, out)
    out = jnp.where(lane == 3, rank2, out)
    out = jnp.where(lane == 4, w1, out)
    route_ref[...] = jnp.where(lane == 5, w2, out)


def router(h, ys, ws, g, r, tm):
    t, d = h.shape
    kern = functools.partial(_router_kernel, tm=tm)
    return pl.pallas_call(
        kern,
        grid=(t // tm,),
        in_specs=([pl.BlockSpec((tm, d), lambda i: (i, 0))]
                  + _mix_specs(ys, ws, tm, lambda fn: fn)
                  + [pl.BlockSpec((1, d), lambda i: (0, 0)),
                     pl.BlockSpec((d, LANES), lambda i: (0, 0))]),
        out_specs=[pl.BlockSpec((tm, d), lambda i: (i, 0)),
                   pl.BlockSpec((tm, LANES), lambda i: (i, 0)),
                   pl.BlockSpec((1, SUBLANES, LANES), lambda i: (i, 0, 0))],
        out_shape=[jax.ShapeDtypeStruct((t, d), F32),
                   jax.ShapeDtypeStruct((t, LANES), F32),
                   jax.ShapeDtypeStruct((t // tm, SUBLANES, LANES), F32)],
        compiler_params=_params(1),
    )(h, *ys, *ws, g.reshape(1, d), r)


def _local_positions(route, off_row):
    lane_f = lax.broadcasted_iota(I32, route.shape, 1).astype(F32)
    lpos1 = jnp.sum(jnp.where(lane_f == route[:, 0:1], off_row, 0.0), axis=-1, keepdims=True) + route[:, 2:3]
    lpos2 = jnp.sum(jnp.where(lane_f == route[:, 1:2], off_row, 0.0), axis=-1, keepdims=True) + route[:, 3:4]
    return lpos1, lpos2


def _run_copies(n_rows, max_bit, make):
    out = []
    for k in range(max_bit, -1, -1):
        bit = k + 3
        offset = pl.multiple_of((n_rows >> (bit + 1)) << (bit + 1), SUBLANES)
        out.append((((n_rows >> bit) & 1) == 1, make(offset, SUBLANES << k)))
    return out


def _start_all(copies):
    for pred, cp in copies:
        pl.when(pred)(cp.start)


def _wait_all(copies):
    for pred, cp in copies:
        pl.when(pred)(cp.wait)


def _dispatch_kernel(seg_ref, cnt_ref, off_ref, tail_ref, h_ref, g_ref, route_ref, offrow_ref, x_hbm,
                     local, zeros, sem, zsem, *, tm, max_bit, tile_rows):
    b = pl.program_id(0)
    nb = pl.num_programs(0)
    slot = b % 2

    def block_copies(blk, s):
        out = []
        for e in range(N_EXPERTS):
            seg, off = seg_ref[blk * N_EXPERTS + e], off_ref[blk * N_EXPERTS + e]
            out += _run_copies(cnt_ref[blk * N_EXPERTS + e], max_bit, lambda o, n, seg=seg, off=off:
                               pltpu.make_async_copy(local.at[s, pl.ds(pl.multiple_of(off + o, SUBLANES), n)],
                                                     x_hbm.at[pl.ds(pl.multiple_of(seg + o, SUBLANES), n)],
                                                     sem.at[s]))
        return out

    def tail_copies():
        out = []
        for e in range(N_EXPERTS):
            start = tail_ref[e]
            out += _run_copies(tail_ref[N_EXPERTS + e], max_bit - 1, lambda o, n, start=start:
                               pltpu.make_async_copy(zeros.at[pl.ds(0, n)],
                                                     x_hbm.at[pl.ds(pl.multiple_of(start + o, SUBLANES), n)], zsem))
        return out

    @pl.when(b >= 2)
    def _():
        _wait_all(block_copies(b - 2, slot))

    t = _rms(h_ref[...], g_ref[...]).astype(BF16)
    lpos1, lpos2 = _local_positions(route_ref[...], offrow_ref[0])
    rows_f = lax.broadcasted_iota(I32, (tm, local.shape[1]), 1).astype(F32)
    place = ((rows_f == lpos1) | (rows_f == lpos2)).astype(BF16)
    local[slot] = lax.dot_general(place, t, (((0,), (0,)), ((), ())), preferred_element_type=F32)
    _start_all(block_copies(b, slot))

    @pl.when(b == nb - 1)
    def _():
        zeros[...] = jnp.zeros_like(zeros)
        _start_all(tail_copies())

        @pl.when(b >= 1)
        def _():
            _wait_all(block_copies(b - 1, 1 - slot))

        _wait_all(block_copies(b, slot))
        _wait_all(tail_copies())

        def fill_idle_tile(tile, carry):
            cp = pltpu.make_async_copy(zeros, x_hbm.at[pl.ds(pl.multiple_of(tile * tile_rows, tile_rows), tile_rows)],
                                       zsem)
            cp.start()
            cp.wait()
            return carry
        lax.fori_loop(tail_ref[2 * N_EXPERTS], x_hbm.shape[0] // tile_rows, fill_idle_tile, 0)


def moe_dispatch(seg, cnt8, off, tail, h, g, route, off_rows, n_rows, tm, local_rows, tile_rows):
    t, d = h.shape
    max_bit = (tm // SUBLANES).bit_length() - 1
    assert tile_rows >= tm // 2 and n_rows % tile_rows == 0
    kern = functools.partial(_dispatch_kernel, tm=tm, max_bit=max_bit, tile_rows=tile_rows)
    grid_spec = pltpu.PrefetchScalarGridSpec(
        num_scalar_prefetch=4,
        grid=(t // tm,),
        in_specs=[pl.BlockSpec((tm, d), lambda i, *_: (i, 0)),
                  pl.BlockSpec((1, d), lambda i, *_: (0, 0)),
                  pl.BlockSpec((tm, LANES), lambda i, *_: (i, 0)),
                  pl.BlockSpec((1, 1, LANES), lambda i, *_: (i, 0, 0))],
        out_specs=pl.BlockSpec(memory_space=pl.ANY),
        scratch_shapes=[pltpu.VMEM((2, local_rows, d), F32), pltpu.VMEM((tile_rows, d), F32),
                        pltpu.SemaphoreType.DMA((2,)), pltpu.SemaphoreType.DMA(())],
    )
    return pl.pallas_call(
        kern,
        grid_spec=grid_spec,
        out_shape=jax.ShapeDtypeStruct((n_rows, d), F32),
        compiler_params=_params(1),
    )(seg, cnt8, off, tail, h, g.reshape(1, d), route, off_rows)


def _moe_kernel(te_ref, nu_ref, x_ref, wg_ref, wu_ref, wd_ref, y_ref, xn_ref):
    i = pl.program_id(0)
    j = pl.program_id(1)
    n_used = nu_ref[0]

    @pl.when(i < n_used)
    def _():
        @pl.when(j == 0)
        def _():
            xn_ref[...] = x_ref[...].astype(BF16)

        xn = xn_ref[...]
        act = (_silu(_dot(xn, wg_ref[0])) * _dot(xn, wu_ref[0])).astype(BF16)
        part = _dot(act, wd_ref[0])

        @pl.when(j == 0)
        def _():
            y_ref[...] = part

        @pl.when(j > 0)
        def _():
            y_ref[...] += part

    @pl.when((i >= n_used) & (j == 0))
    def _():
        y_ref[...] = jnp.zeros_like(y_ref)


def moe_experts(tile_expert, n_used, x, wg, wu, wd, tm, tf):
    n_rows, d = x.shape
    f = wg.shape[2]
    n_ff = f // tf

    def ff_idx(i, j, nu):
        return jnp.where(i < nu[0], j, n_ff - 1)

    grid_spec = pltpu.PrefetchScalarGridSpec(
        num_scalar_prefetch=2,
        grid=(n_rows // tm, n_ff),
        in_specs=[pl.BlockSpec((tm, d), lambda i, j, te, nu: (jnp.minimum(i, nu[0] - 1), 0)),
                  pl.BlockSpec((1, d, tf), lambda i, j, te, nu: (te[i], 0, ff_idx(i, j, nu))),
                  pl.BlockSpec((1, d, tf), lambda i, j, te, nu: (te[i], 0, ff_idx(i, j, nu))),
                  pl.BlockSpec((1, tf, d), lambda i, j, te, nu: (te[i], ff_idx(i, j, nu), 0))],
        out_specs=pl.BlockSpec((tm, d), lambda i, j, te, nu: (i, 0)),
        scratch_shapes=[pltpu.VMEM((tm, d), BF16)],
    )
    return pl.pallas_call(
        _moe_kernel,
        grid_spec=grid_spec,
        out_shape=jax.ShapeDtypeStruct((n_rows, d), F32),
        compiler_params=_params(2),
    )(tile_expert, n_used, x, wg, wu, wd)


def _combine_kernel(seg_ref, cnt_ref, off_ref, h_ref, route_ref, offrow_ref, y_hbm, o_ref, local, sem,
                    *, tm, max_bit):
    b = pl.program_id(0)
    nb = pl.num_programs(0)
    slot = b % 2

    def block_copies(blk, s):
        out = []
        for e in range(N_EXPERTS):
            seg, off = seg_ref[blk * N_EXPERTS + e], off_ref[blk * N_EXPERTS + e]
            out += _run_copies(cnt_ref[blk * N_EXPERTS + e], max_bit, lambda o, n, seg=seg, off=off:
                               pltpu.make_async_copy(y_hbm.at[pl.ds(pl.multiple_of(seg + o, SUBLANES), n)],
                                                     local.at[s, pl.ds(pl.multiple_of(off + o, SUBLANES), n)],
                                                     sem.at[s]))
        return out

    @pl.when(b == 0)
    def _():
        local[...] = jnp.zeros_like(local)
        _start_all(block_copies(0, 0))

    _wait_all(block_copies(b, slot))

    @pl.when(b + 1 < nb)
    def _():
        _start_all(block_copies(b + 1, 1 - slot))

    route = route_ref[...]
    lpos1, lpos2 = _local_positions(route, offrow_ref[0])
    rows_f = lax.broadcasted_iota(I32, (tm, local.shape[1]), 1).astype(F32)
    gate = jnp.where(rows_f == lpos1, route[:, 4:5], 0.0) + jnp.where(rows_f == lpos2, route[:, 5:6], 0.0)
    g_hi, g_lo = _split_bf16(gate)
    y_hi, y_lo = _split_bf16(local[slot])
    o_ref[...] = h_ref[...] + (_dot(g_hi, y_hi) + (_dot(g_lo, y_hi) + _dot(g_hi, y_lo)))


def moe_combine(seg, cnt8, off, h, route, off_rows, y, tm, local_rows):
    t, d = h.shape
    max_bit = (tm // SUBLANES).bit_length() - 1
    kern = functools.partial(_combine_kernel, tm=tm, max_bit=max_bit)
    grid_spec = pltpu.PrefetchScalarGridSpec(
        num_scalar_prefetch=3,
        grid=(t // tm,),
        in_specs=[pl.BlockSpec((tm, d), lambda i, *_: (i, 0)),
                  pl.BlockSpec((tm, LANES), lambda i, *_: (i, 0)),
                  pl.BlockSpec((1, 1, LANES), lambda i, *_: (i, 0, 0)),
                  pl.BlockSpec(memory_space=pl.ANY)],
        out_specs=pl.BlockSpec((tm, d), lambda i, *_: (i, 0)),
        scratch_shapes=[pltpu.VMEM((2, local_rows, d), F32), pltpu.SemaphoreType.DMA((2,))],
    )
    return pl.pallas_call(
        kern,
        grid_spec=grid_spec,
        out_shape=jax.ShapeDtypeStruct((t, d), F32),
        compiler_params=_params(1),
    )(seg, cnt8, off, h, route, off_rows, y)


def _rope_tables(seq):
    half = ATT_HEAD_DIM // 2
    inv_freq = ROPE_THETA ** (-jnp.arange(half, dtype=F32) / half)
    ang = jnp.arange(seq, dtype=jnp.int32).astype(F32)[:, None] * inv_freq[None, :]
    reps = LANES // half
    cos = jnp.tile(jnp.cos(ang), (1, reps))
    sign = jnp.tile(jnp.concatenate([-jnp.ones((half,), F32), jnp.ones((half,), F32)]), LANES // ATT_HEAD_DIM)
    sin = jnp.tile(jnp.sin(ang), (1, reps)) * sign[None, :]
    return cos, sin


def _even_layer(h, bsz, seq, ln_mix, ln_ffn, w_in, conv_w, conv_b, w_a, b_a, w_i, b_i, lam,
                q_norm, k_norm, sinks, w_out, f_gate, f_up, f_down):
    lru_w = conv_w.shape[1]
    q_heads = sinks.shape[0]
    q_w = q_heads * ATT_HEAD_DIM
    kv_w = (w_in.shape[1] - 2 * lru_w - q_w) // 2
    kv_heads = kv_w // ATT_HEAD_DIM

    proj = rms_matmul(h, ln_mix, w_in.astype(BF16), tm=512, tn=w_in.shape[1])
    gates_w = jnp.concatenate([block_diag(*w_a), block_diag(*w_i)], axis=1).astype(BF16)
    y_lru = rglru(proj, conv_w, conv_b, gates_w, jnp.concatenate([b_a, b_i]), lam, bsz, seq, lru_w, tm=256)
    cos, sin = _rope_tables(seq)
    tile2 = lambda g: jnp.tile(g, LANES // ATT_HEAD_DIM).reshape(1, LANES)
    y_att = swa(proj, sinks, cos, sin, tile2(q_norm), tile2(k_norm), bsz, seq,
                q_col=2 * lru_w, k_col=2 * lru_w + q_w, v_col=2 * lru_w + q_w + kv_w,
                q_heads=q_heads, kv_heads=kv_heads)
    w_out = w_out.astype(BF16)
    return ffn(h, [y_lru, y_att], [w_out[:lru_w], w_out[lru_w:]], ln_ffn, f_gate.astype(BF16), f_up.astype(BF16),
               f_down.astype(BF16), tm=512, tf=1408)


def _odd_layer(h, bsz, seq, ln_mix, ln_ffn, w_in, conv_w, a_log, dt_bias, out_norm, w_out, router_w,
               m_gate, m_up, m_down):
    t, d = h.shape
    heads = a_log.shape[0]
    width = heads * GDN_HEAD_DIM
    cols = w_in.shape[1]
    pad = (-cols) % (11 * LANES)
    w_in_p = jnp.pad(w_in, ((0, 0), (0, pad))).astype(BF16)
    proj = rms_matmul(h, ln_mix, w_in_p, tm=512, tn=w_in_p.shape[1])
    lane_pad = lambda v: jnp.pad(v, (0, LANES - heads)).reshape(1, LANES)
    y = gdn(proj, conv_w, lane_pad(a_log), lane_pad(dt_bias), out_norm, bsz, seq, heads, tc=256)
    tb = MOE_BLOCK
    tm = MOE_TILE
    nblk = t // tb
    h, route, cnt = router(h, [y], [w_out.astype(BF16)], ln_ffn,
                           jnp.pad(router_w, ((0, 0), (0, LANES - N_EXPERTS))), tm=tb)
    cnt8 = (cnt[:, 0, :N_EXPERTS].astype(I32) + SUBLANES - 1) // SUBLANES * SUBLANES
    total8 = jnp.sum(cnt8, axis=0)
    tiles_per = (total8 + tm - 1) // tm
    tile_end = jnp.cumsum(tiles_per)
    starts = (tile_end - tiles_per) * tm
    n_used = tile_end[-1:]
    n_tiles = (2 * t + nblk * N_EXPERTS * (SUBLANES - 1)) // tm + N_EXPERTS
    tile_ids = jnp.minimum(jnp.arange(n_tiles, dtype=I32), n_used - 1)
    tile_expert = jnp.sum((tile_ids[:, None] >= tile_end[None, :]).astype(I32), axis=1)
    seg = (starts[None, :] + jnp.cumsum(cnt8, axis=0) - cnt8).reshape(-1)
    off = jnp.cumsum(cnt8, axis=1) - cnt8
    off_rows = jnp.pad(off.astype(F32), ((0, 0), (0, LANES - N_EXPERTS))).reshape(nblk, 1, LANES)
    tail = jnp.concatenate([starts + total8, tiles_per * tm - total8, n_used])
    local_rows = -(-(2 * tb + N_EXPERTS * (SUBLANES - 1)) // LANES) * LANES
    cnt8, off = cnt8.reshape(-1), off.reshape(-1)

    x = moe_dispatch(seg, cnt8, off, tail, h, ln_ffn, route, off_rows, n_tiles * tm, tb, local_rows, tm)
    y = moe_experts(tile_expert, n_used, x, m_gate.astype(BF16), m_up.astype(BF16), m_down.astype(BF16),
                    tm, MOE_FF_CHUNK)
    return moe_combine(seg, cnt8, off, h, route, off_rows, y, tb, local_rows)


def kernel(x, ln_mix, ln_ffn, e_w_in, e_lru_conv_w, e_lru_conv_b, e_lru_w_a, e_lru_b_a, e_lru_w_i, e_lru_b_i, e_lru_lambda, e_q_norm, e_k_norm, e_sinks, e_w_out, e_ffn_w_gate, e_ffn_w_up, e_ffn_w_down, o_w_in, o_conv_w, o_a_log, o_dt_bias, o_out_norm, o_w_out, o_router, o_moe_w_gate, o_moe_w_up, o_moe_w_down):
    bsz, seq, d = x.shape
    h = x.reshape(bsz * seq, d)
    for layer in range(ln_mix.shape[0]):
        j = layer // 2
        if layer % 2 == 0:
            h = _even_layer(h, bsz, seq, ln_mix[layer], ln_ffn[layer], e_w_in[j], e_lru_conv_w[j],
                            e_lru_conv_b[j], e_lru_w_a[j], e_lru_b_a[j], e_lru_w_i[j], e_lru_b_i[j],
                            e_lru_lambda[j], e_q_norm[j], e_k_norm[j], e_sinks[j], e_w_out[j],
                            e_ffn_w_gate[j], e_ffn_w_up[j], e_ffn_w_down[j])
        else:
            h = _odd_layer(h, bsz, seq, ln_mix[layer], ln_ffn[layer], o_w_in[j], o_conv_w[j], o_a_log[j],
                           o_dt_bias[j], o_out_norm[j], o_w_out[j], o_router[j], o_moe_w_gate[j],
                           o_moe_w_up[j], o_moe_w_down[j])
    return h.reshape(bsz, seq, d)
```

```python
import functools

import jax
import jax.numpy as jnp
from jax import lax
from jax.scipy.linalg import block_diag
from jax.experimental import pallas as pl
from jax.experimental.pallas import tpu as pltpu

F32 = jnp.float32
BF16 = jnp.bfloat16
I32 = jnp.int32

EPS = 1e-6
LANES = 128
SUBLANES = 8
MXU_WIDTH = 256
VMEM_LIMIT = 52 * 1024 * 1024

CONV_WIDTH = 4
LRU_C = 8.0
ATT_HEAD_DIM = 64
ATT_WINDOW = 128
ROPE_THETA = 10000.0
GDN_HEAD_DIM = 128
GDN_CHUNK = 64
GDN_PREP_CHUNKS = 2
N_EXPERTS = 8

PROJ_TILE = 512
RGLRU_TILE = 256
GDN_TILE = 512
FFN_TILE = 512
FFN_FF_CHUNK = 1408
MOE_TILE = 512
MOE_BLOCK = 512
MOE_FF_CHUNK = 1792


def _params(n_axes, vmem=VMEM_LIMIT):
    return pltpu.CompilerParams(dimension_semantics=("arbitrary",) * n_axes, vmem_limit_bytes=vmem)


def _rms(x, g):
    return x * lax.rsqrt(jnp.mean(x * x, axis=-1, keepdims=True) + EPS) * g


def _sigmoid(x):
    return 0.5 * (1.0 + jnp.tanh(0.5 * x))


def _silu(x):
    return x * _sigmoid(x)


def _dot(a, b):
    return jnp.dot(a, b, preferred_element_type=F32)


def _dot_nt(a, b):
    return lax.dot_general(a, b, (((1,), (1,)), ((), ())), preferred_element_type=F32)


def _split_bf16(x):
    hi = x.astype(BF16)
    lo = (x - hi.astype(F32)).astype(BF16)
    return hi, lo


def _rms_matmul_kernel(x_ref, g_ref, w_ref, o_ref, xn_ref):
    @pl.when(pl.program_id(1) == 0)
    def _():
        xn_ref[...] = _rms(x_ref[...], g_ref[...]).astype(BF16)

    o_ref[...] = _dot(xn_ref[...], w_ref[...])


def rms_matmul(x, g, w, tm, tn):
    t, d = x.shape
    n = w.shape[1]
    return pl.pallas_call(
        _rms_matmul_kernel,
        grid=(t // tm, n // tn),
        in_specs=[pl.BlockSpec((tm, d), lambda i, j: (i, 0)),
                  pl.BlockSpec((1, d), lambda i, j: (0, 0)),
                  pl.BlockSpec((d, tn), lambda i, j: (0, j))],
        out_specs=pl.BlockSpec((tm, tn), lambda i, j: (i, j)),
        out_shape=jax.ShapeDtypeStruct((t, n), F32),
        scratch_shapes=[pltpu.VMEM((tm, d), BF16)],
        compiler_params=_params(2),
    )(x, g.reshape(1, d), w)


def _causal_conv(tail_ref, x, cw, tm):
    prev = tail_ref[...]
    row = lax.broadcasted_iota(I32, prev.shape, 0)
    acc = None
    for j in range(CONV_WIDTH - 1):
        s = CONV_WIDTH - 1 - j
        xs = pltpu.roll(x, s, 0)
        head = jnp.where(row < s, pltpu.roll(prev, s, 0), xs[0:SUBLANES, :])
        term = cw[j:j + 1, :] * jnp.concatenate([head, xs[SUBLANES:, :]], axis=0)
        acc = term if acc is None else acc + term
    tail_ref[...] = x[tm - SUBLANES:tm, :]
    return acc + cw[CONV_WIDTH - 1:CONV_WIDTH, :] * x


def _rglru_kernel(xb_ref, gate_ref, cw_ref, cb_ref, wg_ref, bg_ref, lam_ref, o_ref, tail_ref, h_ref,
                  *, tm, width):
    @pl.when(pl.program_id(1) == 0)
    def _():
        tail_ref[...] = jnp.zeros_like(tail_ref)
        h_ref[...] = jnp.zeros_like(h_ref)

    xc = _causal_conv(tail_ref, xb_ref[...], cw_ref[...], tm) + cb_ref[...]
    z = _dot(xc.astype(BF16), wg_ref[...]) + bg_ref[...]
    r = _sigmoid(z[:, :width])
    gi = _sigmoid(z[:, width:])
    log_a = -LRU_C * r * jax.nn.softplus(-lam_ref[...])
    a = jnp.exp(log_a)
    u = jnp.sqrt(1.0 - a * a) * (gi * xc)

    row = lax.broadcasted_iota(I32, (tm, width), 0) % SUBLANES
    d = 1
    while d < SUBLANES:
        keep = row >= d
        u = u + jnp.where(keep, a * pltpu.roll(u, d, 0), 0.0)
        a = jnp.where(keep, a * pltpu.roll(a, d, 0), a)
        d *= 2
    carry = h_ref[0:1, :]
    groups = []
    for r0 in range(0, tm, SUBLANES):
        hg = u[r0:r0 + SUBLANES, :] + a[r0:r0 + SUBLANES, :] * carry
        carry = hg[SUBLANES - 1:SUBLANES, :]
        groups.append(hg)
    h = jnp.concatenate(groups, axis=0)
    h_ref[...] = jnp.broadcast_to(carry, h_ref.shape)
    o_ref[...] = (h * jax.nn.gelu(gate_ref[...])).astype(o_ref.dtype)


def rglru(proj, cw, cb, wg, bg, lam, bsz, seq, width, tm):
    nt = seq // tm
    kern = functools.partial(_rglru_kernel, tm=tm, width=width)
    row = lambda c: pl.BlockSpec((1, c), lambda b, i: (0, 0))
    return pl.pallas_call(
        kern,
        grid=(bsz, nt),
        in_specs=[pl.BlockSpec((tm, width), lambda b, i: (b * nt + i, 0)),
                  pl.BlockSpec((tm, width), lambda b, i: (b * nt + i, 1)),
                  pl.BlockSpec((CONV_WIDTH, width), lambda b, i: (0, 0)),
                  row(width),
                  pl.BlockSpec((width, 2 * width), lambda b, i: (0, 0)),
                  row(2 * width),
                  row(width)],
        out_specs=pl.BlockSpec((tm, width), lambda b, i: (b * nt + i, 0)),
        out_shape=jax.ShapeDtypeStruct((bsz * seq, width), BF16),
        scratch_shapes=[pltpu.VMEM((SUBLANES, width), F32), pltpu.VMEM((SUBLANES, width), F32)],
        compiler_params=_params(2),
    )(proj, proj, cw, cb.reshape(1, -1), wg, bg.reshape(1, -1), lam.reshape(1, -1))


def _swa_kernel(sink_ref, q_ref, k_ref, v_ref, cos_ref, sin_ref, qg_ref, kg_ref, o_ref, kall_ref, vall_ref,
                *, q_heads, kv_heads):
    w = ATT_WINDOW
    n = pl.program_id(1)
    group = q_heads // kv_heads
    assert kv_heads * ATT_HEAD_DIM == LANES and group % 2 == 0

    @pl.when(n == 0)
    def _():
        kall_ref[0:w, :] = jnp.zeros((w, LANES), BF16)
        vall_ref[0:w, :] = jnp.zeros((w, LANES), BF16)

    @pl.when(n > 0)
    def _():
        kall_ref[0:w, :] = kall_ref[w:2 * w, :]
        vall_ref[0:w, :] = vall_ref[w:2 * w, :]

    lane = lax.broadcasted_iota(I32, (w, LANES), 1)
    first_half = (lane % ATT_HEAD_DIM) < (ATT_HEAD_DIM // 2)
    lane_head = lane // ATT_HEAD_DIM
    bd_r = lax.broadcasted_iota(I32, (LANES, LANES), 0) // ATT_HEAD_DIM
    bd_c = lax.broadcasted_iota(I32, (LANES, LANES), 1) // ATT_HEAD_DIM
    head_ones = (bd_r == bd_c).astype(BF16)
    cos = cos_ref[...]
    sin = sin_ref[...]

    def norm_rope(x, g):
        hi, lo = _split_bf16(x * x)
        ss = _dot(hi, head_ones) + _dot(lo, head_ones)
        xn = x * lax.rsqrt(ss * (1.0 / ATT_HEAD_DIM) + EPS) * g
        rot = jnp.where(first_half, pltpu.roll(xn, LANES - ATT_HEAD_DIM // 2, 1),
                        pltpu.roll(xn, ATT_HEAD_DIM // 2, 1))
        return xn * cos + rot * sin

    kall_ref[w:2 * w, :] = norm_rope(k_ref[...], kg_ref[...]).astype(BF16)
    vall_ref[w:2 * w, :] = v_ref[...].astype(BF16)
    qc = [norm_rope(q_ref[:, LANES * c:LANES * (c + 1)], qg_ref[...]) for c in range(q_heads // 2)]

    rows = group * w
    qi = lax.broadcasted_iota(I32, (rows, 2 * w), 0) % w
    kj = lax.broadcasted_iota(I32, (rows, 2 * w), 1)
    valid = (kj > qi) & (kj <= qi + w) & ((n > 0) | (kj >= w))
    row_head = lax.broadcasted_iota(I32, (rows, 1), 0) // w

    outs = []
    for g in range(kv_heads):
        parts = []
        for hh in range(group):
            h = g * group + hh
            x = qc[h // 2]
            if h % 2 != g:
                x = pltpu.roll(x, ATT_HEAD_DIM, 1)
            parts.append(jnp.where(lane_head == g, x, 0.0))
        qs = jnp.concatenate(parts, axis=0).astype(BF16)
        s = _dot_nt(qs, kall_ref[...]) * (ATT_HEAD_DIM ** -0.5)
        s = jnp.where(valid, s, -1e30)
        sink = jnp.zeros((rows, 1), F32)
        for hh in range(group):
            sink = jnp.where(row_head == hh, sink_ref[g * group + hh], sink)
        m = jnp.maximum(jnp.max(s, axis=-1, keepdims=True), sink)
        p = jnp.exp(s - m)
        denom = jnp.sum(p, axis=-1, keepdims=True) + jnp.exp(sink - m)
        pv = _dot(p.astype(BF16), vall_ref[...]) / denom
        outs.extend(pv[w * hh:w * (hh + 1), :] for hh in range(group))

    for c in range(q_heads // 2):
        g = (2 * c) // group
        a, b = outs[2 * c], outs[2 * c + 1]
        if g == 1:
            a = pltpu.roll(a, ATT_HEAD_DIM, 1)
        else:
            b = pltpu.roll(b, ATT_HEAD_DIM, 1)
        o_ref[:, LANES * c:LANES * (c + 1)] = jnp.where(lane_head == 0, a, b).astype(o_ref.dtype)


def swa(proj, sinks, cos, sin, qg, kg, bsz, seq, q_col, k_col, v_col, q_heads, kv_heads):
    w = ATT_WINDOW
    nb = seq // w
    qw = q_heads * ATT_HEAD_DIM
    kern = functools.partial(_swa_kernel, q_heads=q_heads, kv_heads=kv_heads)
    return pl.pallas_call(
        kern,
        grid=(bsz, nb),
        in_specs=[pl.BlockSpec(memory_space=pltpu.SMEM),
                  pl.BlockSpec((w, qw), lambda b, i: (b * nb + i, q_col // qw)),
                  pl.BlockSpec((w, LANES), lambda b, i: (b * nb + i, k_col // LANES)),
                  pl.BlockSpec((w, LANES), lambda b, i: (b * nb + i, v_col // LANES)),
                  pl.BlockSpec((w, LANES), lambda b, i: (i, 0)),
                  pl.BlockSpec((w, LANES), lambda b, i: (i, 0)),
                  pl.BlockSpec((1, LANES), lambda b, i: (0, 0)),
                  pl.BlockSpec((1, LANES), lambda b, i: (0, 0))],
        out_specs=pl.BlockSpec((w, qw), lambda b, i: (b * nb + i, 0)),
        out_shape=jax.ShapeDtypeStruct((bsz * seq, qw), BF16),
        scratch_shapes=[pltpu.VMEM((2 * w, LANES), BF16), pltpu.VMEM((2 * w, LANES), BF16)],
        compiler_params=_params(2),
    )(sinks, proj, proj, proj, cos, sin, qg, kg)


def _mix_specs(ys, ws, tm, index_map):
    return ([pl.BlockSpec((tm, y.shape[1]), index_map(lambda i: (i, 0))) for y in ys]
            + [pl.BlockSpec(w.shape, index_map(lambda i: (0, 0))) for w in ws])


def _mix_residual(h_ref, mix_refs):
    n_in = len(mix_refs) // 2
    x = h_ref[...]
    for y_ref, w_ref in zip(mix_refs[:n_in], mix_refs[n_in:]):
        x = x + _dot(y_ref[...], w_ref[...])
    return x


def _ffn_kernel(h_ref, *refs):
    *mix_refs, g_ref, wg_ref, wu_ref, wd_ref, o_ref, xn_ref = refs
    j = pl.program_id(1)

    @pl.when(j == 0)
    def _():
        x = _mix_residual(h_ref, mix_refs)
        xn_ref[...] = _rms(x, g_ref[...]).astype(BF16)
        o_ref[...] = x

    xn = xn_ref[...]
    act = (_silu(_dot(xn, wg_ref[...])) * _dot(xn, wu_ref[...])).astype(BF16)
    o_ref[...] += _dot(act, wd_ref[...])


def ffn(h, ys, ws, g, wg, wu, wd, tm, tf):
    t, d = h.shape
    f = wg.shape[1]
    return pl.pallas_call(
        _ffn_kernel,
        grid=(t // tm, f // tf),
        in_specs=([pl.BlockSpec((tm, d), lambda i, j: (i, 0))]
                  + _mix_specs(ys, ws, tm, lambda fn: lambda i, j: fn(i))
                  + [pl.BlockSpec((1, d), lambda i, j: (0, 0)),
                     pl.BlockSpec((d, tf), lambda i, j: (0, j)),
                     pl.BlockSpec((d, tf), lambda i, j: (0, j)),
                     pl.BlockSpec((tf, d), lambda i, j: (j, 0))]),
        out_specs=pl.BlockSpec((tm, d), lambda i, j: (i, 0)),
        out_shape=jax.ShapeDtypeStruct((t, d), F32),
        scratch_shapes=[pltpu.VMEM((tm, d), BF16)],
        compiler_params=_params(2),
    )(h, *ys, *ws, g.reshape(1, d), wg, wu, wd)


def _gdn_kernel(qkv_ref, gate_ref, ab_ref, cw_ref, alog_ref, dtb_ref, onorm_ref, o_ref,
                tail_ref, q_ref, k_ref, v_ref, gc_ref, beta_ref, uw_ref, aqk_ref, qd_ref, kd_ref, *state_refs,
                tc, heads):
    hd = GDN_HEAD_DIM
    c = GDN_CHUNK
    width = heads * hd

    @pl.when(pl.program_id(1) == 0)
    def _():
        tail_ref[...] = jnp.zeros_like(tail_ref)
        for s_ref in state_refs:
            s_ref[...] = jnp.zeros_like(s_ref)

    qkv = _silu(_causal_conv(tail_ref, qkv_ref[...], cw_ref[...], tc))
    for h in range(heads):
        q = qkv[:, hd * h:hd * (h + 1)]
        k = qkv[:, width + hd * h:width + hd * (h + 1)]
        q_ref[:, hd * h:hd * (h + 1)] = (q * lax.rsqrt(jnp.sum(q * q, axis=-1, keepdims=True) + EPS)
                                          * (hd ** -0.5))
        k_ref[:, hd * h:hd * (h + 1)] = k * lax.rsqrt(jnp.sum(k * k, axis=-1, keepdims=True) + EPS)
    v_ref[...] = qkv[:, 2 * width:]

    ab = ab_ref[...]
    g = -jnp.exp(alog_ref[...]) * jax.nn.softplus(ab + dtb_ref[...])
    beta_ref[...] = pltpu.roll(jax.nn.sigmoid(ab), LANES - heads, 1)
    row = lax.broadcasted_iota(I32, (tc, LANES), 0) % c
    d = 1
    while d < c:
        g = g + jnp.where(row >= d, pltpu.roll(g, d, 0), 0.0)
        d *= 2
    gc_ref[...] = g

    ri = lax.broadcasted_iota(I32, (c, c), 0)
    ci = lax.broadcasted_iota(I32, (c, c), 1)
    tri = ri >= ci
    strict = ri > ci
    eye = (ri == ci).astype(F32)

    hs = range(heads)
    cols = [slice(hd * h, hd * (h + 1)) for h in hs]
    n_chunks = tc // c
    rows = [slice(c * n, c * (n + 1)) for n in range(n_chunks)]

    for g0 in range(0, n_chunks, GDN_PREP_CHUNKS):
        items = [(n, h) for n in range(g0, min(g0 + GDN_PREP_CHUNKS, n_chunks)) for h in hs]
        gc_all = {n: gc_ref[rows[n], :] for n, _ in items}
        gr_all = {n: gc_all[n].T for n in gc_all}
        beta_all = {n: beta_ref[rows[n], :] for n in gc_all}
        gcol = {(n, h): gc_all[n][:, h:h + 1] for n, h in items}
        grow = {(n, h): gr_all[n][h:h + 1, :] for n, h in items}
        beta = {(n, h): beta_all[n][:, h:h + 1] for n, h in items}
        qk_kk = {(n, h): _dot_nt(jnp.concatenate([q_ref[rows[n], cols[h]], k_ref[rows[n], cols[h]]],
                                                 axis=0).astype(BF16), k_ref[rows[n], cols[h]].astype(BF16))
                 for n, h in items}
        decay = {it: jnp.where(tri, jnp.exp(jnp.where(tri, gcol[it] - grow[it], 0.0)), 0.0) for it in items}
        m = {it: -jnp.where(strict, beta[it] * qk_kk[it][c:, :] * decay[it], 0.0) for it in items}
        p = {it: eye + m[it] for it in items}
        e = {it: _dot(m[it].astype(BF16), m[it].astype(BF16)) for it in items}
        lvl = 2
        while 2 * lvl < c:
            r = {it: _dot(jnp.concatenate([p[it], e[it]], axis=0).astype(BF16), e[it].astype(BF16)) for it in items}
            p = {it: p[it] + r[it][:c, :] for it in items}
            e = {it: r[it][c:, :] for it in items}
            lvl *= 2
        p = {it: p[it] + _dot(p[it].astype(BF16), e[it].astype(BF16)) for it in items}
        for n, h in items:
            it = (n, h)
            k = k_ref[rows[n], cols[h]]
            eg = jnp.exp(gcol[it])
            rhs = jnp.concatenate([beta[it] * v_ref[rows[n], cols[h]], (beta[it] * eg) * k], axis=1)
            uw_ref[rows[n], 2 * hd * h:2 * hd * (h + 1)] = _dot(p[it].astype(BF16), rhs.astype(BF16))
            aqk_ref[rows[n], LANES * h:LANES * h + c] = jnp.where(tri, qk_kk[it][:c, :] * decay[it], 0.0)
            qd_ref[rows[n], cols[h]] = q_ref[rows[n], cols[h]] * eg
            kd_ref[rows[n], cols[h]] = k * jnp.exp(gcol[it][c - 1:c, :] - gcol[it])

    for n in range(n_chunks):
        gc_n = gc_ref[rows[n], :]
        g_last = [jnp.exp(gc_n[c - 1:c, h:h + 1]) for h in hs]
        s = [state_refs[h][...] for h in hs]
        ws_qs = [_dot(jnp.concatenate([uw_ref[rows[n], 2 * hd * h + hd:2 * hd * (h + 1)], qd_ref[rows[n], cols[h]]],
                                      axis=0).astype(BF16), s[h].astype(BF16)) for h in hs]
        vb = [(uw_ref[rows[n], 2 * hd * h:2 * hd * h + hd] - ws_qs[h][:c, :]).astype(BF16) for h in hs]
        o = [ws_qs[h][c:, :] + _dot(aqk_ref[rows[n], LANES * h:LANES * h + c].astype(BF16), vb[h]) for h in hs]
        s_new = [s[h] * g_last[h] + _dot(kd_ref[rows[n], cols[h]].T.astype(BF16), vb[h]) for h in hs]
        for h in hs:
            state_refs[h][...] = s_new[h]
            on = o[h] * lax.rsqrt(jnp.mean(o[h] * o[h], axis=-1, keepdims=True) + EPS) * onorm_ref[...]
            o_ref[rows[n], cols[h]] = (on * _silu(gate_ref[rows[n], cols[h]])).astype(o_ref.dtype)


def gdn(proj, cw, alog, dtb, onorm, bsz, seq, heads, tc):
    nt = seq // tc
    width = heads * GDN_HEAD_DIM
    kern = functools.partial(_gdn_kernel, tc=tc, heads=heads)
    row = lambda n: pl.BlockSpec((1, n), lambda b, i: (0, 0))
    return pl.pallas_call(
        kern,
        grid=(bsz, nt),
        in_specs=[pl.BlockSpec((tc, 3 * width), lambda b, i: (b * nt + i, 0)),
                  pl.BlockSpec((tc, width), lambda b, i: (b * nt + i, 3)),
                  pl.BlockSpec((tc, LANES), lambda b, i: (b * nt + i, 4 * width // LANES)),
                  pl.BlockSpec((CONV_WIDTH, 3 * width), lambda b, i: (0, 0)),
                  row(LANES), row(LANES), row(GDN_HEAD_DIM)],
        out_specs=pl.BlockSpec((tc, width), lambda b, i: (b * nt + i, 0)),
        out_shape=jax.ShapeDtypeStruct((bsz * seq, width), BF16),
        scratch_shapes=[pltpu.VMEM((SUBLANES, 3 * width), F32),
                        pltpu.VMEM((tc, width), F32), pltpu.VMEM((tc, width), F32),
                        pltpu.VMEM((tc, width), F32),
                        pltpu.VMEM((tc, LANES), F32), pltpu.VMEM((tc, LANES), F32),
                        pltpu.VMEM((tc, 2 * width), F32), pltpu.VMEM((tc, heads * LANES), F32),
                        pltpu.VMEM((tc, width), F32), pltpu.VMEM((tc, width), F32),
                        *[pltpu.VMEM((GDN_HEAD_DIM, GDN_HEAD_DIM), F32) for _ in range(heads)]],
        compiler_params=_params(2),
    )(proj, proj, proj, cw, alog, dtb, onorm.reshape(1, -1))


def _router_kernel(h_ref, *refs, tm):
    *mix_refs, g_ref, r_ref, x_ref, route_ref, cnt_ref = refs
    x = _mix_residual(h_ref, mix_refs)
    x_ref[...] = x
    t_hi, t_lo = _split_bf16(_rms(x, g_ref[...]))
    r_hi, r_lo = _split_bf16(r_ref[...])
    logits = _dot(t_hi, r_hi) + (_dot(t_lo, r_hi) + _dot(t_hi, r_lo))
    lane = lax.broadcasted_iota(I32, (tm, LANES), 1)
    lane_f = lane.astype(F32)
    neg = -jnp.inf
    lg = jnp.where(lane < N_EXPERTS, logits, neg)
    m1 = jnp.max(lg, axis=-1, keepdims=True)
    i1 = jnp.min(jnp.where(lg == m1, lane_f, float(LANES)), axis=-1, keepdims=True)
    oh1 = lane_f == i1
    lg2 = jnp.where(oh1, neg, lg)
    m2 = jnp.max(lg2, axis=-1, keepdims=True)
    i2 = jnp.min(jnp.where(lg2 == m2, lane_f, float(LANES)), axis=-1, keepdims=True)
    oh2 = lane_f == i2
    e2 = jnp.exp(m2 - m1)
    w1 = 1.0 / (1.0 + e2)
    w2 = e2 / (1.0 + e2)

    both = (oh1 | oh2).astype(BF16)
    ri = lax.broadcasted_iota(I32, (tm, tm), 0)
    ci = lax.broadcasted_iota(I32, (tm, tm), 1)
    pos = _dot((ri > ci).astype(BF16), both)
    rank1 = jnp.sum(jnp.where(oh1, pos, 0.0), axis=-1, keepdims=True)
    rank2 = jnp.sum(jnp.where(oh2, pos, 0.0), axis=-1, keepdims=True)
    cnt_ref[0] = jnp.broadcast_to(jnp.sum(both.astype(F32), axis=0, keepdims=True), (SUBLANES, LANES))

    out = jnp.where(lane == 0, i1, 0.0)
    out = jnp.where(lane == 1, i2, out)
    out = jnp.where(lane == 2, rank1, out)
    out = jnp.where(lane == 3, rank2, out)
    out = jnp.where(lane == 4, w1, out)
    route_ref[...] = jnp.where(lane == 5, w2, out)


def router(h, ys, ws, g, r, tm):
    t, d = h.shape
    kern = functools.partial(_router_kernel, tm=tm)
    return pl.pallas_call(
        kern,
        grid=(t // tm,),
        in_specs=([pl.BlockSpec((tm, d), lambda i: (i, 0))]
                  + _mix_specs(ys, ws, tm, lambda fn: fn)
                  + [pl.BlockSpec((1, d), lambda i: (0, 0)),
                     pl.BlockSpec((d, LANES), lambda i: (0, 0))]),
        out_specs=[pl.BlockSpec((tm, d), lambda i: (i, 0)),
                   pl.BlockSpec((tm, LANES), lambda i: (i, 0)),
                   pl.BlockSpec((1, SUBLANES, LANES), lambda i: (i, 0, 0))],
        out_shape=[jax.ShapeDtypeStruct((t, d), F32),
                   jax.ShapeDtypeStruct((t, LANES), F32),
                   jax.ShapeDtypeStruct((t // tm, SUBLANES, LANES), F32)],
        compiler_params=_params(1),
    )(h, *ys, *ws, g.reshape(1, d), r)


def _local_positions(route, off_row):
    lane_f = lax.broadcasted_iota(I32, route.shape, 1).astype(F32)
    lpos1 = jnp.sum(jnp.where(lane_f == route[:, 0:1], off_row, 0.0), axis=-1, keepdims=True) + route[:, 2:3]
    lpos2 = jnp.sum(jnp.where(lane_f == route[:, 1:2], off_row, 0.0), axis=-1, keepdims=True) + route[:, 3:4]
    return lpos1, lpos2


def _run_copies(n_rows, max_bit, make):
    out = []
    for k in range(max_bit, -1, -1):
        bit = k + 3
        offset = pl.multiple_of((n_rows >> (bit + 1)) << (bit + 1), SUBLANES)
        out.append((((n_rows >> bit) & 1) == 1, make(offset, SUBLANES << k)))
    return out


def _start_all(copies):
    for pred, cp in copies:
        pl.when(pred)(cp.start)


def _wait_all(copies):
    for pred, cp in copies:
        pl.when(pred)(cp.wait)


def _dispatch_kernel(seg_ref, cnt_ref, off_ref, tail_ref, h_ref, g_ref, route_ref, offrow_ref, x_hbm,
                     local, zeros, sem, zsem, *, tm, max_bit, tile_rows):
    b = pl.program_id(0)
    nb = pl.num_programs(0)
    slot = b % 2

    def block_copies(blk, s):
        out = []
        for e in range(N_EXPERTS):
            seg, off = seg_ref[blk * N_EXPERTS + e], off_ref[blk * N_EXPERTS + e]
            out += _run_copies(cnt_ref[blk * N_EXPERTS + e], max_bit, lambda o, n, seg=seg, off=off:
                               pltpu.make_async_copy(local.at[s, pl.ds(pl.multiple_of(off + o, SUBLANES), n)],
                                                     x_hbm.at[pl.ds(pl.multiple_of(seg + o, SUBLANES), n)],
                                                     sem.at[s]))
        return out

    def tail_copies():
        out = []
        for e in range(N_EXPERTS):
            start = tail_ref[e]
            out += _run_copies(tail_ref[N_EXPERTS + e], max_bit - 1, lambda o, n, start=start:
                               pltpu.make_async_copy(zeros.at[pl.ds(0, n)],
                                                     x_hbm.at[pl.ds(pl.multiple_of(start + o, SUBLANES), n)], zsem))
        return out

    @pl.when(b >= 2)
    def _():
        _wait_all(block_copies(b - 2, slot))

    t = _rms(h_ref[...], g_ref[...]).astype(BF16)
    lpos1, lpos2 = _local_positions(route_ref[...], offrow_ref[0])
    rows_f = lax.broadcasted_iota(I32, (tm, local.shape[1]), 1).astype(F32)
    place = ((rows_f == lpos1) | (rows_f == lpos2)).astype(BF16)
    local[slot] = lax.dot_general(place, t, (((0,), (0,)), ((), ())), preferred_element_type=F32)
    _start_all(block_copies(b, slot))

    @pl.when(b == nb - 1)
    def _():
        zeros[...] = jnp.zeros_like(zeros)
        _start_all(tail_copies())

        @pl.when(b >= 1)
        def _():
            _wait_all(block_copies(b - 1, 1 - slot))

        _wait_all(block_copies(b, slot))
        _wait_all(tail_copies())

        def fill_idle_tile(tile, carry):
            cp = pltpu.make_async_copy(zeros, x_hbm.at[pl.ds(pl.multiple_of(tile * tile_rows, tile_rows), tile_rows)],
                                       zsem)
            cp.start()
            cp.wait()
            return carry
        lax.fori_loop(tail_ref[2 * N_EXPERTS], x_hbm.shape[0] // tile_rows, fill_idle_tile, 0)


def moe_dispatch(seg, cnt8, off, tail, h, g, route, off_rows, n_rows, tm, local_rows, tile_rows):
    t, d = h.shape
    max_bit = (tm // SUBLANES).bit_length() - 1
    assert tile_rows >= tm // 2 and n_rows % tile_rows == 0
    kern = functools.partial(_dispatch_kernel, tm=tm, max_bit=max_bit, tile_rows=tile_rows)
    grid_spec = pltpu.PrefetchScalarGridSpec(
        num_scalar_prefetch=4,
        grid=(t // tm,),
        in_specs=[pl.BlockSpec((tm, d), lambda i, *_: (i, 0)),
                  pl.BlockSpec((1, d), lambda i, *_: (0, 0)),
                  pl.BlockSpec((tm, LANES), lambda i, *_: (i, 0)),
                  pl.BlockSpec((1, 1, LANES), lambda i, *_: (i, 0, 0))],
        out_specs=pl.BlockSpec(memory_space=pl.ANY),
        scratch_shapes=[pltpu.VMEM((2, local_rows, d), F32), pltpu.VMEM((tile_rows, d), F32),
                        pltpu.SemaphoreType.DMA((2,)), pltpu.SemaphoreType.DMA(())],
    )
    return pl.pallas_call(
        kern,
        grid_spec=grid_spec,
        out_shape=jax.ShapeDtypeStruct((n_rows, d), F32),
        compiler_params=_params(1),
    )(seg, cnt8, off, tail, h, g.reshape(1, d), route, off_rows)


def _moe_kernel(te_ref, nu_ref, x_ref, wg_ref, wu_ref, wd_ref, y_ref, xn_ref):
    i = pl.program_id(0)
    j = pl.program_id(1)
    n_used = nu_ref[0]

    @pl.when(i < n_used)
    def _():
        @pl.when(j == 0)
        def _():
            xn_ref[...] = x_ref[...].astype(BF16)

        xn = xn_ref[...]
        act = (_silu(_dot(xn, wg_ref[0])) * _dot(xn, wu_ref[0])).astype(BF16)
        part = _dot(act, wd_ref[0])

        @pl.when(j == 0)
        def _():
            y_ref[...] = part

        @pl.when(j > 0)
        def _():
            y_ref[...] += part

    @pl.when((i >= n_used) & (j == 0))
    def _():
        y_ref[...] = jnp.zeros_like(y_ref)


def moe_experts(tile_expert, n_used, x, wg, wu, wd, tm, tf):
    n_rows, d = x.shape
    f = wg.shape[2]
    n_ff = f // tf

    def ff_idx(i, j, nu):
        return jnp.where(i < nu[0], j, n_ff - 1)

    grid_spec = pltpu.PrefetchScalarGridSpec(
        num_scalar_prefetch=2,
        grid=(n_rows // tm, n_ff),
        in_specs=[pl.BlockSpec((tm, d), lambda i, j, te, nu: (jnp.minimum(i, nu[0] - 1), 0)),
                  pl.BlockSpec((1, d, tf), lambda i, j, te, nu: (te[i], 0, ff_idx(i, j, nu))),
                  pl.BlockSpec((1, d, tf), lambda i, j, te, nu: (te[i], 0, ff_idx(i, j, nu))),
                  pl.BlockSpec((1, tf, d), lambda i, j, te, nu: (te[i], ff_idx(i, j, nu), 0))],
        out_specs=pl.BlockSpec((tm, d), lambda i, j, te, nu: (i, 0)),
        scratch_shapes=[pltpu.VMEM((tm, d), BF16)],
    )
    return pl.pallas_call(
        _moe_kernel,
        grid_spec=grid_spec,
        out_shape=jax.ShapeDtypeStruct((n_rows, d), F32),
        compiler_params=_params(2),
    )(tile_expert, n_used, x, wg, wu, wd)


def _combine_kernel(seg_ref, cnt_ref, off_ref, h_ref, route_ref, offrow_ref, y_hbm, o_ref, local, sem,
                    *, tm, max_bit):
    b = pl.program_id(0)
    nb = pl.num_programs(0)
    slot = b % 2

    def block_copies(blk, s):
        out = []
        for e in range(N_EXPERTS):
            seg, off = seg_ref[blk * N_EXPERTS + e], off_ref[blk * N_EXPERTS + e]
            out += _run_copies(cnt_ref[blk * N_EXPERTS + e], max_bit, lambda o, n, seg=seg, off=off:
                               pltpu.make_async_copy(y_hbm.at[pl.ds(pl.multiple_of(seg + o, SUBLANES), n)],
                                                     local.at[s, pl.ds(pl.multiple_of(off + o, SUBLANES), n)],
                                                     sem.at[s]))
        return out

    @pl.when(b == 0)
    def _():
        local[...] = jnp.zeros_like(local)
        _start_all(block_copies(0, 0))

    _wait_all(block_copies(b, slot))

    @pl.when(b + 1 < nb)
    def _():
        _start_all(block_copies(b + 1, 1 - slot))

    route = route_ref[...]
    lpos1, lpos2 = _local_positions(route, offrow_ref[0])
    rows_f = lax.broadcasted_iota(I32, (tm, local.shape[1]), 1).astype(F32)
    gate = jnp.where(rows_f == lpos1, route[:, 4:5], 0.0) + jnp.where(rows_f == lpos2, route[:, 5:6], 0.0)
    g_hi, g_lo = _split_bf16(gate)
    y = local[slot].astype(BF16)
    o_ref[...] = h_ref[...] + (_dot(g_hi, y) + _dot(g_lo, y))


def moe_combine(seg, cnt8, off, h, route, off_rows, y, tm, local_rows):
    t, d = h.shape
    max_bit = (tm // SUBLANES).bit_length() - 1
    kern = functools.partial(_combine_kernel, tm=tm, max_bit=max_bit)
    grid_spec = pltpu.PrefetchScalarGridSpec(
        num_scalar_prefetch=3,
        grid=(t // tm,),
        in_specs=[pl.BlockSpec((tm, d), lambda i, *_: (i, 0)),
                  pl.BlockSpec((tm, LANES), lambda i, *_: (i, 0)),
                  pl.BlockSpec((1, 1, LANES), lambda i, *_: (i, 0, 0)),
                  pl.BlockSpec(memory_space=pl.ANY)],
        out_specs=pl.BlockSpec((tm, d), lambda i, *_: (i, 0)),
        scratch_shapes=[pltpu.VMEM((2, local_rows, d), F32), pltpu.SemaphoreType.DMA((2,))],
    )
    return pl.pallas_call(
        kern,
        grid_spec=grid_spec,
        out_shape=jax.ShapeDtypeStruct((t, d), F32),
        compiler_params=_params(1),
    )(seg, cnt8, off, h, route, off_rows, y)


def _rope_tables(seq):
    half = ATT_HEAD_DIM // 2
    inv_freq = ROPE_THETA ** (-jnp.arange(half, dtype=F32) / half)
    ang = jnp.arange(seq, dtype=jnp.int32).astype(F32)[:, None] * inv_freq[None, :]
    reps = LANES // half
    cos = jnp.tile(jnp.cos(ang), (1, reps))
    sign = jnp.tile(jnp.concatenate([-jnp.ones((half,), F32), jnp.ones((half,), F32)]), LANES // ATT_HEAD_DIM)
    sin = jnp.tile(jnp.sin(ang), (1, reps)) * sign[None, :]
    return cos, sin


def _even_layer(h, bsz, seq, ln_mix, ln_ffn, w_in, conv_w, conv_b, w_a, b_a, w_i, b_i, lam,
                q_norm, k_norm, sinks, w_out, f_gate, f_up, f_down):
    lru_w = conv_w.shape[1]
    q_heads = sinks.shape[0]
    q_w = q_heads * ATT_HEAD_DIM
    kv_w = (w_in.shape[1] - 2 * lru_w - q_w) // 2
    kv_heads = kv_w // ATT_HEAD_DIM

    proj = rms_matmul(h, ln_mix, w_in.astype(BF16), tm=PROJ_TILE, tn=w_in.shape[1])
    gates_w = jnp.concatenate([block_diag(*w_a), block_diag(*w_i)], axis=1).astype(BF16)
    y_lru = rglru(proj, conv_w, conv_b, gates_w, jnp.concatenate([b_a, b_i]), lam, bsz, seq, lru_w,
                  tm=RGLRU_TILE)
    cos, sin = _rope_tables(seq)
    tile2 = lambda g: jnp.tile(g, LANES // ATT_HEAD_DIM).reshape(1, LANES)
    y_att = swa(proj, sinks, cos, sin, tile2(q_norm), tile2(k_norm), bsz, seq,
                q_col=2 * lru_w, k_col=2 * lru_w + q_w, v_col=2 * lru_w + q_w + kv_w,
                q_heads=q_heads, kv_heads=kv_heads)
    w_out = w_out.astype(BF16)
    return ffn(h, [y_lru, y_att], [w_out[:lru_w], w_out[lru_w:]], ln_ffn, f_gate.astype(BF16), f_up.astype(BF16),
               f_down.astype(BF16), tm=FFN_TILE, tf=FFN_FF_CHUNK)


def _odd_layer(h, bsz, seq, ln_mix, ln_ffn, w_in, conv_w, a_log, dt_bias, out_norm, w_out, router_w,
               m_gate, m_up, m_down):
    t, d = h.shape
    heads = a_log.shape[0]
    width = heads * GDN_HEAD_DIM
    cols = w_in.shape[1]
    pad = (-cols) % (11 * LANES)
    w_in_p = jnp.pad(w_in, ((0, 0), (0, pad))).astype(BF16)
    proj = rms_matmul(h, ln_mix, w_in_p, tm=PROJ_TILE, tn=w_in_p.shape[1])
    lane_pad = lambda v: jnp.pad(v, (0, LANES - heads)).reshape(1, LANES)
    y = gdn(proj, conv_w, lane_pad(a_log), lane_pad(dt_bias), out_norm, bsz, seq, heads, tc=GDN_TILE)
    tb = MOE_BLOCK
    tm = MOE_TILE
    nblk = t // tb
    h, route, cnt = router(h, [y], [w_out.astype(BF16)], ln_ffn,
                           jnp.pad(router_w, ((0, 0), (0, LANES - N_EXPERTS))), tm=tb)
    cnt8 = (cnt[:, 0, :N_EXPERTS].astype(I32) + SUBLANES - 1) // SUBLANES * SUBLANES
    total8 = jnp.sum(cnt8, axis=0)
    tiles_per = (total8 + tm - 1) // tm
    tile_end = jnp.cumsum(tiles_per)
    starts = (tile_end - tiles_per) * tm
    n_used = tile_end[-1:]
    n_tiles = (2 * t + nblk * N_EXPERTS * (SUBLANES - 1)) // tm + N_EXPERTS
    tile_ids = jnp.minimum(jnp.arange(n_tiles, dtype=I32), n_used - 1)
    tile_expert = jnp.sum((tile_ids[:, None] >= tile_end[None, :]).astype(I32), axis=1)
    seg = (starts[None, :] + jnp.cumsum(cnt8, axis=0) - cnt8).reshape(-1)
    off = jnp.cumsum(cnt8, axis=1) - cnt8
    off_rows = jnp.pad(off.astype(F32), ((0, 0), (0, LANES - N_EXPERTS))).reshape(nblk, 1, LANES)
    tail = jnp.concatenate([starts + total8, tiles_per * tm - total8, n_used])
    local_rows = -(-(2 * tb + N_EXPERTS * (SUBLANES - 1)) // LANES) * LANES
    cnt8, off = cnt8.reshape(-1), off.reshape(-1)

    x = moe_dispatch(seg, cnt8, off, tail, h, ln_ffn, route, off_rows, n_tiles * tm, tb, local_rows, tm)
    y = moe_experts(tile_expert, n_used, x, m_gate.astype(BF16), m_up.astype(BF16), m_down.astype(BF16),
                    tm, MOE_FF_CHUNK)
    return moe_combine(seg, cnt8, off, h, route, off_rows, y, tb, local_rows)


def kernel(x, ln_mix, ln_ffn, e_w_in, e_lru_conv_w, e_lru_conv_b, e_lru_w_a, e_lru_b_a, e_lru_w_i, e_lru_b_i, e_lru_lambda, e_q_norm, e_k_norm, e_sinks, e_w_out, e_ffn_w_gate, e_ffn_w_up, e_ffn_w_down, o_w_in, o_conv_w, o_a_log, o_dt_bias, o_out_norm, o_w_out, o_router, o_moe_w_gate, o_moe_w_up, o_moe_w_down):
    bsz, seq, d = x.shape
    h = x.reshape(bsz * seq, d)
    for layer in range(ln_mix.shape[0]):
        j = layer // 2
        if layer % 2 == 0:
            h = _even_layer(h, bsz, seq, ln_mix[layer], ln_ffn[layer], e_w_in[j], e_lru_conv_w[j],
                            e_lru_conv_b[j], e_lru_w_a[j], e_lru_b_a[j], e_lru_w_i[j], e_lru_b_i[j],
                            e_lru_lambda[j], e_q_norm[j], e_k_norm[j], e_sinks[j], e_w_out[j],
                            e_ffn_w_gate[j], e_ffn_w_up[j], e_ffn_w_down[j])
        else:
            h = _odd_layer(h, bsz, seq, ln_mix[layer], ln_ffn[layer], o_w_in[j], o_conv_w[j], o_a_log[j],
                           o_dt_bias[j], o_out_norm[j], o_w_out[j], o_router[j], o_moe_w_gate[j],
                           o_moe_w_up[j], o_moe_w_down[j])
    return h.reshape(bsz, seq, d)
```

```python
import functools

import jax
import jax.numpy as jnp
from jax import lax
from jax.scipy.linalg import block_diag
from jax.experimental import pallas as pl
from jax.experimental.pallas import tpu as pltpu

F32 = jnp.float32
BF16 = jnp.bfloat16
I32 = jnp.int32

EPS = 1e-6
LANES = 128
SUBLANES = 8
MXU_WIDTH = 256
VMEM_LIMIT = 52 * 1024 * 1024

CONV_WIDTH = 4
LRU_C = 8.0
ATT_HEAD_DIM = 64
ATT_WINDOW = 128
ROPE_THETA = 10000.0
GDN_HEAD_DIM = 128
GDN_CHUNK = 64
GDN_PREP_CHUNKS = 2
N_EXPERTS = 8

PROJ_TILE = 512
RGLRU_TILE = 256
SWA_BLOCKS = 2
GDN_TILE = 512
FFN_TILE = 512
FFN_FF_CHUNK = 1408
MOE_TILE = 512
MOE_BLOCK = 512
MOE_FF_CHUNK = 1792


def _params(n_axes, vmem=VMEM_LIMIT):
    return pltpu.CompilerParams(dimension_semantics=("arbitrary",) * n_axes, vmem_limit_bytes=vmem)


def _rms(x, g):
    return x * lax.rsqrt(jnp.mean(x * x, axis=-1, keepdims=True) + EPS) * g


def _sigmoid(x):
    return 0.5 * (1.0 + jnp.tanh(0.5 * x))


def _silu(x):
    return x * _sigmoid(x)


def _dot(a, b):
    return jnp.dot(a, b, preferred_element_type=F32)


def _dot_nt(a, b):
    return lax.dot_general(a, b, (((1,), (1,)), ((), ())), preferred_element_type=F32)


def _split_bf16(x):
    hi = x.astype(BF16)
    lo = (x - hi.astype(F32)).astype(BF16)
    return hi, lo


def _rms_matmul_kernel(x_ref, g_ref, w_ref, o_ref, xn_ref):
    @pl.when(pl.program_id(1) == 0)
    def _():
        xn_ref[...] = _rms(x_ref[...], g_ref[...]).astype(BF16)

    o_ref[...] = _dot(xn_ref[...], w_ref[...])


def rms_matmul(x, g, w, tm, tn):
    t, d = x.shape
    n = w.shape[1]
    return pl.pallas_call(
        _rms_matmul_kernel,
        grid=(t // tm, n // tn),
        in_specs=[pl.BlockSpec((tm, d), lambda i, j: (i, 0)),
                  pl.BlockSpec((1, d), lambda i, j: (0, 0)),
                  pl.BlockSpec((d, tn), lambda i, j: (0, j))],
        out_specs=pl.BlockSpec((tm, tn), lambda i, j: (i, j)),
        out_shape=jax.ShapeDtypeStruct((t, n), F32),
        scratch_shapes=[pltpu.VMEM((tm, d), BF16)],
        compiler_params=_params(2),
    )(x, g.reshape(1, d), w)


def _causal_conv(tail_ref, x, cw, tm):
    prev = tail_ref[...]
    row = lax.broadcasted_iota(I32, prev.shape, 0)
    acc = None
    for j in range(CONV_WIDTH - 1):
        s = CONV_WIDTH - 1 - j
        xs = pltpu.roll(x, s, 0)
        head = jnp.where(row < s, pltpu.roll(prev, s, 0), xs[0:SUBLANES, :])
        term = cw[j:j + 1, :] * jnp.concatenate([head, xs[SUBLANES:, :]], axis=0)
        acc = term if acc is None else acc + term
    tail_ref[...] = x[tm - SUBLANES:tm, :]
    return acc + cw[CONV_WIDTH - 1:CONV_WIDTH, :] * x


def _rglru_kernel(xb_ref, gate_ref, cw_ref, cb_ref, wg_ref, bg_ref, lam_ref, o_ref, tail_ref, h_ref,
                  *, tm, width):
    @pl.when(pl.program_id(1) == 0)
    def _():
        tail_ref[...] = jnp.zeros_like(tail_ref)
        h_ref[...] = jnp.zeros_like(h_ref)

    xc = _causal_conv(tail_ref, xb_ref[...], cw_ref[...], tm) + cb_ref[...]
    z = _dot(xc.astype(BF16), wg_ref[...]) + bg_ref[...]
    r = _sigmoid(z[:, :width])
    gi = _sigmoid(z[:, width:])
    log_a = -LRU_C * r * jax.nn.softplus(-lam_ref[...])
    a = jnp.exp(log_a)
    u = jnp.sqrt(1.0 - a * a) * (gi * xc)

    row = lax.broadcasted_iota(I32, (tm, width), 0) % SUBLANES
    d = 1
    while d < SUBLANES:
        keep = row >= d
        u = u + jnp.where(keep, a * pltpu.roll(u, d, 0), 0.0)
        a = jnp.where(keep, a * pltpu.roll(a, d, 0), a)
        d *= 2
    carry = h_ref[0:1, :]
    groups = []
    for r0 in range(0, tm, SUBLANES):
        hg = u[r0:r0 + SUBLANES, :] + a[r0:r0 + SUBLANES, :] * carry
        carry = hg[SUBLANES - 1:SUBLANES, :]
        groups.append(hg)
    h = jnp.concatenate(groups, axis=0)
    h_ref[...] = jnp.broadcast_to(carry, h_ref.shape)
    o_ref[...] = (h * jax.nn.gelu(gate_ref[...])).astype(o_ref.dtype)


def rglru(proj, cw, cb, wg, bg, lam, bsz, seq, width, tm):
    nt = seq // tm
    kern = functools.partial(_rglru_kernel, tm=tm, width=width)
    row = lambda c: pl.BlockSpec((1, c), lambda b, i: (0, 0))
    return pl.pallas_call(
        kern,
        grid=(bsz, nt),
        in_specs=[pl.BlockSpec((tm, width), lambda b, i: (b * nt + i, 0)),
                  pl.BlockSpec((tm, width), lambda b, i: (b * nt + i, 1)),
                  pl.BlockSpec((CONV_WIDTH, width), lambda b, i: (0, 0)),
                  row(width),
                  pl.BlockSpec((width, 2 * width), lambda b, i: (0, 0)),
                  row(2 * width),
                  row(width)],
        out_specs=pl.BlockSpec((tm, width), lambda b, i: (b * nt + i, 0)),
        out_shape=jax.ShapeDtypeStruct((bsz * seq, width), BF16),
        scratch_shapes=[pltpu.VMEM((SUBLANES, width), F32), pltpu.VMEM((SUBLANES, width), F32)],
        compiler_params=_params(2),
    )(proj, proj, cw, cb.reshape(1, -1), wg, bg.reshape(1, -1), lam.reshape(1, -1))


def _swa_kernel(sink_ref, q_ref, k_ref, v_ref, cos_ref, sin_ref, qg_ref, kg_ref, o_ref, kall_ref, vall_ref,
                *, q_heads, kv_heads, blocks):
    w = ATT_WINDOW
    n = pl.program_id(1)
    group = q_heads // kv_heads
    assert kv_heads * ATT_HEAD_DIM == LANES and group % 2 == 0

    @pl.when(n == 0)
    def _():
        kall_ref[0:w, :] = jnp.zeros((w, LANES), BF16)
        vall_ref[0:w, :] = jnp.zeros((w, LANES), BF16)

    @pl.when(n > 0)
    def _():
        kall_ref[0:w, :] = kall_ref[blocks * w:(blocks + 1) * w, :]
        vall_ref[0:w, :] = vall_ref[blocks * w:(blocks + 1) * w, :]

    lane = lax.broadcasted_iota(I32, (blocks * w, LANES), 1)
    first_half = (lane % ATT_HEAD_DIM) < (ATT_HEAD_DIM // 2)
    lane_head = lax.broadcasted_iota(I32, (w, LANES), 1) // ATT_HEAD_DIM
    bd_r = lax.broadcasted_iota(I32, (LANES, LANES), 0) // ATT_HEAD_DIM
    bd_c = lax.broadcasted_iota(I32, (LANES, LANES), 1) // ATT_HEAD_DIM
    head_ones = (bd_r == bd_c).astype(BF16)
    cos = cos_ref[...]
    sin = sin_ref[...]

    def norm_rope(x, g):
        hi, lo = _split_bf16(x * x)
        ss = _dot(hi, head_ones) + _dot(lo, head_ones)
        xn = x * lax.rsqrt(ss * (1.0 / ATT_HEAD_DIM) + EPS) * g
        rot = jnp.where(first_half, pltpu.roll(xn, LANES - ATT_HEAD_DIM // 2, 1),
                        pltpu.roll(xn, ATT_HEAD_DIM // 2, 1))
        return xn * cos + rot * sin

    kall_ref[w:(blocks + 1) * w, :] = norm_rope(k_ref[...], kg_ref[...]).astype(BF16)
    vall_ref[w:(blocks + 1) * w, :] = v_ref[...].astype(BF16)
    qc = [norm_rope(q_ref[:, LANES * c:LANES * (c + 1)], qg_ref[...]) for c in range(q_heads // 2)]

    rows = group * w
    qi = lax.broadcasted_iota(I32, (rows, 2 * w), 0) % w
    kj = lax.broadcasted_iota(I32, (rows, 2 * w), 1)
    band = (kj > qi) & (kj <= qi + w)
    row_head = lax.broadcasted_iota(I32, (rows, 1), 0) // w

    for sb in range(blocks):
        blk = slice(sb * w, (sb + 1) * w)
        keys = kall_ref[sb * w:(sb + 2) * w, :]
        vals = vall_ref[sb * w:(sb + 2) * w, :]
        valid = (band & ((n > 0) | (kj >= w))) if sb == 0 else band
        outs = []
        for g in range(kv_heads):
            parts = []
            for hh in range(group):
                h = g * group + hh
                x = qc[h // 2][blk, :]
                if h % 2 != g:
                    x = pltpu.roll(x, ATT_HEAD_DIM, 1)
                parts.append(jnp.where(lane_head == g, x, 0.0))
            qs = jnp.concatenate(parts, axis=0).astype(BF16)
            s = _dot_nt(qs, keys) * (ATT_HEAD_DIM ** -0.5)
            s = jnp.where(valid, s, -1e30)
            sink = jnp.zeros((rows, 1), F32)
            for hh in range(group):
                sink = jnp.where(row_head == hh, sink_ref[g * group + hh], sink)
            m = jnp.maximum(jnp.max(s, axis=-1, keepdims=True), sink)
            p = jnp.exp(s - m)
            denom = jnp.sum(p, axis=-1, keepdims=True) + jnp.exp(sink - m)
            pv = _dot(p.astype(BF16), vals) / denom
            outs.extend(pv[w * hh:w * (hh + 1), :] for hh in range(group))

        for c in range(q_heads // 2):
            g = (2 * c) // group
            a, b = outs[2 * c], outs[2 * c + 1]
            if g == 1:
                a = pltpu.roll(a, ATT_HEAD_DIM, 1)
            else:
                b = pltpu.roll(b, ATT_HEAD_DIM, 1)
            o_ref[blk, LANES * c:LANES * (c + 1)] = jnp.where(lane_head == 0, a, b).astype(o_ref.dtype)


def swa(proj, sinks, cos, sin, qg, kg, bsz, seq, q_col, k_col, v_col, q_heads, kv_heads, blocks):
    w = ATT_WINDOW
    r = blocks * w
    nb = seq // r
    qw = q_heads * ATT_HEAD_DIM
    kern = functools.partial(_swa_kernel, q_heads=q_heads, kv_heads=kv_heads, blocks=blocks)
    return pl.pallas_call(
        kern,
        grid=(bsz, nb),
        in_specs=[pl.BlockSpec(memory_space=pltpu.SMEM),
                  pl.BlockSpec((r, qw), lambda b, i: (b * nb + i, q_col // qw)),
                  pl.BlockSpec((r, LANES), lambda b, i: (b * nb + i, k_col // LANES)),
                  pl.BlockSpec((r, LANES), lambda b, i: (b * nb + i, v_col // LANES)),
                  pl.BlockSpec((r, LANES), lambda b, i: (i, 0)),
                  pl.BlockSpec((r, LANES), lambda b, i: (i, 0)),
                  pl.BlockSpec((1, LANES), lambda b, i: (0, 0)),
                  pl.BlockSpec((1, LANES), lambda b, i: (0, 0))],
        out_specs=pl.BlockSpec((r, qw), lambda b, i: (b * nb + i, 0)),
        out_shape=jax.ShapeDtypeStruct((bsz * seq, qw), BF16),
        scratch_shapes=[pltpu.VMEM(((blocks + 1) * w, LANES), BF16), pltpu.VMEM(((blocks + 1) * w, LANES), BF16)],
        compiler_params=_params(2),
    )(sinks, proj, proj, proj, cos, sin, qg, kg)


def _mix_specs(ys, ws, tm, index_map):
    return ([pl.BlockSpec((tm, y.shape[1]), index_map(lambda i: (i, 0))) for y in ys]
            + [pl.BlockSpec(w.shape, index_map(lambda i: (0, 0))) for w in ws])


def _mix_residual(h_ref, mix_refs):
    n_in = len(mix_refs) // 2
    x = h_ref[...]
    for y_ref, w_ref in zip(mix_refs[:n_in], mix_refs[n_in:]):
        x = x + _dot(y_ref[...], w_ref[...])
    return x


def _ffn_kernel(h_ref, *refs):
    *mix_refs, g_ref, wg_ref, wu_ref, wd_ref, o_ref, xn_ref = refs
    j = pl.program_id(1)

    @pl.when(j == 0)
    def _():
        x = _mix_residual(h_ref, mix_refs)
        xn_ref[...] = _rms(x, g_ref[...]).astype(BF16)
        o_ref[...] = x

    xn = xn_ref[...]
    act = (_silu(_dot(xn, wg_ref[...])) * _dot(xn, wu_ref[...])).astype(BF16)
    o_ref[...] += _dot(act, wd_ref[...])


def ffn(h, ys, ws, g, wg, wu, wd, tm, tf):
    t, d = h.shape
    f = wg.shape[1]
    return pl.pallas_call(
        _ffn_kernel,
        grid=(t // tm, f // tf),
        in_specs=([pl.BlockSpec((tm, d), lambda i, j: (i, 0))]
                  + _mix_specs(ys, ws, tm, lambda fn: lambda i, j: fn(i))
                  + [pl.BlockSpec((1, d), lambda i, j: (0, 0)),
                     pl.BlockSpec((d, tf), lambda i, j: (0, j)),
                     pl.BlockSpec((d, tf), lambda i, j: (0, j)),
                     pl.BlockSpec((tf, d), lambda i, j: (j, 0))]),
        out_specs=pl.BlockSpec((tm, d), lambda i, j: (i, 0)),
        out_shape=jax.ShapeDtypeStruct((t, d), F32),
        scratch_shapes=[pltpu.VMEM((tm, d), BF16)],
        compiler_params=_params(2),
    )(h, *ys, *ws, g.reshape(1, d), wg, wu, wd)


def _gdn_kernel(qkv_ref, gate_ref, ab_ref, cw_ref, alog_ref, dtb_ref, onorm_ref, o_ref,
                tail_ref, q_ref, k_ref, v_ref, gc_ref, beta_ref, uw_ref, aqk_ref, qd_ref, kd_ref, *state_refs,
                tc, heads):
    hd = GDN_HEAD_DIM
    c = GDN_CHUNK
    width = heads * hd

    @pl.when(pl.program_id(1) == 0)
    def _():
        tail_ref[...] = jnp.zeros_like(tail_ref)
        for s_ref in state_refs:
            s_ref[...] = jnp.zeros_like(s_ref)

    qkv = _silu(_causal_conv(tail_ref, qkv_ref[...], cw_ref[...], tc))
    for h in range(heads):
        q = qkv[:, hd * h:hd * (h + 1)]
        k = qkv[:, width + hd * h:width + hd * (h + 1)]
        q_ref[:, hd * h:hd * (h + 1)] = (q * lax.rsqrt(jnp.sum(q * q, axis=-1, keepdims=True) + EPS)
                                          * (hd ** -0.5))
        k_ref[:, hd * h:hd * (h + 1)] = k * lax.rsqrt(jnp.sum(k * k, axis=-1, keepdims=True) + EPS)
    v_ref[...] = qkv[:, 2 * width:]

    ab = ab_ref[...]
    g = -jnp.exp(alog_ref[...]) * jax.nn.softplus(ab + dtb_ref[...])
    beta_ref[...] = pltpu.roll(jax.nn.sigmoid(ab), LANES - heads, 1)
    row = lax.broadcasted_iota(I32, (tc, LANES), 0) % c
    d = 1
    while d < c:
        g = g + jnp.where(row >= d, pltpu.roll(g, d, 0), 0.0)
        d *= 2
    gc_ref[...] = g

    ri = lax.broadcasted_iota(I32, (c, c), 0)
    ci = lax.broadcasted_iota(I32, (c, c), 1)
    tri = ri >= ci
    strict = ri > ci
    eye = (ri == ci).astype(F32)

    hs = range(heads)
    cols = [slice(hd * h, hd * (h + 1)) for h in hs]
    n_chunks = tc // c
    rows = [slice(c * n, c * (n + 1)) for n in range(n_chunks)]

    for g0 in range(0, n_chunks, GDN_PREP_CHUNKS):
        items = [(n, h) for n in range(g0, min(g0 + GDN_PREP_CHUNKS, n_chunks)) for h in hs]
        gc_all = {n: gc_ref[rows[n], :] for n, _ in items}
        gr_all = {n: gc_all[n].T for n in gc_all}
        beta_all = {n: beta_ref[rows[n], :] for n in gc_all}
        gcol = {(n, h): gc_all[n][:, h:h + 1] for n, h in items}
        grow = {(n, h): gr_all[n][h:h + 1, :] for n, h in items}
        beta = {(n, h): beta_all[n][:, h:h + 1] for n, h in items}
        qk_kk = {(n, h): _dot_nt(jnp.concatenate([q_ref[rows[n], cols[h]], k_ref[rows[n], cols[h]]],
                                                 axis=0).astype(BF16), k_ref[rows[n], cols[h]].astype(BF16))
                 for n, h in items}
        decay = {it: jnp.where(tri, jnp.exp(jnp.where(tri, gcol[it] - grow[it], 0.0)), 0.0) for it in items}
        m = {it: -jnp.where(strict, beta[it] * qk_kk[it][c:, :] * decay[it], 0.0) for it in items}
        p = {it: eye + m[it] for it in items}
        e = {it: _dot(m[it].astype(BF16), m[it].astype(BF16)) for it in items}
        lvl = 2
        while 2 * lvl < c:
            r = {it: _dot(jnp.concatenate([p[it], e[it]], axis=0).astype(BF16), e[it].astype(BF16)) for it in items}
            p = {it: p[it] + r[it][:c, :] for it in items}
            e = {it: r[it][c:, :] for it in items}
            lvl *= 2
        p = {it: p[it] + _dot(p[it].astype(BF16), e[it].astype(BF16)) for it in items}
        for n, h in items:
            it = (n, h)
            k = k_ref[rows[n], cols[h]]
            eg = jnp.exp(gcol[it])
            rhs = jnp.concatenate([beta[it] * v_ref[rows[n], cols[h]], (beta[it] * eg) * k], axis=1)
            uw_ref[rows[n], 2 * hd * h:2 * hd * (h + 1)] = _dot(p[it].astype(BF16), rhs.astype(BF16))
            aqk_ref[rows[n], LANES * h:LANES * h + c] = jnp.where(tri, qk_kk[it][:c, :] * decay[it], 0.0)
            qd_ref[rows[n], cols[h]] = q_ref[rows[n], cols[h]] * eg
            kd_ref[rows[n], cols[h]] = k * jnp.exp(gcol[it][c - 1:c, :] - gcol[it])

    for n in range(n_chunks):
        gc_n = gc_ref[rows[n], :]
        g_last = [jnp.exp(gc_n[c - 1:c, h:h + 1]) for h in hs]
        s = [state_refs[h][...] for h in hs]
        ws_qs = [_dot(jnp.concatenate([uw_ref[rows[n], 2 * hd * h + hd:2 * hd * (h + 1)], qd_ref[rows[n], cols[h]]],
                                      axis=0).astype(BF16), s[h].astype(BF16)) for h in hs]
        vb = [(uw_ref[rows[n], 2 * hd * h:2 * hd * h + hd] - ws_qs[h][:c, :]).astype(BF16) for h in hs]
        o = [ws_qs[h][c:, :] + _dot(aqk_ref[rows[n], LANES * h:LANES * h + c].astype(BF16), vb[h]) for h in hs]
        s_new = [s[h] * g_last[h] + _dot(kd_ref[rows[n], cols[h]].T.astype(BF16), vb[h]) for h in hs]
        for h in hs:
            state_refs[h][...] = s_new[h]
            on = o[h] * lax.rsqrt(jnp.mean(o[h] * o[h], axis=-1, keepdims=True) + EPS) * onorm_ref[...]
            o_ref[rows[n], cols[h]] = (on * _silu(gate_ref[rows[n], cols[h]])).astype(o_ref.dtype)


def gdn(proj, cw, alog, dtb, onorm, bsz, seq, heads, tc):
    nt = seq // tc
    width = heads * GDN_HEAD_DIM
    kern = functools.partial(_gdn_kernel, tc=tc, heads=heads)
    row = lambda n: pl.BlockSpec((1, n), lambda b, i: (0, 0))
    return pl.pallas_call(
        kern,
        grid=(bsz, nt),
        in_specs=[pl.BlockSpec((tc, 3 * width), lambda b, i: (b * nt + i, 0)),
                  pl.BlockSpec((tc, width), lambda b, i: (b * nt + i, 3)),
                  pl.BlockSpec((tc, LANES), lambda b, i: (b * nt + i, 4 * width // LANES)),
                  pl.BlockSpec((CONV_WIDTH, 3 * width), lambda b, i: (0, 0)),
                  row(LANES), row(LANES), row(GDN_HEAD_DIM)],
        out_specs=pl.BlockSpec((tc, width), lambda b, i: (b * nt + i, 0)),
        out_shape=jax.ShapeDtypeStruct((bsz * seq, width), BF16),
        scratch_shapes=[pltpu.VMEM((SUBLANES, 3 * width), F32),
                        pltpu.VMEM((tc, width), F32), pltpu.VMEM((tc, width), F32),
                        pltpu.VMEM((tc, width), F32),
                        pltpu.VMEM((tc, LANES), F32), pltpu.VMEM((tc, LANES), F32),
                        pltpu.VMEM((tc, 2 * width), F32), pltpu.VMEM((tc, heads * LANES), F32),
                        pltpu.VMEM((tc, width), F32), pltpu.VMEM((tc, width), F32),
                        *[pltpu.VMEM((GDN_HEAD_DIM, GDN_HEAD_DIM), F32) for _ in range(heads)]],
        compiler_params=_params(2),
    )(proj, proj, proj, cw, alog, dtb, onorm.reshape(1, -1))


def _router_kernel(h_ref, *refs, tm):
    *mix_refs, g_ref, r_ref, x_ref, route_ref, cnt_ref = refs
    x = _mix_residual(h_ref, mix_refs)
    x_ref[...] = x
    t_hi, t_lo = _split_bf16(_rms(x, g_ref[...]))
    r_hi, r_lo = _split_bf16(r_ref[...])
    logits = _dot(t_hi, r_hi) + (_dot(t_lo, r_hi) + _dot(t_hi, r_lo))
    lane = lax.broadcasted_iota(I32, (tm, LANES), 1)
    lane_f = lane.astype(F32)
    neg = -jnp.inf
    lg = jnp.where(lane < N_EXPERTS, logits, neg)
    m1 = jnp.max(lg, axis=-1, keepdims=True)
    i1 = jnp.min(jnp.where(lg == m1, lane_f, float(LANES)), axis=-1, keepdims=True)
    oh1 = lane_f == i1
    lg2 = jnp.where(oh1, neg, lg)
    m2 = jnp.max(lg2, axis=-1, keepdims=True)
    i2 = jnp.min(jnp.where(lg2 == m2, lane_f, float(LANES)), axis=-1, keepdims=True)
    oh2 = lane_f == i2
    e2 = jnp.exp(m2 - m1)
    w1 = 1.0 / (1.0 + e2)
    w2 = e2 / (1.0 + e2)

    both = (oh1 | oh2).astype(BF16)
    ri = lax.broadcasted_iota(I32, (tm, tm), 0)
    ci = lax.broadcasted_iota(I32, (tm, tm), 1)
    pos = _dot((ri > ci).astype(BF16), both)
    rank1 = jnp.sum(jnp.where(oh1, pos, 0.0), axis=-1, keepdims=True)
    rank2 = jnp.sum(jnp.where(oh2, pos, 0.0), axis=-1, keepdims=True)
    cnt_ref[0] = jnp.broadcast_to(jnp.sum(both.astype(F32), axis=0, keepdims=True), (SUBLANES, LANES))

    out = jnp.where(lane == 0, i1, 0.0)
    out = jnp.where(lane == 1, i2, out)
    out = jnp.where(lane == 2, rank1, out)
    out = jnp.where(lane == 3, rank2, out)
    out = jnp.where(lane == 4, w1, out)
    route_ref[...] = jnp.where(lane == 5, w2, out)


def router(h, ys, ws, g, r, tm):
    t, d = h.shape
    kern = functools.partial(_router_kernel, tm=tm)
    return pl.pallas_call(
        kern,
        grid=(t // tm,),
        in_specs=([pl.BlockSpec((tm, d), lambda i: (i, 0))]
                  + _mix_specs(ys, ws, tm, lambda fn: fn)
                  + [pl.BlockSpec((1, d), lambda i: (0, 0)),
                     pl.BlockSpec((d, LANES), lambda i: (0, 0))]),
        out_specs=[pl.BlockSpec((tm, d), lambda i: (i, 0)),
                   pl.BlockSpec((tm, LANES), lambda i: (i, 0)),
                   pl.BlockSpec((1, SUBLANES, LANES), lambda i: (i, 0, 0))],
        out_shape=[jax.ShapeDtypeStruct((t, d), F32),
                   jax.ShapeDtypeStruct((t, LANES), F32),
                   jax.ShapeDtypeStruct((t // tm, SUBLANES, LANES), F32)],
        compiler_params=_params(1),
    )(h, *ys, *ws, g.reshape(1, d), r)


def _local_positions(route, off_row):
    lane_f = lax.broadcasted_iota(I32, route.shape, 1).astype(F32)
    lpos1 = jnp.sum(jnp.where(lane_f == route[:, 0:1], off_row, 0.0), axis=-1, keepdims=True) + route[:, 2:3]
    lpos2 = jnp.sum(jnp.where(lane_f == route[:, 1:2], off_row, 0.0), axis=-1, keepdims=True) + route[:, 3:4]
    return lpos1, lpos2


def _run_copies(n_rows, max_bit, make):
    out = []
    for k in range(max_bit, -1, -1):
        bit = k + 3
        offset = pl.multiple_of((n_rows >> (bit + 1)) << (bit + 1), SUBLANES)
        out.append((((n_rows >> bit) & 1) == 1, make(offset, SUBLANES << k)))
    return out


def _start_all(copies):
    for pred, cp in copies:
        pl.when(pred)(cp.start)


def _wait_all(copies):
    for pred, cp in copies:
        pl.when(pred)(cp.wait)


def _dispatch_kernel(seg_ref, cnt_ref, off_ref, tail_ref, h_ref, g_ref, route_ref, offrow_ref, x_hbm,
                     local, zeros, sem, zsem, *, tm, max_bit, tile_rows):
    b = pl.program_id(0)
    nb = pl.num_programs(0)
    slot = b % 2

    def block_copies(blk, s):
        out = []
        for e in range(N_EXPERTS):
            seg, off = seg_ref[blk * N_EXPERTS + e], off_ref[blk * N_EXPERTS + e]
            out += _run_copies(cnt_ref[blk * N_EXPERTS + e], max_bit, lambda o, n, seg=seg, off=off:
                               pltpu.make_async_copy(local.at[s, pl.ds(pl.multiple_of(off + o, SUBLANES), n)],
                                                     x_hbm.at[pl.ds(pl.multiple_of(seg + o, SUBLANES), n)],
                                                     sem.at[s]))
        return out

    def tail_copies():
        out = []
        for e in range(N_EXPERTS):
            start = tail_ref[e]
            out += _run_copies(tail_ref[N_EXPERTS + e], max_bit - 1, lambda o, n, start=start:
                               pltpu.make_async_copy(zeros.at[pl.ds(0, n)],
                                                     x_hbm.at[pl.ds(pl.multiple_of(start + o, SUBLANES), n)], zsem))
        return out

    @pl.when(b >= 2)
    def _():
        _wait_all(block_copies(b - 2, slot))

    t = _rms(h_ref[...], g_ref[...]).astype(BF16)
    lpos1, lpos2 = _local_positions(route_ref[...], offrow_ref[0])
    rows_f = lax.broadcasted_iota(I32, (tm, local.shape[1]), 1).astype(F32)
    place = ((rows_f == lpos1) | (rows_f == lpos2)).astype(BF16)
    local[slot] = lax.dot_general(place, t, (((0,), (0,)), ((), ())), preferred_element_type=F32)
    _start_all(block_copies(b, slot))

    @pl.when(b == nb - 1)
    def _():
        zeros[...] = jnp.zeros_like(zeros)
        _start_all(tail_copies())

        @pl.when(b >= 1)
        def _():
            _wait_all(block_copies(b - 1, 1 - slot))

        _wait_all(block_copies(b, slot))
        _wait_all(tail_copies())

        def fill_idle_tile(tile, carry):
            cp = pltpu.make_async_copy(zeros, x_hbm.at[pl.ds(pl.multiple_of(tile * tile_rows, tile_rows), tile_rows)],
                                       zsem)
            cp.start()
            cp.wait()
            return carry
        lax.fori_loop(tail_ref[2 * N_EXPERTS], x_hbm.shape[0] // tile_rows, fill_idle_tile, 0)


def moe_dispatch(seg, cnt8, off, tail, h, g, route, off_rows, n_rows, tm, local_rows, tile_rows):
    t, d = h.shape
    max_bit = (tm // SUBLANES).bit_length() - 1
    assert tile_rows >= tm // 2 and n_rows % tile_rows == 0
    kern = functools.partial(_dispatch_kernel, tm=tm, max_bit=max_bit, tile_rows=tile_rows)
    grid_spec = pltpu.PrefetchScalarGridSpec(
        num_scalar_prefetch=4,
        grid=(t // tm,),
        in_specs=[pl.BlockSpec((tm, d), lambda i, *_: (i, 0)),
                  pl.BlockSpec((1, d), lambda i, *_: (0, 0)),
                  pl.BlockSpec((tm, LANES), lambda i, *_: (i, 0)),
                  pl.BlockSpec((1, 1, LANES), lambda i, *_: (i, 0, 0))],
        out_specs=pl.BlockSpec(memory_space=pl.ANY),
        scratch_shapes=[pltpu.VMEM((2, local_rows, d), F32), pltpu.VMEM((tile_rows, d), F32),
                        pltpu.SemaphoreType.DMA((2,)), pltpu.SemaphoreType.DMA(())],
    )
    return pl.pallas_call(
        kern,
        grid_spec=grid_spec,
        out_shape=jax.ShapeDtypeStruct((n_rows, d), F32),
        compiler_params=_params(1),
    )(seg, cnt8, off, tail, h, g.reshape(1, d), route, off_rows)


def _moe_kernel(te_ref, nu_ref, x_ref, wg_ref, wu_ref, wd_ref, y_ref, xn_ref):
    i = pl.program_id(0)
    j = pl.program_id(1)
    n_used = nu_ref[0]

    @pl.when(i < n_used)
    def _():
        @pl.when(j == 0)
        def _():
            xn_ref[...] = x_ref[...].astype(BF16)

        xn = xn_ref[...]
        act = (_silu(_dot(xn, wg_ref[0])) * _dot(xn, wu_ref[0])).astype(BF16)
        part = _dot(act, wd_ref[0])

        @pl.when(j == 0)
        def _():
            y_ref[...] = part

        @pl.when(j > 0)
        def _():
            y_ref[...] += part

    @pl.when((i >= n_used) & (j == 0))
    def _():
        y_ref[...] = jnp.zeros_like(y_ref)


def moe_experts(tile_expert, n_used, x, wg, wu, wd, tm, tf):
    n_rows, d = x.shape
    f = wg.shape[2]
    n_ff = f // tf

    def ff_idx(i, j, nu):
        return jnp.where(i < nu[0], j, n_ff - 1)

    grid_spec = pltpu.PrefetchScalarGridSpec(
        num_scalar_prefetch=2,
        grid=(n_rows // tm, n_ff),
        in_specs=[pl.BlockSpec((tm, d), lambda i, j, te, nu: (jnp.minimum(i, nu[0] - 1), 0)),
                  pl.BlockSpec((1, d, tf), lambda i, j, te, nu: (te[i], 0, ff_idx(i, j, nu))),
                  pl.BlockSpec((1, d, tf), lambda i, j, te, nu: (te[i], 0, ff_idx(i, j, nu))),
                  pl.BlockSpec((1, tf, d), lambda i, j, te, nu: (te[i], ff_idx(i, j, nu), 0))],
        out_specs=pl.BlockSpec((tm, d), lambda i, j, te, nu: (i, 0)),
        scratch_shapes=[pltpu.VMEM((tm, d), BF16)],
    )
    return pl.pallas_call(
        _moe_kernel,
        grid_spec=grid_spec,
        out_shape=jax.ShapeDtypeStruct((n_rows, d), F32),
        compiler_params=_params(2),
    )(tile_expert, n_used, x, wg, wu, wd)


def _combine_kernel(seg_ref, cnt_ref, off_ref, h_ref, route_ref, offrow_ref, y_hbm, o_ref, local, sem,
                    *, tm, max_bit):
    b = pl.program_id(0)
    nb = pl.num_programs(0)
    slot = b % 2

    def block_copies(blk, s):
        out = []
        for e in range(N_EXPERTS):
            seg, off = seg_ref[blk * N_EXPERTS + e], off_ref[blk * N_EXPERTS + e]
            out += _run_copies(cnt_ref[blk * N_EXPERTS + e], max_bit, lambda o, n, seg=seg, off=off:
                               pltpu.make_async_copy(y_hbm.at[pl.ds(pl.multiple_of(seg + o, SUBLANES), n)],
                                                     local.at[s, pl.ds(pl.multiple_of(off + o, SUBLANES), n)],
                                                     sem.at[s]))
        return out

    @pl.when(b == 0)
    def _():
        local[...] = jnp.zeros_like(local)
        _start_all(block_copies(0, 0))

    _wait_all(block_copies(b, slot))

    @pl.when(b + 1 < nb)
    def _():
        _start_all(block_copies(b + 1, 1 - slot))

    route = route_ref[...]
    lpos1, lpos2 = _local_positions(route, offrow_ref[0])
    rows_f = lax.broadcasted_iota(I32, (tm, local.shape[1]), 1).astype(F32)
    gate = jnp.where(rows_f == lpos1, route[:, 4:5], 0.0) + jnp.where(rows_f == lpos2, route[:, 5:6], 0.0)
    g_hi, g_lo = _split_bf16(gate)
    y = local[slot].astype(BF16)
    o_ref[...] = h_ref[...] + (_dot(g_hi, y) + _dot(g_lo, y))


def moe_combine(seg, cnt8, off, h, route, off_rows, y, tm, local_rows):
    t, d = h.shape
    max_bit = (tm // SUBLANES).bit_length() - 1
    kern = functools.partial(_combine_kernel, tm=tm, max_bit=max_bit)
    grid_spec = pltpu.PrefetchScalarGridSpec(
        num_scalar_prefetch=3,
        grid=(t // tm,),
        in_specs=[pl.BlockSpec((tm, d), lambda i, *_: (i, 0)),
                  pl.BlockSpec((tm, LANES), lambda i, *_: (i, 0)),
                  pl.BlockSpec((1, 1, LANES), lambda i, *_: (i, 0, 0)),
                  pl.BlockSpec(memory_space=pl.ANY)],
        out_specs=pl.BlockSpec((tm, d), lambda i, *_: (i, 0)),
        scratch_shapes=[pltpu.VMEM((2, local_rows, d), F32), pltpu.SemaphoreType.DMA((2,))],
    )
    return pl.pallas_call(
        kern,
        grid_spec=grid_spec,
        out_shape=jax.ShapeDtypeStruct((t, d), F32),
        compiler_params=_params(1),
    )(seg, cnt8, off, h, route, off_rows, y)


def _rope_tables(seq):
    half = ATT_HEAD_DIM // 2
    inv_freq = ROPE_THETA ** (-jnp.arange(half, dtype=F32) / half)
    ang = jnp.arange(seq, dtype=jnp.int32).astype(F32)[:, None] * inv_freq[None, :]
    reps = LANES // half
    cos = jnp.tile(jnp.cos(ang), (1, reps))
    sign = jnp.tile(jnp.concatenate([-jnp.ones((half,), F32), jnp.ones((half,), F32)]), LANES // ATT_HEAD_DIM)
    sin = jnp.tile(jnp.sin(ang), (1, reps)) * sign[None, :]
    return cos, sin


def _even_layer(h, bsz, seq, ln_mix, ln_ffn, w_in, conv_w, conv_b, w_a, b_a, w_i, b_i, lam,
                q_norm, k_norm, sinks, w_out, f_gate, f_up, f_down):
    lru_w = conv_w.shape[1]
    q_heads = sinks.shape[0]
    q_w = q_heads * ATT_HEAD_DIM
    kv_w = (w_in.shape[1] - 2 * lru_w - q_w) // 2
    kv_heads = kv_w // ATT_HEAD_DIM

    proj = rms_matmul(h, ln_mix, w_in.astype(BF16), tm=PROJ_TILE, tn=w_in.shape[1])
    gates_w = jnp.concatenate([block_diag(*w_a), block_diag(*w_i)], axis=1).astype(BF16)
    y_lru = rglru(proj, conv_w, conv_b, gates_w, jnp.concatenate([b_a, b_i]), lam, bsz, seq, lru_w,
                  tm=RGLRU_TILE)
    cos, sin = _rope_tables(seq)
    tile2 = lambda g: jnp.tile(g, LANES // ATT_HEAD_DIM).reshape(1, LANES)
    y_att = swa(proj, sinks, cos, sin, tile2(q_norm), tile2(k_norm), bsz, seq,
                q_col=2 * lru_w, k_col=2 * lru_w + q_w, v_col=2 * lru_w + q_w + kv_w,
                q_heads=q_heads, kv_heads=kv_heads, blocks=SWA_BLOCKS)
    w_out = w_out.astype(BF16)
    return ffn(h, [y_lru, y_att], [w_out[:lru_w], w_out[lru_w:]], ln_ffn, f_gate.astype(BF16), f_up.astype(BF16),
               f_down.astype(BF16), tm=FFN_TILE, tf=FFN_FF_CHUNK)


def _odd_layer(h, bsz, seq, ln_mix, ln_ffn, w_in, conv_w, a_log, dt_bias, out_norm, w_out, router_w,
               m_gate, m_up, m_down):
    t, d = h.shape
    heads = a_log.shape[0]
    width = heads * GDN_HEAD_DIM
    cols = w_in.shape[1]
    pad = (-cols) % (11 * LANES)
    w_in_p = jnp.pad(w_in, ((0, 0), (0, pad))).astype(BF16)
    proj = rms_matmul(h, ln_mix, w_in_p, tm=PROJ_TILE, tn=w_in_p.shape[1])
    lane_pad = lambda v: jnp.pad(v, (0, LANES - heads)).reshape(1, LANES)
    y = gdn(proj, conv_w, lane_pad(a_log), lane_pad(dt_bias), out_norm, bsz, seq, heads, tc=GDN_TILE)
    tb = MOE_BLOCK
    tm = MOE_TILE
    nblk = t // tb
    h, route, cnt = router(h, [y], [w_out.astype(BF16)], ln_ffn,
                           jnp.pad(router_w, ((0, 0), (0, LANES - N_EXPERTS))), tm=tb)
    cnt8 = (cnt[:, 0, :N_EXPERTS].astype(I32) + SUBLANES - 1) // SUBLANES * SUBLANES
    total8 = jnp.sum(cnt8, axis=0)
    tiles_per = (total8 + tm - 1) // tm
    tile_end = jnp.cumsum(tiles_per)
    starts = (tile_end - tiles_per) * tm
    n_used = tile_end[-1:]
    n_tiles = (2 * t + nblk * N_EXPERTS * (SUBLANES - 1)) // tm + N_EXPERTS
    tile_ids = jnp.minimum(jnp.arange(n_tiles, dtype=I32), n_used - 1)
    tile_expert = jnp.sum((tile_ids[:, None] >= tile_end[None, :]).astype(I32), axis=1)
    seg = (starts[None, :] + jnp.cumsum(cnt8, axis=0) - cnt8).reshape(-1)
    off = jnp.cumsum(cnt8, axis=1) - cnt8
    off_rows = jnp.pad(off.astype(F32), ((0, 0), (0, LANES - N_EXPERTS))).reshape(nblk, 1, LANES)
    tail = jnp.concatenate([starts + total8, tiles_per * tm - total8, n_used])
    local_rows = -(-(2 * tb + N_EXPERTS * (SUBLANES - 1)) // LANES) * LANES
    cnt8, off = cnt8.reshape(-1), off.reshape(-1)

    x = moe_dispatch(seg, cnt8, off, tail, h, ln_ffn, route, off_rows, n_tiles * tm, tb, local_rows, tm)
    y = moe_experts(tile_expert, n_used, x, m_gate.astype(BF16), m_up.astype(BF16), m_down.astype(BF16),
                    tm, MOE_FF_CHUNK)
    return moe_combine(seg, cnt8, off, h, route, off_rows, y, tb, local_rows)


def kernel(x, ln_mix, ln_ffn, e_w_in, e_lru_conv_w, e_lru_conv_b, e_lru_w_a, e_lru_b_a, e_lru_w_i, e_lru_b_i, e_lru_lambda, e_q_norm, e_k_norm, e_sinks, e_w_out, e_ffn_w_gate, e_ffn_w_up, e_ffn_w_down, o_w_in, o_conv_w, o_a_log, o_dt_bias, o_out_norm, o_w_out, o_router, o_moe_w_gate, o_moe_w_up, o_moe_w_down):
    bsz, seq, d = x.shape
    h = x.reshape(bsz * seq, d)
    for layer in range(ln_mix.shape[0]):
        j = layer // 2
        if layer % 2 == 0:
            h = _even_layer(h, bsz, seq, ln_mix[layer], ln_ffn[layer], e_w_in[j], e_lru_conv_w[j],
                            e_lru_conv_b[j], e_lru_w_a[j], e_lru_b_a[j], e_lru_w_i[j], e_lru_b_i[j],
                            e_lru_lambda[j], e_q_norm[j], e_k_norm[j], e_sinks[j], e_w_out[j],
                            e_ffn_w_gate[j], e_ffn_w_up[j], e_ffn_w_down[j])
        else:
            h = _odd_layer(h, bsz, seq, ln_mix[layer], ln_ffn[layer], o_w_in[j], o_conv_w[j], o_a_log[j],
                           o_dt_bias[j], o_out_norm[j], o_w_out[j], o_router[j], o_moe_w_gate[j],
                           o_moe_w_up[j], o_moe_w_down[j])
    return h.reshape(bsz, seq, d)
```

```python
import functools

import jax
import jax.numpy as jnp
from jax import lax
from jax.scipy.linalg import block_diag
from jax.experimental import pallas as pl
from jax.experimental.pallas import tpu as pltpu

F32 = jnp.float32
BF16 = jnp.bfloat16
I32 = jnp.int32

EPS = 1e-6
LANES = 128
SUBLANES = 8
MXU_WIDTH = 256
VMEM_LIMIT = 52 * 1024 * 1024

CONV_WIDTH = 4
LRU_C = 8.0
ATT_HEAD_DIM = 64
ATT_WINDOW = 128
ROPE_THETA = 10000.0
GDN_HEAD_DIM = 128
GDN_CHUNK = 64
GDN_PREP_CHUNKS = 2
N_EXPERTS = 8

PROJ_TILE = 512
RGLRU_TILE = 256
SWA_BLOCKS = 2
GDN_TILE = 512
FFN_TILE = 512
FFN_FF_CHUNK = 2816
MOE_TILE = 512
MOE_BLOCK = 512
MOE_FF_CHUNK = 1792


def _params(n_axes, vmem=VMEM_LIMIT):
    return pltpu.CompilerParams(dimension_semantics=("arbitrary",) * n_axes, vmem_limit_bytes=vmem)


def _rms(x, g):
    return x * lax.rsqrt(jnp.mean(x * x, axis=-1, keepdims=True) + EPS) * g


def _sigmoid(x):
    return 0.5 * (1.0 + jnp.tanh(0.5 * x))


def _silu(x):
    return x * _sigmoid(x)


def _dot(a, b):
    return jnp.dot(a, b, preferred_element_type=F32)


def _dot_nt(a, b):
    return lax.dot_general(a, b, (((1,), (1,)), ((), ())), preferred_element_type=F32)


def _split_bf16(x):
    hi = x.astype(BF16)
    lo = (x - hi.astype(F32)).astype(BF16)
    return hi, lo


def _rms_matmul_kernel(x_ref, g_ref, w_ref, o_ref, xn_ref):
    @pl.when(pl.program_id(1) == 0)
    def _():
        xn_ref[...] = _rms(x_ref[...], g_ref[...]).astype(BF16)

    o_ref[...] = _dot(xn_ref[...], w_ref[...])


def rms_matmul(x, g, w, tm, tn):
    t, d = x.shape
    n = w.shape[1]
    return pl.pallas_call(
        _rms_matmul_kernel,
        grid=(t // tm, n // tn),
        in_specs=[pl.BlockSpec((tm, d), lambda i, j: (i, 0)),
                  pl.BlockSpec((1, d), lambda i, j: (0, 0)),
                  pl.BlockSpec((d, tn), lambda i, j: (0, j))],
        out_specs=pl.BlockSpec((tm, tn), lambda i, j: (i, j)),
        out_shape=jax.ShapeDtypeStruct((t, n), F32),
        scratch_shapes=[pltpu.VMEM((tm, d), BF16)],
        compiler_params=_params(2),
    )(x, g.reshape(1, d), w)


def _causal_conv(tail_ref, x, cw, tm):
    prev = tail_ref[...]
    row = lax.broadcasted_iota(I32, prev.shape, 0)
    acc = None
    for j in range(CONV_WIDTH - 1):
        s = CONV_WIDTH - 1 - j
        xs = pltpu.roll(x, s, 0)
        head = jnp.where(row < s, pltpu.roll(prev, s, 0), xs[0:SUBLANES, :])
        term = cw[j:j + 1, :] * jnp.concatenate([head, xs[SUBLANES:, :]], axis=0)
        acc = term if acc is None else acc + term
    tail_ref[...] = x[tm - SUBLANES:tm, :]
    return acc + cw[CONV_WIDTH - 1:CONV_WIDTH, :] * x


def _rglru_kernel(xb_ref, gate_ref, cw_ref, cb_ref, wg_ref, bg_ref, lam_ref, o_ref, tail_ref, h_ref,
                  *, tm, width):
    @pl.when(pl.program_id(1) == 0)
    def _():
        tail_ref[...] = jnp.zeros_like(tail_ref)
        h_ref[...] = jnp.zeros_like(h_ref)

    xc = _causal_conv(tail_ref, xb_ref[...], cw_ref[...], tm) + cb_ref[...]
    z = _dot(xc.astype(BF16), wg_ref[...]) + bg_ref[...]
    r = _sigmoid(z[:, :width])
    gi = _sigmoid(z[:, width:])
    log_a = -LRU_C * r * jax.nn.softplus(-lam_ref[...])
    a = jnp.exp(log_a)
    u = jnp.sqrt(1.0 - a * a) * (gi * xc)

    row = lax.broadcasted_iota(I32, (tm, width), 0) % SUBLANES
    d = 1
    while d < SUBLANES:
        keep = row >= d
        u = u + jnp.where(keep, a * pltpu.roll(u, d, 0), 0.0)
        a = jnp.where(keep, a * pltpu.roll(a, d, 0), a)
        d *= 2
    carry = h_ref[0:1, :]
    groups = []
    for r0 in range(0, tm, SUBLANES):
        hg = u[r0:r0 + SUBLANES, :] + a[r0:r0 + SUBLANES, :] * carry
        carry = hg[SUBLANES - 1:SUBLANES, :]
        groups.append(hg)
    h = jnp.concatenate(groups, axis=0)
    h_ref[...] = jnp.broadcast_to(carry, h_ref.shape)
    o_ref[...] = (h * jax.nn.gelu(gate_ref[...])).astype(o_ref.dtype)


def rglru(proj, cw, cb, wg, bg, lam, bsz, seq, width, tm):
    nt = seq // tm
    kern = functools.partial(_rglru_kernel, tm=tm, width=width)
    row = lambda c: pl.BlockSpec((1, c), lambda b, i: (0, 0))
    return pl.pallas_call(
        kern,
        grid=(bsz, nt),
        in_specs=[pl.BlockSpec((tm, width), lambda b, i: (b * nt + i, 0)),
                  pl.BlockSpec((tm, width), lambda b, i: (b * nt + i, 1)),
                  pl.BlockSpec((CONV_WIDTH, width), lambda b, i: (0, 0)),
                  row(width),
                  pl.BlockSpec((width, 2 * width), lambda b, i: (0, 0)),
                  row(2 * width),
                  row(width)],
        out_specs=pl.BlockSpec((tm, width), lambda b, i: (b * nt + i, 0)),
        out_shape=jax.ShapeDtypeStruct((bsz * seq, width), BF16),
        scratch_shapes=[pltpu.VMEM((SUBLANES, width), F32), pltpu.VMEM((SUBLANES, width), F32)],
        compiler_params=_params(2),
    )(proj, proj, cw, cb.reshape(1, -1), wg, bg.reshape(1, -1), lam.reshape(1, -1))


def _swa_kernel(sink_ref, q_ref, k_ref, v_ref, cos_ref, sin_ref, qg_ref, kg_ref, o_ref, kall_ref, vall_ref,
                *, q_heads, kv_heads, blocks):
    w = ATT_WINDOW
    n = pl.program_id(1)
    group = q_heads // kv_heads
    assert kv_heads * ATT_HEAD_DIM == LANES and group % 2 == 0

    @pl.when(n == 0)
    def _():
        kall_ref[0:w, :] = jnp.zeros((w, LANES), BF16)
        vall_ref[0:w, :] = jnp.zeros((w, LANES), BF16)

    @pl.when(n > 0)
    def _():
        kall_ref[0:w, :] = kall_ref[blocks * w:(blocks + 1) * w, :]
        vall_ref[0:w, :] = vall_ref[blocks * w:(blocks + 1) * w, :]

    lane = lax.broadcasted_iota(I32, (blocks * w, LANES), 1)
    first_half = (lane % ATT_HEAD_DIM) < (ATT_HEAD_DIM // 2)
    lane_head = lax.broadcasted_iota(I32, (w, LANES), 1) // ATT_HEAD_DIM
    bd_r = lax.broadcasted_iota(I32, (LANES, LANES), 0) // ATT_HEAD_DIM
    bd_c = lax.broadcasted_iota(I32, (LANES, LANES), 1) // ATT_HEAD_DIM
    head_ones = (bd_r == bd_c).astype(BF16)
    cos = cos_ref[...]
    sin = sin_ref[...]

    def norm_rope(x, g):
        hi, lo = _split_bf16(x * x)
        ss = _dot(hi, head_ones) + _dot(lo, head_ones)
        xn = x * lax.rsqrt(ss * (1.0 / ATT_HEAD_DIM) + EPS) * g
        rot = jnp.where(first_half, pltpu.roll(xn, LANES - ATT_HEAD_DIM // 2, 1),
                        pltpu.roll(xn, ATT_HEAD_DIM // 2, 1))
        return xn * cos + rot * sin

    kall_ref[w:(blocks + 1) * w, :] = norm_rope(k_ref[...], kg_ref[...]).astype(BF16)
    vall_ref[w:(blocks + 1) * w, :] = v_ref[...].astype(BF16)
    qc = [norm_rope(q_ref[:, LANES * c:LANES * (c + 1)], qg_ref[...]) for c in range(q_heads // 2)]

    rows = group * w
    qi = lax.broadcasted_iota(I32, (rows, 2 * w), 0) % w
    kj = lax.broadcasted_iota(I32, (rows, 2 * w), 1)
    band = (kj > qi) & (kj <= qi + w)
    row_head = lax.broadcasted_iota(I32, (rows, 1), 0) // w

    for sb in range(blocks):
        blk = slice(sb * w, (sb + 1) * w)
        keys = kall_ref[sb * w:(sb + 2) * w, :]
        vals = vall_ref[sb * w:(sb + 2) * w, :]
        valid = (band & ((n > 0) | (kj >= w))) if sb == 0 else band
        outs = []
        for g in range(kv_heads):
            parts = []
            for hh in range(group):
                h = g * group + hh
                x = qc[h // 2][blk, :]
                if h % 2 != g:
                    x = pltpu.roll(x, ATT_HEAD_DIM, 1)
                parts.append(jnp.where(lane_head == g, x, 0.0))
            qs = jnp.concatenate(parts, axis=0).astype(BF16)
            s = _dot_nt(qs, keys) * (ATT_HEAD_DIM ** -0.5)
            s = jnp.where(valid, s, -1e30)
            sink = jnp.zeros((rows, 1), F32)
            for hh in range(group):
                sink = jnp.where(row_head == hh, sink_ref[g * group + hh], sink)
            m = jnp.maximum(jnp.max(s, axis=-1, keepdims=True), sink)
            p = jnp.exp(s - m)
            denom = jnp.sum(p, axis=-1, keepdims=True) + jnp.exp(sink - m)
            pv = _dot(p.astype(BF16), vals) / denom
            outs.extend(pv[w * hh:w * (hh + 1), :] for hh in range(group))

        for c in range(q_heads // 2):
            g = (2 * c) // group
            a, b = outs[2 * c], outs[2 * c + 1]
            if g == 1:
                a = pltpu.roll(a, ATT_HEAD_DIM, 1)
            else:
                b = pltpu.roll(b, ATT_HEAD_DIM, 1)
            o_ref[blk, LANES * c:LANES * (c + 1)] = jnp.where(lane_head == 0, a, b).astype(o_ref.dtype)


def swa(proj, sinks, cos, sin, qg, kg, bsz, seq, q_col, k_col, v_col, q_heads, kv_heads, blocks):
    w = ATT_WINDOW
    r = blocks * w
    nb = seq // r
    qw = q_heads * ATT_HEAD_DIM
    kern = functools.partial(_swa_kernel, q_heads=q_heads, kv_heads=kv_heads, blocks=blocks)
    return pl.pallas_call(
        kern,
        grid=(bsz, nb),
        in_specs=[pl.BlockSpec(memory_space=pltpu.SMEM),
                  pl.BlockSpec((r, qw), lambda b, i: (b * nb + i, q_col // qw)),
                  pl.BlockSpec((r, LANES), lambda b, i: (b * nb + i, k_col // LANES)),
                  pl.BlockSpec((r, LANES), lambda b, i: (b * nb + i, v_col // LANES)),
                  pl.BlockSpec((r, LANES), lambda b, i: (i, 0)),
                  pl.BlockSpec((r, LANES), lambda b, i: (i, 0)),
                  pl.BlockSpec((1, LANES), lambda b, i: (0, 0)),
                  pl.BlockSpec((1, LANES), lambda b, i: (0, 0))],
        out_specs=pl.BlockSpec((r, qw), lambda b, i: (b * nb + i, 0)),
        out_shape=jax.ShapeDtypeStruct((bsz * seq, qw), BF16),
        scratch_shapes=[pltpu.VMEM(((blocks + 1) * w, LANES), BF16), pltpu.VMEM(((blocks + 1) * w, LANES), BF16)],
        compiler_params=_params(2),
    )(sinks, proj, proj, proj, cos, sin, qg, kg)


def _mix_specs(ys, ws, tm, index_map):
    return ([pl.BlockSpec((tm, y.shape[1]), index_map(lambda i: (i, 0))) for y in ys]
            + [pl.BlockSpec(w.shape, index_map(lambda i: (0, 0))) for w in ws])


def _mix_residual(h_ref, mix_refs):
    n_in = len(mix_refs) // 2
    x = h_ref[...]
    for y_ref, w_ref in zip(mix_refs[:n_in], mix_refs[n_in:]):
        x = x + _dot(y_ref[...], w_ref[...])
    return x


def _ffn_kernel(h_ref, *refs):
    *mix_refs, g_ref, wg_ref, wu_ref, wd_ref, o_ref, xn_ref = refs
    j = pl.program_id(1)

    @pl.when(j == 0)
    def _():
        x = _mix_residual(h_ref, mix_refs)
        xn_ref[...] = _rms(x, g_ref[...]).astype(BF16)
        o_ref[...] = x

    xn = xn_ref[...]
    act = (_silu(_dot(xn, wg_ref[...])) * _dot(xn, wu_ref[...])).astype(BF16)
    o_ref[...] += _dot(act, wd_ref[...])


def ffn(h, ys, ws, g, wg, wu, wd, tm, tf):
    t, d = h.shape
    f = wg.shape[1]
    resident = dict(pipeline_mode=pl.Buffered(1)) if tf == f else {}
    return pl.pallas_call(
        _ffn_kernel,
        grid=(t // tm, f // tf),
        in_specs=([pl.BlockSpec((tm, d), lambda i, j: (i, 0))]
                  + _mix_specs(ys, ws, tm, lambda fn: lambda i, j: fn(i))
                  + [pl.BlockSpec((1, d), lambda i, j: (0, 0)),
                     pl.BlockSpec((d, tf), lambda i, j: (0, j), **resident),
                     pl.BlockSpec((d, tf), lambda i, j: (0, j), **resident),
                     pl.BlockSpec((tf, d), lambda i, j: (j, 0), **resident)]),
        out_specs=pl.BlockSpec((tm, d), lambda i, j: (i, 0)),
        out_shape=jax.ShapeDtypeStruct((t, d), F32),
        scratch_shapes=[pltpu.VMEM((tm, d), BF16)],
        compiler_params=_params(2),
    )(h, *ys, *ws, g.reshape(1, d), wg, wu, wd)


def _gdn_kernel(qkv_ref, gate_ref, ab_ref, cw_ref, alog_ref, dtb_ref, onorm_ref, o_ref,
                tail_ref, q_ref, k_ref, v_ref, gc_ref, beta_ref, uw_ref, aqk_ref, qd_ref, kd_ref, *state_refs,
                tc, heads):
    hd = GDN_HEAD_DIM
    c = GDN_CHUNK
    width = heads * hd

    @pl.when(pl.program_id(1) == 0)
    def _():
        tail_ref[...] = jnp.zeros_like(tail_ref)
        for s_ref in state_refs:
            s_ref[...] = jnp.zeros_like(s_ref)

    qkv = _silu(_causal_conv(tail_ref, qkv_ref[...], cw_ref[...], tc))
    for h in range(heads):
        q = qkv[:, hd * h:hd * (h + 1)]
        k = qkv[:, width + hd * h:width + hd * (h + 1)]
        q_ref[:, hd * h:hd * (h + 1)] = (q * lax.rsqrt(jnp.sum(q * q, axis=-1, keepdims=True) + EPS)
                                          * (hd ** -0.5))
        k_ref[:, hd * h:hd * (h + 1)] = k * lax.rsqrt(jnp.sum(k * k, axis=-1, keepdims=True) + EPS)
    v_ref[...] = qkv[:, 2 * width:]

    ab = ab_ref[...]
    g = -jnp.exp(alog_ref[...]) * jax.nn.softplus(ab + dtb_ref[...])
    beta_ref[...] = pltpu.roll(jax.nn.sigmoid(ab), LANES - heads, 1)
    row = lax.broadcasted_iota(I32, (tc, LANES), 0) % c
    d = 1
    while d < c:
        g = g + jnp.where(row >= d, pltpu.roll(g, d, 0), 0.0)
        d *= 2
    gc_ref[...] = g

    ri = lax.broadcasted_iota(I32, (c, c), 0)
    ci = lax.broadcasted_iota(I32, (c, c), 1)
    tri = ri >= ci
    strict = ri > ci
    eye = (ri == ci).astype(F32)

    hs = range(heads)
    cols = [slice(hd * h, hd * (h + 1)) for h in hs]
    n_chunks = tc // c
    rows = [slice(c * n, c * (n + 1)) for n in range(n_chunks)]

    for g0 in range(0, n_chunks, GDN_PREP_CHUNKS):
        items = [(n, h) for n in range(g0, min(g0 + GDN_PREP_CHUNKS, n_chunks)) for h in hs]
        gc_all = {n: gc_ref[rows[n], :] for n, _ in items}
        gr_all = {n: gc_all[n].T for n in gc_all}
        beta_all = {n: beta_ref[rows[n], :] for n in gc_all}
        gcol = {(n, h): gc_all[n][:, h:h + 1] for n, h in items}
        grow = {(n, h): gr_all[n][h:h + 1, :] for n, h in items}
        beta = {(n, h): beta_all[n][:, h:h + 1] for n, h in items}
        qk_kk = {(n, h): _dot_nt(jnp.concatenate([q_ref[rows[n], cols[h]], k_ref[rows[n], cols[h]]],
                                                 axis=0).astype(BF16), k_ref[rows[n], cols[h]].astype(BF16))
                 for n, h in items}
        decay = {it: jnp.where(tri, jnp.exp(jnp.where(tri, gcol[it] - grow[it], 0.0)), 0.0) for it in items}
        m = {it: -jnp.where(strict, beta[it] * qk_kk[it][c:, :] * decay[it], 0.0) for it in items}
        p = {it: eye + m[it] for it in items}
        e = {it: _dot(m[it].astype(BF16), m[it].astype(BF16)) for it in items}
        lvl = 2
        while 2 * lvl < c:
            r = {it: _dot(jnp.concatenate([p[it], e[it]], axis=0).astype(BF16), e[it].astype(BF16)) for it in items}
            p = {it: p[it] + r[it][:c, :] for it in items}
            e = {it: r[it][c:, :] for it in items}
            lvl *= 2
        p = {it: p[it] + _dot(p[it].astype(BF16), e[it].astype(BF16)) for it in items}
        for n, h in items:
            it = (n, h)
            k = k_ref[rows[n], cols[h]]
            eg = jnp.exp(gcol[it])
            rhs = jnp.concatenate([beta[it] * v_ref[rows[n], cols[h]], (beta[it] * eg) * k], axis=1)
            uw_ref[rows[n], 2 * hd * h:2 * hd * (h + 1)] = _dot(p[it].astype(BF16), rhs.astype(BF16))
            aqk_ref[rows[n], LANES * h:LANES * h + c] = jnp.where(tri, qk_kk[it][:c, :] * decay[it], 0.0)
            qd_ref[rows[n], cols[h]] = q_ref[rows[n], cols[h]] * eg
            kd_ref[rows[n], cols[h]] = k * jnp.exp(gcol[it][c - 1:c, :] - gcol[it])

    for n in range(n_chunks):
        gc_n = gc_ref[rows[n], :]
        g_last = [jnp.exp(gc_n[c - 1:c, h:h + 1]) for h in hs]
        s = [state_refs[h][...] for h in hs]
        ws_qs = [_dot(jnp.concatenate([uw_ref[rows[n], 2 * hd * h + hd:2 * hd * (h + 1)], qd_ref[rows[n], cols[h]]],
                                      axis=0).astype(BF16), s[h].astype(BF16)) for h in hs]
        vb = [(uw_ref[rows[n], 2 * hd * h:2 * hd * h + hd] - ws_qs[h][:c, :]).astype(BF16) for h in hs]
        o = [ws_qs[h][c:, :] + _dot(aqk_ref[rows[n], LANES * h:LANES * h + c].astype(BF16), vb[h]) for h in hs]
        s_new = [s[h] * g_last[h] + _dot(kd_ref[rows[n], cols[h]].T.astype(BF16), vb[h]) for h in hs]
        for h in hs:
            state_refs[h][...] = s_new[h]
            on = o[h] * lax.rsqrt(jnp.mean(o[h] * o[h], axis=-1, keepdims=True) + EPS) * onorm_ref[...]
            o_ref[rows[n], cols[h]] = (on * _silu(gate_ref[rows[n], cols[h]])).astype(o_ref.dtype)


def gdn(proj, cw, alog, dtb, onorm, bsz, seq, heads, tc):
    nt = seq // tc
    width = heads * GDN_HEAD_DIM
    kern = functools.partial(_gdn_kernel, tc=tc, heads=heads)
    row = lambda n: pl.BlockSpec((1, n), lambda b, i: (0, 0))
    return pl.pallas_call(
        kern,
        grid=(bsz, nt),
        in_specs=[pl.BlockSpec((tc, 3 * width), lambda b, i: (b * nt + i, 0)),
                  pl.BlockSpec((tc, width), lambda b, i: (b * nt + i, 3)),
                  pl.BlockSpec((tc, LANES), lambda b, i: (b * nt + i, 4 * width // LANES)),
                  pl.BlockSpec((CONV_WIDTH, 3 * width), lambda b, i: (0, 0)),
                  row(LANES), row(LANES), row(GDN_HEAD_DIM)],
        out_specs=pl.BlockSpec((tc, width), lambda b, i: (b * nt + i, 0)),
        out_shape=jax.ShapeDtypeStruct((bsz * seq, width), BF16),
        scratch_shapes=[pltpu.VMEM((SUBLANES, 3 * width), F32),
                        pltpu.VMEM((tc, width), F32), pltpu.VMEM((tc, width), F32),
                        pltpu.VMEM((tc, width), F32),
                        pltpu.VMEM((tc, LANES), F32), pltpu.VMEM((tc, LANES), F32),
                        pltpu.VMEM((tc, 2 * width), F32), pltpu.VMEM((tc, heads * LANES), F32),
                        pltpu.VMEM((tc, width), F32), pltpu.VMEM((tc, width), F32),
                        *[pltpu.VMEM((GDN_HEAD_DIM, GDN_HEAD_DIM), F32) for _ in range(heads)]],
        compiler_params=_params(2),
    )(proj, proj, proj, cw, alog, dtb, onorm.reshape(1, -1))


def _router_kernel(h_ref, *refs, tm):
    *mix_refs, g_ref, r_ref, x_ref, route_ref, cnt_ref = refs
    x = _mix_residual(h_ref, mix_refs)
    x_ref[...] = x
    t_hi, t_lo = _split_bf16(_rms(x, g_ref[...]))
    r_hi, r_lo = _split_bf16(r_ref[...])
    logits = _dot(t_hi, r_hi) + (_dot(t_lo, r_hi) + _dot(t_hi, r_lo))
    lane = lax.broadcasted_iota(I32, (tm, LANES), 1)
    lane_f = lane.astype(F32)
    neg = -jnp.inf
    lg = jnp.where(lane < N_EXPERTS, logits, neg)
    m1 = jnp.max(lg, axis=-1, keepdims=True)
    i1 = jnp.min(jnp.where(lg == m1, lane_f, float(LANES)), axis=-1, keepdims=True)
    oh1 = lane_f == i1
    lg2 = jnp.where(oh1, neg, lg)
    m2 = jnp.max(lg2, axis=-1, keepdims=True)
    i2 = jnp.min(jnp.where(lg2 == m2, lane_f, float(LANES)), axis=-1, keepdims=True)
    oh2 = lane_f == i2
    e2 = jnp.exp(m2 - m1)
    w1 = 1.0 / (1.0 + e2)
    w2 = e2 / (1.0 + e2)

    both = (oh1 | oh2).astype(BF16)
    ri = lax.broadcasted_iota(I32, (tm, tm), 0)
    ci = lax.broadcasted_iota(I32, (tm, tm), 1)
    pos = _dot((ri > ci).astype(BF16), both)
    rank1 = jnp.sum(jnp.where(oh1, pos, 0.0), axis=-1, keepdims=True)
    rank2 = jnp.sum(jnp.where(oh2, pos, 0.0), axis=-1, keepdims=True)
    cnt_ref[0] = jnp.broadcast_to(jnp.sum(both.astype(F32), axis=0, keepdims=True), (SUBLANES, LANES))

    out = jnp.where(lane == 0, i1, 0.0)
    out = jnp.where(lane == 1, i2, out)
    out = jnp.where(lane == 2, rank1, out)
    out = jnp.where(lane == 3, rank2, out)
    out = jnp.where(lane == 4, w1, out)
    route_ref[...] = jnp.where(lane == 5, w2, out)


def router(h, ys, ws, g, r, tm):
    t, d = h.shape
    kern = functools.partial(_router_kernel, tm=tm)
    return pl.pallas_call(
        kern,
        grid=(t // tm,),
        in_specs=([pl.BlockSpec((tm, d), lambda i: (i, 0))]
                  + _mix_specs(ys, ws, tm, lambda fn: fn)
                  + [pl.BlockSpec((1, d), lambda i: (0, 0)),
                     pl.BlockSpec((d, LANES), lambda i: (0, 0))]),
        out_specs=[pl.BlockSpec((tm, d), lambda i: (i, 0)),
                   pl.BlockSpec((tm, LANES), lambda i: (i, 0)),
                   pl.BlockSpec((1, SUBLANES, LANES), lambda i: (i, 0, 0))],
        out_shape=[jax.ShapeDtypeStruct((t, d), F32),
                   jax.ShapeDtypeStruct((t, LANES), F32),
                   jax.ShapeDtypeStruct((t // tm, SUBLANES, LANES), F32)],
        compiler_params=_params(1),
    )(h, *ys, *ws, g.reshape(1, d), r)


def _local_positions(route, off_row):
    lane_f = lax.broadcasted_iota(I32, route.shape, 1).astype(F32)
    lpos1 = jnp.sum(jnp.where(lane_f == route[:, 0:1], off_row, 0.0), axis=-1, keepdims=True) + route[:, 2:3]
    lpos2 = jnp.sum(jnp.where(lane_f == route[:, 1:2], off_row, 0.0), axis=-1, keepdims=True) + route[:, 3:4]
    return lpos1, lpos2


def _run_copies(n_rows, max_bit, make):
    out = []
    for k in range(max_bit, -1, -1):
        bit = k + 3
        offset = pl.multiple_of((n_rows >> (bit + 1)) << (bit + 1), SUBLANES)
        out.append((((n_rows >> bit) & 1) == 1, make(offset, SUBLANES << k)))
    return out


def _start_all(copies):
    for pred, cp in copies:
        pl.when(pred)(cp.start)


def _wait_all(copies):
    for pred, cp in copies:
        pl.when(pred)(cp.wait)


def _dispatch_kernel(seg_ref, cnt_ref, off_ref, tail_ref, h_ref, g_ref, route_ref, offrow_ref, x_hbm,
                     local, zeros, sem, zsem, *, tm, max_bit, tile_rows):
    b = pl.program_id(0)
    nb = pl.num_programs(0)
    slot = b % 2

    def block_copies(blk, s):
        out = []
        for e in range(N_EXPERTS):
            seg, off = seg_ref[blk * N_EXPERTS + e], off_ref[blk * N_EXPERTS + e]
            out += _run_copies(cnt_ref[blk * N_EXPERTS + e], max_bit, lambda o, n, seg=seg, off=off:
                               pltpu.make_async_copy(local.at[s, pl.ds(pl.multiple_of(off + o, SUBLANES), n)],
                                                     x_hbm.at[pl.ds(pl.multiple_of(seg + o, SUBLANES), n)],
                                                     sem.at[s]))
        return out

    def tail_copies():
        out = []
        for e in range(N_EXPERTS):
            start = tail_ref[e]
            out += _run_copies(tail_ref[N_EXPERTS + e], max_bit - 1, lambda o, n, start=start:
                               pltpu.make_async_copy(zeros.at[pl.ds(0, n)],
                                                     x_hbm.at[pl.ds(pl.multiple_of(start + o, SUBLANES), n)], zsem))
        return out

    @pl.when(b >= 2)
    def _():
        _wait_all(block_copies(b - 2, slot))

    t = _rms(h_ref[...], g_ref[...]).astype(BF16)
    lpos1, lpos2 = _local_positions(route_ref[...], offrow_ref[0])
    rows_f = lax.broadcasted_iota(I32, (tm, local.shape[1]), 1).astype(F32)
    place = ((rows_f == lpos1) | (rows_f == lpos2)).astype(BF16)
    local[slot] = lax.dot_general(place, t, (((0,), (0,)), ((), ())), preferred_element_type=F32)
    _start_all(block_copies(b, slot))

    @pl.when(b == nb - 1)
    def _():
        zeros[...] = jnp.zeros_like(zeros)
        _start_all(tail_copies())

        @pl.when(b >= 1)
        def _():
            _wait_all(block_copies(b - 1, 1 - slot))

        _wait_all(block_copies(b, slot))
        _wait_all(tail_copies())

        def fill_idle_tile(tile, carry):
            cp = pltpu.make_async_copy(zeros, x_hbm.at[pl.ds(pl.multiple_of(tile * tile_rows, tile_rows), tile_rows)],
                                       zsem)
            cp.start()
            cp.wait()
            return carry
        lax.fori_loop(tail_ref[2 * N_EXPERTS], x_hbm.shape[0] // tile_rows, fill_idle_tile, 0)


def moe_dispatch(seg, cnt8, off, tail, h, g, route, off_rows, n_rows, tm, local_rows, tile_rows):
    t, d = h.shape
    max_bit = (tm // SUBLANES).bit_length() - 1
    assert tile_rows >= tm // 2 and n_rows % tile_rows == 0
    kern = functools.partial(_dispatch_kernel, tm=tm, max_bit=max_bit, tile_rows=tile_rows)
    grid_spec = pltpu.PrefetchScalarGridSpec(
        num_scalar_prefetch=4,
        grid=(t // tm,),
        in_specs=[pl.BlockSpec((tm, d), lambda i, *_: (i, 0)),
                  pl.BlockSpec((1, d), lambda i, *_: (0, 0)),
                  pl.BlockSpec((tm, LANES), lambda i, *_: (i, 0)),
                  pl.BlockSpec((1, 1, LANES), lambda i, *_: (i, 0, 0))],
        out_specs=pl.BlockSpec(memory_space=pl.ANY),
        scratch_shapes=[pltpu.VMEM((2, local_rows, d), F32), pltpu.VMEM((tile_rows, d), F32),
                        pltpu.SemaphoreType.DMA((2,)), pltpu.SemaphoreType.DMA(())],
    )
    return pl.pallas_call(
        kern,
        grid_spec=grid_spec,
        out_shape=jax.ShapeDtypeStruct((n_rows, d), F32),
        compiler_params=_params(1),
    )(seg, cnt8, off, tail, h, g.reshape(1, d), route, off_rows)


def _moe_kernel(te_ref, nu_ref, x_ref, wg_ref, wu_ref, wd_ref, y_ref, xn_ref):
    i = pl.program_id(0)
    j = pl.program_id(1)
    n_used = nu_ref[0]

    @pl.when(i < n_used)
    def _():
        @pl.when(j == 0)
        def _():
            xn_ref[...] = x_ref[...].astype(BF16)

        xn = xn_ref[...]
        act = (_silu(_dot(xn, wg_ref[0])) * _dot(xn, wu_ref[0])).astype(BF16)
        part = _dot(act, wd_ref[0])

        @pl.when(j == 0)
        def _():
            y_ref[...] = part

        @pl.when(j > 0)
        def _():
            y_ref[...] += part

    @pl.when((i >= n_used) & (j == 0))
    def _():
        y_ref[...] = jnp.zeros_like(y_ref)


def moe_experts(tile_expert, n_used, x, wg, wu, wd, tm, tf):
    n_rows, d = x.shape
    f = wg.shape[2]
    n_ff = f // tf

    def ff_idx(i, j, nu):
        return jnp.where(i < nu[0], j, n_ff - 1)

    grid_spec = pltpu.PrefetchScalarGridSpec(
        num_scalar_prefetch=2,
        grid=(n_rows // tm, n_ff),
        in_specs=[pl.BlockSpec((tm, d), lambda i, j, te, nu: (jnp.minimum(i, nu[0] - 1), 0)),
                  pl.BlockSpec((1, d, tf), lambda i, j, te, nu: (te[i], 0, ff_idx(i, j, nu))),
                  pl.BlockSpec((1, d, tf), lambda i, j, te, nu: (te[i], 0, ff_idx(i, j, nu))),
                  pl.BlockSpec((1, tf, d), lambda i, j, te, nu: (te[i], ff_idx(i, j, nu), 0))],
        out_specs=pl.BlockSpec((tm, d), lambda i, j, te, nu: (i, 0)),
        scratch_shapes=[pltpu.VMEM((tm, d), BF16)],
    )
    return pl.pallas_call(
        _moe_kernel,
        grid_spec=grid_spec,
        out_shape=jax.ShapeDtypeStruct((n_rows, d), F32),
        compiler_params=_params(2),
    )(tile_expert, n_used, x, wg, wu, wd)


def _combine_kernel(seg_ref, cnt_ref, off_ref, h_ref, route_ref, offrow_ref, y_hbm, o_ref, local, sem,
                    *, tm, max_bit):
    b = pl.program_id(0)
    nb = pl.num_programs(0)
    slot = b % 2

    def block_copies(blk, s):
        out = []
        for e in range(N_EXPERTS):
            seg, off = seg_ref[blk * N_EXPERTS + e], off_ref[blk * N_EXPERTS + e]
            out += _run_copies(cnt_ref[blk * N_EXPERTS + e], max_bit, lambda o, n, seg=seg, off=off:
                               pltpu.make_async_copy(y_hbm.at[pl.ds(pl.multiple_of(seg + o, SUBLANES), n)],
                                                     local.at[s, pl.ds(pl.multiple_of(off + o, SUBLANES), n)],
                                                     sem.at[s]))
        return out

    @pl.when(b == 0)
    def _():
        local[...] = jnp.zeros_like(local)
        _start_all(block_copies(0, 0))

    _wait_all(block_copies(b, slot))

    @pl.when(b + 1 < nb)
    def _():
        _start_all(block_copies(b + 1, 1 - slot))

    route = route_ref[...]
    lpos1, lpos2 = _local_positions(route, offrow_ref[0])
    rows_f = lax.broadcasted_iota(I32, (tm, local.shape[1]), 1).astype(F32)
    gate = jnp.where(rows_f == lpos1, route[:, 4:5], 0.0) + jnp.where(rows_f == lpos2, route[:, 5:6], 0.0)
    g_hi, g_lo = _split_bf16(gate)
    y = local[slot].astype(BF16)
    o_ref[...] = h_ref[...] + (_dot(g_hi, y) + _dot(g_lo, y))


def moe_combine(seg, cnt8, off, h, route, off_rows, y, tm, local_rows):
    t, d = h.shape
    max_bit = (tm // SUBLANES).bit_length() - 1
    kern = functools.partial(_combine_kernel, tm=tm, max_bit=max_bit)
    grid_spec = pltpu.PrefetchScalarGridSpec(
        num_scalar_prefetch=3,
        grid=(t // tm,),
        in_specs=[pl.BlockSpec((tm, d), lambda i, *_: (i, 0)),
                  pl.BlockSpec((tm, LANES), lambda i, *_: (i, 0)),
                  pl.BlockSpec((1, 1, LANES), lambda i, *_: (i, 0, 0)),
                  pl.BlockSpec(memory_space=pl.ANY)],
        out_specs=pl.BlockSpec((tm, d), lambda i, *_: (i, 0)),
        scratch_shapes=[pltpu.VMEM((2, local_rows, d), F32), pltpu.SemaphoreType.DMA((2,))],
    )
    return pl.pallas_call(
        kern,
        grid_spec=grid_spec,
        out_shape=jax.ShapeDtypeStruct((t, d), F32),
        compiler_params=_params(1),
    )(seg, cnt8, off, h, route, off_rows, y)


def _rope_tables(seq):
    half = ATT_HEAD_DIM // 2
    inv_freq = ROPE_THETA ** (-jnp.arange(half, dtype=F32) / half)
    ang = jnp.arange(seq, dtype=jnp.int32).astype(F32)[:, None] * inv_freq[None, :]
    reps = LANES // half
    cos = jnp.tile(jnp.cos(ang), (1, reps))
    sign = jnp.tile(jnp.concatenate([-jnp.ones((half,), F32), jnp.ones((half,), F32)]), LANES // ATT_HEAD_DIM)
    sin = jnp.tile(jnp.sin(ang), (1, reps)) * sign[None, :]
    return cos, sin


def _even_layer(h, bsz, seq, ln_mix, ln_ffn, w_in, conv_w, conv_b, w_a, b_a, w_i, b_i, lam,
                q_norm, k_norm, sinks, w_out, f_gate, f_up, f_down):
    lru_w = conv_w.shape[1]
    q_heads = sinks.shape[0]
    q_w = q_heads * ATT_HEAD_DIM
    kv_w = (w_in.shape[1] - 2 * lru_w - q_w) // 2
    kv_heads = kv_w // ATT_HEAD_DIM

    proj = rms_matmul(h, ln_mix, w_in.astype(BF16), tm=PROJ_TILE, tn=w_in.shape[1])
    gates_w = jnp.concatenate([block_diag(*w_a), block_diag(*w_i)], axis=1).astype(BF16)
    y_lru = rglru(proj, conv_w, conv_b, gates_w, jnp.concatenate([b_a, b_i]), lam, bsz, seq, lru_w,
                  tm=RGLRU_TILE)
    cos, sin = _rope_tables(seq)
    tile2 = lambda g: jnp.tile(g, LANES // ATT_HEAD_DIM).reshape(1, LANES)
    y_att = swa(proj, sinks, cos, sin, tile2(q_norm), tile2(k_norm), bsz, seq,
                q_col=2 * lru_w, k_col=2 * lru_w + q_w, v_col=2 * lru_w + q_w + kv_w,
                q_heads=q_heads, kv_heads=kv_heads, blocks=SWA_BLOCKS)
    w_out = w_out.astype(BF16)
    return ffn(h, [y_lru, y_att], [w_out[:lru_w], w_out[lru_w:]], ln_ffn, f_gate.astype(BF16), f_up.astype(BF16),
               f_down.astype(BF16), tm=FFN_TILE, tf=FFN_FF_CHUNK)


def _odd_layer(h, bsz, seq, ln_mix, ln_ffn, w_in, conv_w, a_log, dt_bias, out_norm, w_out, router_w,
               m_gate, m_up, m_down):
    t, d = h.shape
    heads = a_log.shape[0]
    width = heads * GDN_HEAD_DIM
    cols = w_in.shape[1]
    pad = (-cols) % (11 * LANES)
    w_in_p = jnp.pad(w_in, ((0, 0), (0, pad))).astype(BF16)
    proj = rms_matmul(h, ln_mix, w_in_p, tm=PROJ_TILE, tn=w_in_p.shape[1])
    lane_pad = lambda v: jnp.pad(v, (0, LANES - heads)).reshape(1, LANES)
    y = gdn(proj, conv_w, lane_pad(a_log), lane_pad(dt_bias), out_norm, bsz, seq, heads, tc=GDN_TILE)
    tb = MOE_BLOCK
    tm = MOE_TILE
    nblk = t // tb
    h, route, cnt = router(h, [y], [w_out.astype(BF16)], ln_ffn,
                           jnp.pad(router_w, ((0, 0), (0, LANES - N_EXPERTS))), tm=tb)
    cnt8 = (cnt[:, 0, :N_EXPERTS].astype(I32) + SUBLANES - 1) // SUBLANES * SUBLANES
    total8 = jnp.sum(cnt8, axis=0)
    tiles_per = (total8 + tm - 1) // tm
    tile_end = jnp.cumsum(tiles_per)
    starts = (tile_end - tiles_per) * tm
    n_used = tile_end[-1:]
    n_tiles = (2 * t + nblk * N_EXPERTS * (SUBLANES - 1)) // tm + N_EXPERTS
    tile_ids = jnp.minimum(jnp.arange(n_tiles, dtype=I32), n_used - 1)
    tile_expert = jnp.sum((tile_ids[:, None] >= tile_end[None, :]).astype(I32), axis=1)
    seg = (starts[None, :] + jnp.cumsum(cnt8, axis=0) - cnt8).reshape(-1)
    off = jnp.cumsum(cnt8, axis=1) - cnt8
    off_rows = jnp.pad(off.astype(F32), ((0, 0), (0, LANES - N_EXPERTS))).reshape(nblk, 1, LANES)
    tail = jnp.concatenate([starts + total8, tiles_per * tm - total8, n_used])
    local_rows = -(-(2 * tb + N_EXPERTS * (SUBLANES - 1)) // LANES) * LANES
    cnt8, off = cnt8.reshape(-1), off.reshape(-1)

    x = moe_dispatch(seg, cnt8, off, tail, h, ln_ffn, route, off_rows, n_tiles * tm, tb, local_rows, tm)
    y = moe_experts(tile_expert, n_used, x, m_gate.astype(BF16), m_up.astype(BF16), m_down.astype(BF16),
                    tm, MOE_FF_CHUNK)
    return moe_combine(seg, cnt8, off, h, route, off_rows, y, tb, local_rows)


def kernel(x, ln_mix, ln_ffn, e_w_in, e_lru_conv_w, e_lru_conv_b, e_lru_w_a, e_lru_b_a, e_lru_w_i, e_lru_b_i, e_lru_lambda, e_q_norm, e_k_norm, e_sinks, e_w_out, e_ffn_w_gate, e_ffn_w_up, e_ffn_w_down, o_w_in, o_conv_w, o_a_log, o_dt_bias, o_out_norm, o_w_out, o_router, o_moe_w_gate, o_moe_w_up, o_moe_w_down):
    bsz, seq, d = x.shape
    h = x.reshape(bsz * seq, d)
    for layer in range(ln_mix.shape[0]):
        j = layer // 2
        if layer % 2 == 0:
            h = _even_layer(h, bsz, seq, ln_mix[layer], ln_ffn[layer], e_w_in[j], e_lru_conv_w[j],
                            e_lru_conv_b[j], e_lru_w_a[j], e_lru_b_a[j], e_lru_w_i[j], e_lru_b_i[j],
                            e_lru_lambda[j], e_q_norm[j], e_k_norm[j], e_sinks[j], e_w_out[j],
                            e_ffn_w_gate[j], e_ffn_w_up[j], e_ffn_w_down[j])
        else:
            h = _odd_layer(h, bsz, seq, ln_mix[layer], ln_ffn[layer], o_w_in[j], o_conv_w[j], o_a_log[j],
                           o_dt_bias[j], o_out_norm[j], o_w_out[j], o_router[j], o_moe_w_gate[j],
                           o_moe_w_up[j], o_moe_w_down[j])
    return h.reshape(bsz, seq, d)
```
